```python
import jax, jax.numpy as jnp
from jax import lax
import numpy as np

D_MODEL = 1024
BATCH = 4
SEQ = 4096
DEPTH = 1

CTX_LEN = 256
GRID_W = 64
HG_HEADS = 8
HG_HEAD_DIM = 128
HG_WIDTH = HG_HEADS * HG_HEAD_DIM
CONV_GROUPS = 8
CONV_WIDTH = 1024
CONV_K = 3
MIX_WIDTH = HG_WIDTH + CONV_WIDTH
HG_COLS = 5 * HG_WIDTH
CONV_COLS = 4 * CONV_WIDTH
IN_COLS = HG_COLS + CONV_COLS
CHUNK = 64
EPS = 1e-6

kernel_name = 'hymba_hgrn2_shortconv_prefix_dit'


def rmsnorm(x, w):
    xf = x.astype(jnp.float32)
    y = xf * lax.rsqrt(jnp.mean(xf * xf, axis=-1, keepdims=True) + EPS)
    return (y * w.astype(jnp.float32)).astype(x.dtype)


def to_heads(z):
    b, t, _ = z.shape
    return z.reshape(b, t, HG_HEADS, HG_HEAD_DIM).transpose(0, 2, 1, 3)


def from_heads(z):
    b, h, t, d = z.shape
    return z.transpose(0, 2, 1, 3).reshape(b, t, h * d)


def lower_bound(logits, layer):
    sm = jax.nn.softmax(logits.astype(jnp.float32), axis=0)
    return jnp.cumsum(sm, axis=0)[layer]


def chunk_scan(q, k, v, logf, s0):
    b_, h_, t_, dk = q.shape
    dv = v.shape[-1]
    n = t_ // CHUNK
    split = lambda a: a.reshape(b_, h_, n, CHUNK, a.shape[-1]).transpose(2, 0, 1, 3, 4)
    mask = jnp.tril(jnp.ones((CHUNK, CHUNK), dtype=bool))[:, :, None]

    def step(S, inp):
        qc, kc, vc, gc = inp
        bcum = jnp.cumsum(gc, axis=2)
        diff = bcum[:, :, :, None, :] - bcum[:, :, None, :, :]
        decay = jnp.where(mask, jnp.exp(jnp.where(mask, diff, 0.0)), 0.0)
        attn = jnp.einsum('bhtk,bhsk,bhtsk->bhts', qc, kc, decay)
        o = jnp.einsum('bhts,bhsv->bhtv', attn, vc) + jnp.einsum('bhtk,bhkv->bhtv', qc * jnp.exp(bcum), S)
        b_last = bcum[:, :, -1, :]
        S_new = jnp.exp(b_last)[..., None] * S + jnp.einsum(
            'bhsk,bhsv->bhkv', kc * jnp.exp(b_last[:, :, None, :] - bcum), vc)
        return S_new, o

    s_fin, os_ = lax.scan(step, s0, (split(q), split(k), split(v), split(logf)))
    o = os_.transpose(1, 2, 0, 3, 4).reshape(b_, h_, t_, dv)
    return o, s_fin


def bidir_scan(q, k_f, k_b, v, g_f, g_b, s_f, s_b):
    flip = lambda a: jnp.flip(a, axis=2)
    o_f, sf = chunk_scan(q, k_f, v, g_f, s_f)
    o_b, sb = chunk_scan(flip(q), flip(k_b), flip(v), flip(g_b), s_b)
    return o_f + flip(o_b), sf, sb


def hgrn_prep(p_hg, lb_f, lb_b):
    q, i, zf, zb = [p_hg[..., j * HG_WIDTH:(j + 1) * HG_WIDTH].astype(jnp.float32) for j in range(4)]
    f_f = lb_f + (1.0 - lb_f) * jax.nn.sigmoid(zf)
    f_b = lb_b + (1.0 - lb_b) * jax.nn.sigmoid(zb)
    return (to_heads(q), to_heads(i), to_heads(1.0 - f_f), to_heads(jnp.log(f_f)),
            to_heads(1.0 - f_b), to_heads(jnp.log(f_b)))


def hgrn_out(o, gate, onorm_w, dtype):
    o = o * lax.rsqrt(jnp.mean(o * o, axis=-1, keepdims=True) + EPS) * onorm_w.astype(jnp.float32)
    return from_heads(o).astype(dtype) * jax.nn.silu(gate)


def conv3(u, w):
    pad = [(0, 0)] * (u.ndim - 2) + [(1, 1), (0, 0)]
    up = jnp.pad(u, pad)
    return w[0] * up[..., :-2, :] + w[1] * up[..., 1:-1, :] + w[2] * up[..., 2:, :]


def conv_branch(p_cv, w, grid):
    h, bg, cg, gate = [p_cv[..., j * CONV_WIDTH:(j + 1) * CONV_WIDTH] for j in range(4)]
    u = cg * h
    if grid:
        b_, t_, ch = u.shape
        rows = t_ // GRID_W
        v = conv3(u.reshape(b_, rows, GRID_W, ch), w).reshape(b_, t_, ch)
    else:
        v = conv3(u, w)
    return bg * v * jax.nn.silu(gate)


def setup_inputs(seed: int = 0) -> dict:
    key = jax.random.key(seed)
    ks = jax.random.split(key, 13)
    f32 = jnp.float32
    x = jax.random.normal(ks[0], (BATCH, SEQ, D_MODEL), f32)
    c = jax.random.normal(ks[1], (BATCH, D_MODEL), f32)
    ctx = jax.random.normal(ks[2], (BATCH, CTX_LEN, D_MODEL), f32)
    c_ctx = jax.random.normal(ks[3], (D_MODEL,), f32)
    norm_w = 1.0 + 0.02 * jax.random.normal(ks[4], (DEPTH, D_MODEL), f32)
    w_ada = 0.5 * D_MODEL ** -0.5 * jax.random.normal(ks[5], (DEPTH, D_MODEL, 3 * D_MODEL), f32)
    b_ada = 0.02 * jax.random.normal(ks[6], (DEPTH, 3 * D_MODEL), f32)
    w_in = D_MODEL ** -0.5 * jax.random.normal(ks[7], (DEPTH, D_MODEL, IN_COLS), f32)
    hg_lb_logits = 0.1 * jax.random.normal(ks[8], (2, DEPTH + 1, HG_WIDTH), f32)
    hg_onorm_w = 1.0 + 0.02 * jax.random.normal(ks[9], (DEPTH, HG_HEAD_DIM), f32)
    conv_w = CONV_K ** -0.5 * jax.random.normal(ks[10], (DEPTH, CONV_K, CONV_WIDTH), f32)
    w_out = MIX_WIDTH ** -0.5 * jax.random.normal(ks[11], (DEPTH, MIX_WIDTH, D_MODEL), f32)
    final_norm_w = 1.0 + 0.02 * jax.random.normal(ks[12], (D_MODEL,), f32)
    return {'x': x, 'c': c, 'ctx': ctx, 'c_ctx': c_ctx, 'norm_w': norm_w, 'w_ada': w_ada,
            'b_ada': b_ada, 'w_in': w_in, 'hg_lb_logits': hg_lb_logits, 'hg_onorm_w': hg_onorm_w,
            'conv_w': conv_w, 'w_out': w_out, 'final_norm_w': final_norm_w}


def reference(x, c, ctx, c_ctx, norm_w, w_ada, b_ada, w_in, hg_lb_logits, hg_onorm_w,
              conv_w, w_out, final_norm_w):
    xc = ctx
    bsz = x.shape[0]
    for l in range(DEPTH):
        last = l == DEPTH - 1
        shift, scale, gate = jnp.split(jax.nn.silu(c) @ w_ada[l] + b_ada[l], 3, axis=-1)
        shift_c, scale_c, gate_c = jnp.split(jax.nn.silu(c_ctx) @ w_ada[l] + b_ada[l], 3, axis=-1)
        h = rmsnorm(x, norm_w[l]) * (1.0 + scale[:, None]) + shift[:, None]
        hc = rmsnorm(xc, norm_w[l]) * (1.0 + scale_c) + shift_c
        p = h @ w_in[l]
        pc = hc @ (w_in[l, :, :HG_COLS] if last else w_in[l])
        lb_f = lower_bound(hg_lb_logits[0], l)
        lb_b = lower_bound(hg_lb_logits[1], l)
        qc, vc, kcf, gcf, kcb, gcb = hgrn_prep(pc[..., :HG_COLS], lb_f, lb_b)
        s0 = jnp.zeros((bsz, HG_HEADS, HG_HEAD_DIM, HG_HEAD_DIM), jnp.float32)
        oc, s_f, s_b = bidir_scan(qc, kcf, kcb, vc, gcf, gcb, s0, s0)
        q, v, kf, gf, kb, gb = hgrn_prep(p[..., :HG_COLS], lb_f, lb_b)
        o, _, _ = bidir_scan(q, kf, kb, v, gf, gb, s_f, s_b)
        y_hg = hgrn_out(o, p[..., 4 * HG_WIDTH:HG_COLS], hg_onorm_w[l], x.dtype)
        y_cv = conv_branch(p[..., HG_COLS:], conv_w[l], grid=True)
        y = jnp.concatenate([y_hg, y_cv], axis=-1) @ w_out[l]
        if not last:
            yc_hg = hgrn_out(oc, pc[..., 4 * HG_WIDTH:HG_COLS], hg_onorm_w[l], xc.dtype)
            yc_cv = conv_branch(pc[..., HG_COLS:], conv_w[l], grid=False)
            yc = jnp.concatenate([yc_hg, yc_cv], axis=-1) @ w_out[l]
            xc = xc + gate_c * yc
        x = x + gate[:, None] * y
    return rmsnorm(x, final_norm_w)
```

```python
import functools

import jax
import jax.numpy as jnp
from jax import lax
from jax.experimental import pallas as pl
from jax.experimental.pallas import tpu as pltpu

D_MODEL = 1024
HEADS = 8
HEAD_DIM = 128
HG_WIDTH = HEADS * HEAD_DIM
CONV_WIDTH = 1024
GRID_W = 64
EPS = 1e-6

CHUNK = 64
TILE_T = 256
EXP_CLAMP = 60.0
VMEM_LIMIT = 56 * 1024 * 1024

F32 = jnp.float32
BF16 = jnp.bfloat16

_NT = (((1,), (1,)), ((), ()))
_TN = (((0,), (0,)), ((), ()))


def _sigmoid(x):
    return 1.0 / (1.0 + jnp.exp(-x))


def _rms_rows(xf):
    return lax.rsqrt(jnp.mean(xf * xf, axis=-1, keepdims=True) + EPS)


def _tri(n, reverse):
    t = lax.broadcasted_iota(jnp.int32, (n, n), 0)
    r = lax.broadcasted_iota(jnp.int32, (n, n), 1)
    return (r >= t) if reverse else (r <= t)


def _cumsum_time(g, tri2):
    g_hi = g.astype(BF16)
    g_lo = (g - g_hi.astype(F32)).astype(BF16)
    return jnp.dot(tri2, jnp.concatenate([g_hi, g_lo], axis=0), preferred_element_type=F32)


def _gates(z, lb):
    f = lb + (1.0 - lb) * _sigmoid(z)
    return jnp.log(f), 1.0 - f


def _mod_kernel(cc_ref, w_ref, b_ref, lg_ref, mod_ref, lb_ref):
    cc = cc_ref[...]
    s = cc * _sigmoid(cc)
    mod_ref[...] = jnp.dot(s, w_ref[...], precision=lax.Precision.HIGHEST,
                           preferred_element_type=F32) + b_ref[...]
    n_rows = lg_ref.shape[0] // 2
    for d in range(2):
        lg = lg_ref[d * n_rows:(d + 1) * n_rows, :]
        e = jnp.exp(lg - jnp.max(lg, axis=0, keepdims=True))
        lb_ref[d:d + 1, :] = e[0:1, :] / jnp.sum(e, axis=0, keepdims=True)


def _mod_call(cc, w_ada, b_ada, lb_logits):
    n_out = w_ada.shape[1]
    bn = 768
    return pl.pallas_call(
        _mod_kernel,
        grid=(n_out // bn,),
        in_specs=[
            pl.BlockSpec((8, D_MODEL), lambda j: (0, 0)),
            pl.BlockSpec((D_MODEL, bn), lambda j: (0, j)),
            pl.BlockSpec((1, bn), lambda j: (0, j)),
            pl.BlockSpec(lb_logits.shape, lambda j: (0, 0)),
        ],
        out_specs=[
            pl.BlockSpec((8, bn), lambda j: (0, j)),
            pl.BlockSpec((2, HG_WIDTH), lambda j: (0, 0)),
        ],
        out_shape=[
            jax.ShapeDtypeStruct((8, n_out), F32),
            jax.ShapeDtypeStruct((2, HG_WIDTH), F32),
        ],
        compiler_params=pltpu.CompilerParams(dimension_semantics=("arbitrary",)),
        name="adaln_mod",
    )(cc, w_ada, b_ada, lb_logits)


def _modulated_norm(x, nw, mod_ref):
    shift = mod_ref[0, 0:1, :]
    scale = mod_ref[0, 1:2, :]
    y = (x * _rms_rows(x)) * nw
    return (y * (1.0 + scale) + shift).astype(BF16)


def _ctx_kernel(x_ref, mod_ref, nw_ref, w_ref, lb_ref, sf_ref, sb_ref):
    n = x_ref.shape[1]
    h = _modulated_norm(x_ref[0], nw_ref[...], mod_ref)
    v = jnp.dot(h, w_ref[:, 0:HG_WIDTH], preferred_element_type=F32).astype(BF16)
    for d, out_ref in ((0, sf_ref), (1, sb_ref)):
        z = jnp.dot(h, w_ref[:, (1 + d) * HG_WIDTH:(2 + d) * HG_WIDTH], preferred_element_type=F32)
        g, kk = _gates(z, lb_ref[d:d + 1, :])
        tri = _tri(n, reverse=bool(d)).astype(BF16)
        bc = _cumsum_time(g, jnp.concatenate([tri, tri], axis=1))
        tot = bc[0:1, :] if d else bc[n - 1:n, :]
        kd = (kk * jnp.exp(tot - bc)).astype(BF16)
        for hd in range(HEADS):
            sl = slice(hd * HEAD_DIM, (hd + 1) * HEAD_DIM)
            out_ref[0, hd] = lax.dot_general(v[:, sl], kd[:, sl], _TN, preferred_element_type=F32)


def _ctx_call(ctx, mod3, nw, w_c, lb):
    bsz, n, _ = ctx.shape
    st_shape = jax.ShapeDtypeStruct((bsz, HEADS, HEAD_DIM, HEAD_DIM), F32)
    st_spec = pl.BlockSpec((1, HEADS, HEAD_DIM, HEAD_DIM), lambda b: (b, 0, 0, 0))
    return pl.pallas_call(
        _ctx_kernel,
        grid=(bsz,),
        in_specs=[
            pl.BlockSpec((1, n, D_MODEL), lambda b: (b, 0, 0)),
            pl.BlockSpec((1, 3, D_MODEL), lambda b: (bsz, 0, 0)),
            pl.BlockSpec((1, D_MODEL), lambda b: (0, 0)),
            pl.BlockSpec(w_c.shape, lambda b: (0, 0)),
            pl.BlockSpec((2, HG_WIDTH), lambda b: (0, 0)),
        ],
        out_specs=[st_spec, st_spec],
        out_shape=[st_shape, st_shape],
        compiler_params=pltpu.CompilerParams(dimension_semantics=("arbitrary",),
                                             vmem_limit_bytes=VMEM_LIMIT),
        name="ctx_states",
    )(ctx, mod3, nw, w_c, lb)


def _scan_tile(z_ref, q_ref, v_ref, lb, st_ref, emit, reverse):
    n_tok = z_ref.shape[0]
    n_chunks = n_tok // CHUNK
    tri = _tri(CHUNK, reverse)
    tri_b = tri.astype(BF16)
    tri2 = jnp.concatenate([tri_b, tri_b], axis=1)
    mid_row = CHUNK // 2 if reverse else CHUNK // 2 - 1
    tot_row = 0 if reverse else CHUNK - 1

    def body(ci, carry):
        n = (n_chunks - 1 - ci) if reverse else ci
        r0 = pl.multiple_of(n * CHUNK, CHUNK)
        rows = pl.ds(r0, CHUNK)
        g, kk = _gates(z_ref[rows, :], lb)
        bc = _cumsum_time(g, tri2)
        ref = bc[mid_row:mid_row + 1, :]
        tot = bc[tot_row:tot_row + 1, :]
        q = q_ref[rows, :].astype(F32)
        v = v_ref[rows, :]
        d = bc - ref
        qt = (q * jnp.exp(jnp.minimum(d, EXP_CLAMP))).astype(BF16)
        kt = (kk * jnp.exp(jnp.minimum(-d, EXP_CLAMP))).astype(BF16)
        qs = (q * jnp.exp(bc)).astype(BF16)
        kd = (kk * jnp.exp(tot - bc)).astype(BF16)
        dec = jnp.exp(tot)
        for hd in range(HEADS):
            sl = slice(hd * HEAD_DIM, (hd + 1) * HEAD_DIM)
            a = lax.dot_general(qt[:, sl], kt[:, sl], _NT, preferred_element_type=F32)
            a = jnp.where(tri, a, 0.0).astype(BF16)
            st = st_ref[hd]
            o = (jnp.dot(a, v[:, sl], preferred_element_type=F32)
                 + lax.dot_general(qs[:, sl], st.astype(BF16), _NT, preferred_element_type=F32))
            st_ref[hd] = st * dec[:, sl] + lax.dot_general(v[:, sl], kd[:, sl], _TN,
                                                           preferred_element_type=F32)
            emit(rows, sl, o)
        return carry

    lax.fori_loop(0, n_chunks, body, 0)


def _bwd_kernel(x_ref, mod_ref, nw_ref, w_ref, lb_ref, s0_ref,
                q_ref, v_ref, ob_ref, z_s, st_s):
    @pl.when(pl.program_id(1) == 0)
    def _():
        st_s[...] = s0_ref[0]

    h = _modulated_norm(x_ref[0], nw_ref[...], mod_ref)
    q_ref[0] = jnp.dot(h, w_ref[:, 0:HG_WIDTH], preferred_element_type=F32).astype(BF16)
    v_ref[0] = jnp.dot(h, w_ref[:, HG_WIDTH:2 * HG_WIDTH], preferred_element_type=F32).astype(BF16)
    z_s[...] = jnp.dot(h, w_ref[:, 2 * HG_WIDTH:3 * HG_WIDTH], preferred_element_type=F32)

    def emit(rows, sl, o):
        ob_ref[0, rows, sl] = o.astype(BF16)

    _scan_tile(z_s, q_ref.at[0], v_ref.at[0], lb_ref[1:2, :], st_s, emit, reverse=True)


def _bwd_call(x, mod3, nw, w_b, lb, sb):
    bsz, seq, _ = x.shape
    nt = seq // TILE_T
    tile = lambda b, t: (b, nt - 1 - t, 0)
    act = jax.ShapeDtypeStruct((bsz, seq, HG_WIDTH), BF16)
    act_spec = pl.BlockSpec((1, TILE_T, HG_WIDTH), tile)
    return pl.pallas_call(
        _bwd_kernel,
        grid=(bsz, nt),
        in_specs=[
            pl.BlockSpec((1, TILE_T, D_MODEL), tile),
            pl.BlockSpec((1, 3, D_MODEL), lambda b, t: (b, 0, 0)),
            pl.BlockSpec((1, D_MODEL), lambda b, t: (0, 0)),
            pl.BlockSpec(w_b.shape, lambda b, t: (0, 0)),
            pl.BlockSpec((2, HG_WIDTH), lambda b, t: (0, 0)),
            pl.BlockSpec((1, HEADS, HEAD_DIM, HEAD_DIM), lambda b, t: (b, 0, 0, 0)),
        ],
        out_specs=[act_spec, act_spec, act_spec],
        out_shape=[act, act, act],
        scratch_shapes=[
            pltpu.VMEM((TILE_T, HG_WIDTH), F32),
            pltpu.VMEM((HEADS, HEAD_DIM, HEAD_DIM), F32),
        ],
        compiler_params=pltpu.CompilerParams(dimension_semantics=("arbitrary", "arbitrary"),
                                             vmem_limit_bytes=VMEM_LIMIT),
        name="bwd_scan",
    )(x, mod3, nw, w_b, lb, sb)


def _fwd_kernel(x_ref, mod_ref, nw_ref, w_ref, lb_ref, s0_ref, q_ref, v_ref, ob_ref,
                onw_ref, cw_ref, wo_ref, fnw_ref, out_ref, z_s, o_s, mix_s, st_s):
    @pl.when(pl.program_id(1) == 0)
    def _():
        st_s[...] = s0_ref[0]

    x = x_ref[0]
    h = _modulated_norm(x, nw_ref[...], mod_ref)

    z_s[...] = jnp.dot(h, w_ref[:, 0:HG_WIDTH], preferred_element_type=F32)

    def emit(rows, sl, o):
        o_s[rows, sl] = o + ob_ref[0, rows, sl].astype(F32)

    _scan_tile(z_s, q_ref.at[0], v_ref.at[0], lb_ref[0:1, :], st_s, emit, reverse=False)

    gate = jnp.dot(h, w_ref[:, HG_WIDTH:2 * HG_WIDTH], preferred_element_type=F32)
    onw = onw_ref[...]
    for hd in range(HEADS):
        sl = slice(hd * HEAD_DIM, (hd + 1) * HEAD_DIM)
        o = o_s[:, sl]
        g = gate[:, sl]
        y = (o * _rms_rows(o) * onw[:, sl]) * (g * _sigmoid(g))
        mix_s[:, sl] = y.astype(BF16)

    c0 = 2 * HG_WIDTH
    cv = jnp.dot(h, w_ref[:, c0:c0 + 4 * CONV_WIDTH], preferred_element_type=F32)
    n_tok = cv.shape[0]
    u = cv[:, 2 * CONV_WIDTH:3 * CONV_WIDTH] * cv[:, 0:CONV_WIDTH]
    col = lax.broadcasted_iota(jnp.int32, (n_tok, 1), 0) % GRID_W
    u_prev = jnp.where(col == 0, 0.0, pltpu.roll(u, 1, axis=0))
    u_next = jnp.where(col == GRID_W - 1, 0.0, pltpu.roll(u, n_tok - 1, axis=0))
    conv = cw_ref[0:1, :] * u_prev + cw_ref[1:2, :] * u + cw_ref[2:3, :] * u_next
    cg = cv[:, 3 * CONV_WIDTH:4 * CONV_WIDTH]
    y_cv = cv[:, CONV_WIDTH:2 * CONV_WIDTH] * conv * (cg * _sigmoid(cg))
    mix_s[:, HG_WIDTH:] = y_cv.astype(BF16)

    y = jnp.dot(mix_s[...], wo_ref[...], preferred_element_type=F32)
    r = x + mod_ref[0, 2:3, :] * y
    out_ref[0] = (r * _rms_rows(r)) * fnw_ref[...]


def _fwd_call(x, mod3, nw, w_f, lb, sf, q, v, ob, onw, conv_w, w_o, fnw):
    bsz, seq, _ = x.shape
    nt = seq // TILE_T
    tile = lambda b, t: (b, t, 0)
    const2 = lambda b, t: (0, 0)
    act_spec = pl.BlockSpec((1, TILE_T, HG_WIDTH), tile)
    return pl.pallas_call(
        _fwd_kernel,
        grid=(bsz, nt),
        in_specs=[
            pl.BlockSpec((1, TILE_T, D_MODEL), tile),
            pl.BlockSpec((1, 3, D_MODEL), lambda b, t: (b, 0, 0)),
            pl.BlockSpec((1, D_MODEL), const2),
            pl.BlockSpec(w_f.shape, const2),
            pl.BlockSpec((2, HG_WIDTH), const2),
            pl.BlockSpec((1, HEADS, HEAD_DIM, HEAD_DIM), lambda b, t: (b, 0, 0, 0)),
            act_spec, act_spec, act_spec,
            pl.BlockSpec((1, HG_WIDTH), const2),
            pl.BlockSpec(conv_w.shape, const2),
            pl.BlockSpec(w_o.shape, const2),
            pl.BlockSpec((1, D_MODEL), const2),
        ],
        out_specs=pl.BlockSpec((1, TILE_T, D_MODEL), tile),
        out_shape=jax.ShapeDtypeStruct((bsz, seq, D_MODEL), F32),
        scratch_shapes=[
            pltpu.VMEM((TILE_T, HG_WIDTH), F32),
            pltpu.VMEM((TILE_T, HG_WIDTH), F32),
            pltpu.VMEM((TILE_T, HG_WIDTH + CONV_WIDTH), BF16),
            pltpu.VMEM((HEADS, HEAD_DIM, HEAD_DIM), F32),
        ],
        compiler_params=pltpu.CompilerParams(dimension_semantics=("arbitrary", "arbitrary"),
                                             vmem_limit_bytes=VMEM_LIMIT),
        name="fwd_scan_out",
    )(x, mod3, nw, w_f, lb, sf, q, v, ob, onw, conv_w, w_o, fnw)


def kernel(x, c, ctx, c_ctx, norm_w, w_ada, b_ada, w_in, hg_lb_logits, hg_onorm_w, conv_w,
           w_out, final_norm_w):
    assert w_in.shape[0] == 1, "single-layer block"
    bsz = x.shape[0]
    W = HG_WIDTH
    cc = jnp.concatenate([c, c_ctx[None, :], jnp.zeros((8 - bsz - 1, D_MODEL), F32)], axis=0)
    mod, lb = _mod_call(cc, w_ada[0], b_ada[0][None, :], hg_lb_logits.reshape(-1, W))
    mod3 = mod.reshape(8, 3, D_MODEL)

    wi = w_in[0].astype(BF16)
    w_c = jnp.concatenate([wi[:, W:2 * W], wi[:, 2 * W:3 * W], wi[:, 3 * W:4 * W]], axis=1)
    w_b = jnp.concatenate([wi[:, 0:2 * W], wi[:, 3 * W:4 * W]], axis=1)
    w_f = jnp.concatenate([wi[:, 2 * W:3 * W], wi[:, 4 * W:]], axis=1)
    nw = norm_w[0][None, :]

    sf, sb = _ctx_call(ctx, mod3, nw, w_c, lb)
    q, v, ob = _bwd_call(x, mod3, nw, w_b, lb, sb)
    onw = jnp.tile(hg_onorm_w[0], HEADS)[None, :]
    return _fwd_call(x, mod3, nw, w_f, lb, sf, q, v, ob, onw, conv_w[0],
                     w_out[0].astype(BF16), final_norm_w[None, :])
```

```python
import jax
import jax.numpy as jnp
from jax import lax
from jax.experimental import pallas as pl
from jax.experimental.pallas import tpu as pltpu

D_MODEL = 1024
HEADS = 8
HEAD_DIM = 128
HG_WIDTH = HEADS * HEAD_DIM
CONV_WIDTH = 1024
GRID_W = 64
EPS = 1e-6

CHUNK = 128
BLK = 16
NBLK = CHUNK // BLK
LEVELS = NBLK.bit_length() - 1
N_OPS = 4 + 2 * LEVELS
TILE_T = 256
EXP_CLAMP = 70.0
VMEM_LIMIT = 56 * 1024 * 1024

F32 = jnp.float32
BF16 = jnp.bfloat16

_NT = (((1,), (1,)), ((), ()))
_TN = (((0,), (0,)), ((), ()))


def _sigmoid(x):
    return 1.0 / (1.0 + jnp.exp(-x))


def _rms_rows(xf):
    return lax.rsqrt(jnp.mean(xf * xf, axis=-1, keepdims=True) + EPS)


def _tri(n, reverse):
    t = lax.broadcasted_iota(jnp.int32, (n, n), 0)
    r = lax.broadcasted_iota(jnp.int32, (n, n), 1)
    return (r >= t) if reverse else (r <= t)


def _cumsum_time(g, tri2):
    g_hi = g.astype(BF16)
    g_lo = (g - g_hi.astype(F32)).astype(BF16)
    return jnp.dot(tri2, jnp.concatenate([g_hi, g_lo], axis=0), preferred_element_type=F32)


def _gates(z, lb):
    f = lb + (1.0 - lb) * _sigmoid(z)
    return jnp.log(f), 1.0 - f


def _mod_kernel(cc_ref, w_ref, b_ref, lg_ref, mod_ref, lb_ref):
    cc = cc_ref[...]
    s = cc * _sigmoid(cc)
    mod_ref[...] = jnp.dot(s, w_ref[...], precision=lax.Precision.HIGHEST,
                           preferred_element_type=F32) + b_ref[...]
    n_rows = lg_ref.shape[0] // 2
    for d in range(2):
        lg = lg_ref[d * n_rows:(d + 1) * n_rows, :]
        e = jnp.exp(lg - jnp.max(lg, axis=0, keepdims=True))
        lb_ref[d:d + 1, :] = e[0:1, :] / jnp.sum(e, axis=0, keepdims=True)


def _mod_call(cc, w_ada, b_ada, lb_logits):
    n_out = w_ada.shape[1]
    bn = 768
    return pl.pallas_call(
        _mod_kernel,
        grid=(n_out // bn,),
        in_specs=[
            pl.BlockSpec((8, D_MODEL), lambda j: (0, 0)),
            pl.BlockSpec((D_MODEL, bn), lambda j: (0, j)),
            pl.BlockSpec((1, bn), lambda j: (0, j)),
            pl.BlockSpec(lb_logits.shape, lambda j: (0, 0)),
        ],
        out_specs=[
            pl.BlockSpec((8, bn), lambda j: (0, j)),
            pl.BlockSpec((2, HG_WIDTH), lambda j: (0, 0)),
        ],
        out_shape=[
            jax.ShapeDtypeStruct((8, n_out), F32),
            jax.ShapeDtypeStruct((2, HG_WIDTH), F32),
        ],
        compiler_params=pltpu.CompilerParams(dimension_semantics=("arbitrary",)),
        name="adaln_mod",
    )(cc, w_ada, b_ada, lb_logits)


def _modulated_norm(x, nw, mod_ref):
    shift = mod_ref[0, 0:1, :]
    scale = mod_ref[0, 1:2, :]
    y = (x * _rms_rows(x)) * nw
    return (y * (1.0 + scale) + shift).astype(BF16)


def _ctx_kernel(x_ref, mod_ref, nw_ref, w_ref, lb_ref, sf_ref, sb_ref):
    n = x_ref.shape[1]
    h = _modulated_norm(x_ref[0], nw_ref[...], mod_ref)
    v = jnp.dot(h, w_ref[:, 0:HG_WIDTH], preferred_element_type=F32).astype(BF16)
    for d, out_ref in ((0, sf_ref), (1, sb_ref)):
        z = jnp.dot(h, w_ref[:, (1 + d) * HG_WIDTH:(2 + d) * HG_WIDTH], preferred_element_type=F32)
        g, kk = _gates(z, lb_ref[d:d + 1, :])
        tri = _tri(n, reverse=bool(d)).astype(BF16)
        bc = _cumsum_time(g, jnp.concatenate([tri, tri], axis=1))
        tot = bc[0:1, :] if d else bc[n - 1:n, :]
        ks = (kk * jnp.exp(tot - bc)).astype(BF16)
        for hd in range(HEADS):
            sl = slice(hd * HEAD_DIM, (hd + 1) * HEAD_DIM)
            out_ref[0, hd] = lax.dot_general(ks[:, sl], v[:, sl], _TN, preferred_element_type=F32)


def _ctx_call(ctx, mod3, nw, w_c, lb):
    bsz, n, _ = ctx.shape
    st_shape = jax.ShapeDtypeStruct((bsz, HEADS, HEAD_DIM, HEAD_DIM), F32)
    st_spec = pl.BlockSpec((1, HEADS, HEAD_DIM, HEAD_DIM), lambda b: (b, 0, 0, 0))
    return pl.pallas_call(
        _ctx_kernel,
        grid=(bsz,),
        in_specs=[
            pl.BlockSpec((1, n, D_MODEL), lambda b: (b, 0, 0)),
            pl.BlockSpec((1, 3, D_MODEL), lambda b: (bsz, 0, 0)),
            pl.BlockSpec((1, D_MODEL), lambda b: (0, 0)),
            pl.BlockSpec(w_c.shape, lambda b: (0, 0)),
            pl.BlockSpec((2, HG_WIDTH), lambda b: (0, 0)),
        ],
        out_specs=[st_spec, st_spec],
        out_shape=[st_shape, st_shape],
        compiler_params=pltpu.CompilerParams(dimension_semantics=("arbitrary",),
                                             vmem_limit_bytes=VMEM_LIMIT),
        name="ctx_states",
    )(ctx, mod3, nw, w_c, lb)


def _pivot_row(i, level, reverse):
    half = NBLK >> (level + 1)
    boundary = ((i // (2 * half)) * 2 + 1) * half * BLK
    return boundary if reverse else boundary - 1


def _q_side(i, level, reverse):
    later = (i // (NBLK >> (level + 1))) % 2 == 1
    return later != reverse


def _scan_prep(c, z_ref, q_ref, lb, ops_s, bc_s, tri2, reverse):
    r0 = c * CHUNK
    g, kk = _gates(z_ref[r0:r0 + CHUNK, :], lb)
    bc_s[r0:r0 + CHUNK, :] = _cumsum_time(g, tri2)
    blk = lax.broadcasted_iota(jnp.int32, (NBLK, 1), 0)

    def per_block(row_of_block):
        out = None
        for i in reversed(range(NBLK)):
            if i + 1 < NBLK and row_of_block[i] == row_of_block[i + 1]:
                continue
            row = bc_s[r0 + row_of_block[i]:r0 + row_of_block[i] + 1, :]
            out = row if out is None else jnp.where(blk <= i, row, out)
        return jnp.broadcast_to(out, (NBLK, out.shape[1]))

    mid_off = BLK // 2 if reverse else BLK // 2 - 1
    mid = per_block([i * BLK + mid_off for i in range(NBLK)])
    tot_row = r0 if reverse else r0 + CHUNK - 1
    tot = bc_s[tot_row:tot_row + 1, :]
    scale = [jnp.exp(mid), jnp.exp(tot - mid)]
    for level in range(LEVELS):
        piv = per_block([_pivot_row(i, level, reverse) for i in range(NBLK)])
        scale.append(jnp.exp(-jnp.abs(mid - piv)))
    for i in range(NBLK):
        rows = slice(r0 + i * BLK, r0 + (i + 1) * BLK)
        d = bc_s[rows, :] - mid[i:i + 1, :]
        qd = q_ref[rows, :].astype(F32) * jnp.exp(jnp.minimum(d, EXP_CLAMP))
        kd = kk[i * BLK:(i + 1) * BLK, :] * jnp.exp(jnp.minimum(-d, EXP_CLAMP))
        ops_s[0, rows, :] = qd.astype(BF16)
        ops_s[1, rows, :] = kd.astype(BF16)
        ops_s[2, rows, :] = (qd * scale[0][i:i + 1, :]).astype(BF16)
        ops_s[3, rows, :] = (kd * scale[1][i:i + 1, :]).astype(BF16)
        for level in range(LEVELS):
            s = scale[2 + level][i:i + 1, :]
            if _q_side(i, level, reverse):
                ops_s[4 + 2 * level, rows, :] = (qd * s).astype(BF16)
            else:
                ops_s[5 + 2 * level, rows, :] = (kd * s).astype(BF16)
    return jnp.exp(tot)


def _pair_masks(reverse):
    t = lax.broadcasted_iota(jnp.int32, (CHUNK, CHUNK), 0)
    s = lax.broadcasted_iota(jnp.int32, (CHUNK, CHUNK), 1)
    causal = (s >= t) if reverse else (s <= t)
    same = [(t // (CHUNK >> level)) == (s // (CHUNK >> level)) for level in range(1, LEVELS + 1)]
    return causal, same


def _scan_mm(c, dec, v_ref, st_ref, ops_s, masks, emit):
    rows = slice(c * CHUNK, (c + 1) * CHUNK)
    causal, same = masks
    attn = []
    for hd in range(HEADS):
        sl = slice(hd * HEAD_DIM, (hd + 1) * HEAD_DIM)
        pair = lambda n: lax.dot_general(ops_s[n, rows, sl], ops_s[n + 1, rows, sl], _NT,
                                         preferred_element_type=F32)
        a = pair(4)
        for level in range(1, LEVELS):
            a = jnp.where(same[level - 1], pair(4 + 2 * level), a)
        a = jnp.where(same[LEVELS - 1], jnp.where(causal, pair(0), 0.0), a)
        attn.append(a.astype(BF16))
    for hd in range(HEADS):
        sl = slice(hd * HEAD_DIM, (hd + 1) * HEAD_DIM)
        st = st_ref[hd]
        v = v_ref[rows, sl]
        lhs = jnp.concatenate([ops_s[2, rows, sl], attn[hd]], axis=1)
        rhs = jnp.concatenate([st.astype(BF16), v], axis=0)
        emit(rows, sl, jnp.dot(lhs, rhs, preferred_element_type=F32))
        kv = lax.dot_general(ops_s[3, rows, sl], v, _TN, preferred_element_type=F32)
        dec_col = jnp.transpose(jnp.broadcast_to(dec[:, sl], (HEAD_DIM, HEAD_DIM)))
        st_ref[hd] = st * dec_col + kv


def _scan_tile(z_ref, q_ref, v_ref, lb, st_ref, ops_s, bc_s, emit, reverse):
    n_chunks = z_ref.shape[0] // CHUNK
    order = range(n_chunks - 1, -1, -1) if reverse else range(n_chunks)
    tri_b = _tri(CHUNK, reverse).astype(BF16)
    tri2 = jnp.concatenate([tri_b, tri_b], axis=1)
    masks = _pair_masks(reverse)
    dec = {c: _scan_prep(c, z_ref, q_ref, lb, ops_s, bc_s, tri2, reverse) for c in order}
    for c in order:
        _scan_mm(c, dec[c], v_ref, st_ref, ops_s, masks, emit)


def _init_scan_scratch(s0_ref, st_s, ops_s):
    @pl.when(pl.program_id(1) == 0)
    def _():
        st_s[...] = s0_ref[0]

    @pl.when((pl.program_id(0) == 0) & (pl.program_id(1) == 0))
    def _():
        ops_s[4:] = jnp.zeros((2 * LEVELS,) + ops_s.shape[1:], BF16)


_SCAN_SCRATCH = [
    pltpu.VMEM((HEADS, HEAD_DIM, HEAD_DIM), F32),
    pltpu.VMEM((N_OPS, TILE_T, HG_WIDTH), BF16),
    pltpu.VMEM((TILE_T, HG_WIDTH), F32),
    pltpu.VMEM((TILE_T, HG_WIDTH), F32),
]


def _bwd_kernel(x_ref, mod_ref, nw_ref, w_ref, lb_ref, s0_ref,
                q_ref, v_ref, ob_ref, st_s, ops_s, bc_s, z_s):
    _init_scan_scratch(s0_ref, st_s, ops_s)
    h = _modulated_norm(x_ref[0], nw_ref[...], mod_ref)
    q_ref[0] = jnp.dot(h, w_ref[:, 0:HG_WIDTH], preferred_element_type=F32).astype(BF16)
    v_ref[0] = jnp.dot(h, w_ref[:, HG_WIDTH:2 * HG_WIDTH], preferred_element_type=F32).astype(BF16)
    z_s[...] = jnp.dot(h, w_ref[:, 2 * HG_WIDTH:3 * HG_WIDTH], preferred_element_type=F32)

    def emit(rows, sl, o):
        ob_ref[0, rows, sl] = o.astype(BF16)

    _scan_tile(z_s, q_ref.at[0], v_ref.at[0], lb_ref[1:2, :], st_s, ops_s, bc_s, emit, reverse=True)


def _bwd_call(x, mod3, nw, w_b, lb, sb):
    bsz, seq, _ = x.shape
    nt = seq // TILE_T
    tile = lambda b, t: (b, nt - 1 - t, 0)
    act = jax.ShapeDtypeStruct((bsz, seq, HG_WIDTH), BF16)
    act_spec = pl.BlockSpec((1, TILE_T, HG_WIDTH), tile)
    return pl.pallas_call(
        _bwd_kernel,
        grid=(bsz, nt),
        in_specs=[
            pl.BlockSpec((1, TILE_T, D_MODEL), tile),
            pl.BlockSpec((1, 3, D_MODEL), lambda b, t: (b, 0, 0)),
            pl.BlockSpec((1, D_MODEL), lambda b, t: (0, 0)),
            pl.BlockSpec(w_b.shape, lambda b, t: (0, 0)),
            pl.BlockSpec((2, HG_WIDTH), lambda b, t: (0, 0)),
            pl.BlockSpec((1, HEADS, HEAD_DIM, HEAD_DIM), lambda b, t: (b, 0, 0, 0)),
        ],
        out_specs=[act_spec, act_spec, act_spec],
        out_shape=[act, act, act],
        scratch_shapes=_SCAN_SCRATCH,
        compiler_params=pltpu.CompilerParams(dimension_semantics=("arbitrary", "arbitrary"),
                                             vmem_limit_bytes=VMEM_LIMIT),
        name="bwd_scan",
    )(x, mod3, nw, w_b, lb, sb)


def _fwd_kernel(x_ref, mod_ref, nw_ref, w_ref, lb_ref, s0_ref, q_ref, v_ref, ob_ref,
                onw_ref, cw_ref, wo_ref, fnw_ref, out_ref, st_s, ops_s, bc_s, z_s, o_s, mix_s):
    _init_scan_scratch(s0_ref, st_s, ops_s)
    x = x_ref[0]
    h = _modulated_norm(x, nw_ref[...], mod_ref)

    z_s[...] = jnp.dot(h, w_ref[:, 0:HG_WIDTH], preferred_element_type=F32)

    def emit(rows, sl, o):
        o_s[rows, sl] = o + ob_ref[0, rows, sl].astype(F32)

    _scan_tile(z_s, q_ref.at[0], v_ref.at[0], lb_ref[0:1, :], st_s, ops_s, bc_s, emit, reverse=False)

    gate = jnp.dot(h, w_ref[:, HG_WIDTH:2 * HG_WIDTH], preferred_element_type=F32)
    onw = onw_ref[...]
    for hd in range(HEADS):
        sl = slice(hd * HEAD_DIM, (hd + 1) * HEAD_DIM)
        o = o_s[:, sl]
        g = gate[:, sl]
        y = (o * _rms_rows(o) * onw[:, sl]) * (g * _sigmoid(g))
        mix_s[:, sl] = y.astype(BF16)

    c0 = 2 * HG_WIDTH
    cv = jnp.dot(h, w_ref[:, c0:c0 + 4 * CONV_WIDTH], preferred_element_type=F32)
    n_tok = cv.shape[0]
    u = cv[:, 2 * CONV_WIDTH:3 * CONV_WIDTH] * cv[:, 0:CONV_WIDTH]
    col = lax.broadcasted_iota(jnp.int32, (n_tok, 1), 0) % GRID_W
    u_prev = jnp.where(col == 0, 0.0, pltpu.roll(u, 1, axis=0))
    u_next = jnp.where(col == GRID_W - 1, 0.0, pltpu.roll(u, n_tok - 1, axis=0))
    conv = cw_ref[0:1, :] * u_prev + cw_ref[1:2, :] * u + cw_ref[2:3, :] * u_next
    cg = cv[:, 3 * CONV_WIDTH:4 * CONV_WIDTH]
    y_cv = cv[:, CONV_WIDTH:2 * CONV_WIDTH] * conv * (cg * _sigmoid(cg))
    mix_s[:, HG_WIDTH:] = y_cv.astype(BF16)

    y = jnp.dot(mix_s[...], wo_ref[...], preferred_element_type=F32)
    r = x + mod_ref[0, 2:3, :] * y
    out_ref[0] = (r * _rms_rows(r)) * fnw_ref[...]


def _fwd_call(x, mod3, nw, w_f, lb, sf, q, v, ob, onw, conv_w, w_o, fnw):
    bsz, seq, _ = x.shape
    nt = seq // TILE_T
    tile = lambda b, t: (b, t, 0)
    const2 = lambda b, t: (0, 0)
    act_spec = pl.BlockSpec((1, TILE_T, HG_WIDTH), tile)
    return pl.pallas_call(
        _fwd_kernel,
        grid=(bsz, nt),
        in_specs=[
            pl.BlockSpec((1, TILE_T, D_MODEL), tile),
            pl.BlockSpec((1, 3, D_MODEL), lambda b, t: (b, 0, 0)),
            pl.BlockSpec((1, D_MODEL), const2),
            pl.BlockSpec(w_f.shape, const2),
            pl.BlockSpec((2, HG_WIDTH), const2),
            pl.BlockSpec((1, HEADS, HEAD_DIM, HEAD_DIM), lambda b, t: (b, 0, 0, 0)),
            act_spec, act_spec, act_spec,
            pl.BlockSpec((1, HG_WIDTH), const2),
            pl.BlockSpec(conv_w.shape, const2),
            pl.BlockSpec(w_o.shape, const2),
            pl.BlockSpec((1, D_MODEL), const2),
        ],
        out_specs=pl.BlockSpec((1, TILE_T, D_MODEL), tile),
        out_shape=jax.ShapeDtypeStruct((bsz, seq, D_MODEL), F32),
        scratch_shapes=_SCAN_SCRATCH + [
            pltpu.VMEM((TILE_T, HG_WIDTH), F32),
            pltpu.VMEM((TILE_T, HG_WIDTH + CONV_WIDTH), BF16),
        ],
        compiler_params=pltpu.CompilerParams(dimension_semantics=("arbitrary", "arbitrary"),
                                             vmem_limit_bytes=VMEM_LIMIT),
        name="fwd_scan_out",
    )(x, mod3, nw, w_f, lb, sf, q, v, ob, onw, conv_w, w_o, fnw)


def kernel(x, c, ctx, c_ctx, norm_w, w_ada, b_ada, w_in, hg_lb_logits, hg_onorm_w, conv_w,
           w_out, final_norm_w):
    assert w_in.shape[0] == 1, "single-layer block"
    bsz = x.shape[0]
    W = HG_WIDTH
    cc = jnp.concatenate([c, c_ctx[None, :], jnp.zeros((8 - bsz - 1, D_MODEL), F32)], axis=0)
    mod, lb = _mod_call(cc, w_ada[0], b_ada[0][None, :], hg_lb_logits.reshape(-1, W))
    mod3 = mod.reshape(8, 3, D_MODEL)

    wi = w_in[0].astype(BF16)
    w_c = jnp.concatenate([wi[:, W:2 * W], wi[:, 2 * W:3 * W], wi[:, 3 * W:4 * W]], axis=1)
    w_b = jnp.concatenate([wi[:, 0:2 * W], wi[:, 3 * W:4 * W]], axis=1)
    w_f = jnp.concatenate([wi[:, 2 * W:3 * W], wi[:, 4 * W:]], axis=1)
    nw = norm_w[0][None, :]

    sf, sb = _ctx_call(ctx, mod3, nw, w_c, lb)
    q, v, ob = _bwd_call(x, mod3, nw, w_b, lb, sb)
    onw = jnp.tile(hg_onorm_w[0], HEADS)[None, :]
    return _fwd_call(x, mod3, nw, w_f, lb, sf, q, v, ob, onw, conv_w[0],
                     w_out[0].astype(BF16), final_norm_w[None, :])
```

```python
import jax
import jax.numpy as jnp
from jax import lax
from jax.experimental import pallas as pl
from jax.experimental.pallas import tpu as pltpu

D_MODEL = 1024
HEADS = 8
HEAD_DIM = 128
HG_WIDTH = HEADS * HEAD_DIM
CONV_WIDTH = 1024
GRID_W = 64
EPS = 1e-6

CHUNK = 128
BLK = 16
NBLK = CHUNK // BLK
LEVELS = NBLK.bit_length() - 1
N_OPS = 4 + 2 * LEVELS
TILE_T = 256
EXP_CLAMP = 70.0
VMEM_LIMIT = 56 * 1024 * 1024

F32 = jnp.float32
BF16 = jnp.bfloat16

_NT = (((1,), (1,)), ((), ()))
_TN = (((0,), (0,)), ((), ()))


def _sigmoid(x):
    return 1.0 / (1.0 + jnp.exp(-x))


def _rms_rows(xf):
    return lax.rsqrt(jnp.mean(xf * xf, axis=-1, keepdims=True) + EPS)


def _tri(n, reverse):
    t = lax.broadcasted_iota(jnp.int32, (n, n), 0)
    r = lax.broadcasted_iota(jnp.int32, (n, n), 1)
    return (r >= t) if reverse else (r <= t)


def _cumsum_time(g, tri2):
    g_hi = g.astype(BF16)
    g_lo = (g - g_hi.astype(F32)).astype(BF16)
    return jnp.dot(tri2, jnp.concatenate([g_hi, g_lo], axis=0), preferred_element_type=F32)


def _gates(z, lb):
    f = lb + (1.0 - lb) * _sigmoid(z)
    return jnp.log(f), 1.0 - f


def _mod_kernel(cc_ref, w_ref, b_ref, lg_ref, mod_ref, lb_ref):
    cc = cc_ref[...]
    s = cc * _sigmoid(cc)
    mod_ref[...] = jnp.dot(s, w_ref[...], precision=lax.Precision.HIGHEST,
                           preferred_element_type=F32) + b_ref[...]
    n_rows = lg_ref.shape[0] // 2
    for d in range(2):
        lg = lg_ref[d * n_rows:(d + 1) * n_rows, :]
        e = jnp.exp(lg - jnp.max(lg, axis=0, keepdims=True))
        lb_ref[d:d + 1, :] = e[0:1, :] / jnp.sum(e, axis=0, keepdims=True)


def _mod_call(cc, w_ada, b_ada, lb_logits):
    n_out = w_ada.shape[1]
    bn = 768
    return pl.pallas_call(
        _mod_kernel,
        grid=(n_out // bn,),
        in_specs=[
            pl.BlockSpec((8, D_MODEL), lambda j: (0, 0)),
            pl.BlockSpec((D_MODEL, bn), lambda j: (0, j)),
            pl.BlockSpec((1, bn), lambda j: (0, j)),
            pl.BlockSpec(lb_logits.shape, lambda j: (0, 0)),
        ],
        out_specs=[
            pl.BlockSpec((8, bn), lambda j: (0, j)),
            pl.BlockSpec((2, HG_WIDTH), lambda j: (0, 0)),
        ],
        out_shape=[
            jax.ShapeDtypeStruct((8, n_out), F32),
            jax.ShapeDtypeStruct((2, HG_WIDTH), F32),
        ],
        compiler_params=pltpu.CompilerParams(dimension_semantics=("arbitrary",)),
        name="adaln_mod",
    )(cc, w_ada, b_ada, lb_logits)


def _modulated_norm(x, nw, mod_ref):
    shift = mod_ref[0, 0:1, :]
    scale = mod_ref[0, 1:2, :]
    y = (x * _rms_rows(x)) * nw
    return (y * (1.0 + scale) + shift).astype(BF16)


def _ctx_kernel(x_ref, mod_ref, nw_ref, w_ref, lb_ref, sf_ref, sb_ref):
    n = x_ref.shape[1]
    h = _modulated_norm(x_ref[0], nw_ref[...], mod_ref)
    v = jnp.dot(h, w_ref[:, 0:HG_WIDTH], preferred_element_type=F32).astype(BF16)
    for d, out_ref in ((0, sf_ref), (1, sb_ref)):
        z = jnp.dot(h, w_ref[:, (1 + d) * HG_WIDTH:(2 + d) * HG_WIDTH], preferred_element_type=F32)
        g, kk = _gates(z, lb_ref[d:d + 1, :])
        tri = _tri(n, reverse=bool(d)).astype(BF16)
        bc = _cumsum_time(g, jnp.concatenate([tri, tri], axis=1))
        tot = bc[0:1, :] if d else bc[n - 1:n, :]
        ks = (kk * jnp.exp(tot - bc)).astype(BF16)
        for hd in range(HEADS):
            sl = slice(hd * HEAD_DIM, (hd + 1) * HEAD_DIM)
            out_ref[0, hd] = lax.dot_general(ks[:, sl], v[:, sl], _TN, preferred_element_type=F32)


def _ctx_call(ctx, mod3, nw, w_c, lb):
    bsz, n, _ = ctx.shape
    st_shape = jax.ShapeDtypeStruct((bsz, HEADS, HEAD_DIM, HEAD_DIM), F32)
    st_spec = pl.BlockSpec((1, HEADS, HEAD_DIM, HEAD_DIM), lambda b: (b, 0, 0, 0))
    return pl.pallas_call(
        _ctx_kernel,
        grid=(bsz,),
        in_specs=[
            pl.BlockSpec((1, n, D_MODEL), lambda b: (b, 0, 0)),
            pl.BlockSpec((1, 3, D_MODEL), lambda b: (bsz, 0, 0)),
            pl.BlockSpec((1, D_MODEL), lambda b: (0, 0)),
            pl.BlockSpec(w_c.shape, lambda b: (0, 0)),
            pl.BlockSpec((2, HG_WIDTH), lambda b: (0, 0)),
        ],
        out_specs=[st_spec, st_spec],
        out_shape=[st_shape, st_shape],
        compiler_params=pltpu.CompilerParams(dimension_semantics=("arbitrary",),
                                             vmem_limit_bytes=VMEM_LIMIT),
        name="ctx_states",
    )(ctx, mod3, nw, w_c, lb)


def _pivot_row(i, level, reverse):
    half = NBLK >> (level + 1)
    boundary = ((i // (2 * half)) * 2 + 1) * half * BLK
    return boundary if reverse else boundary - 1


def _q_side(i, level, reverse):
    later = (i // (NBLK >> (level + 1))) % 2 == 1
    return later != reverse


def _scan_gates(c, z_ref, lb, bc_s, tri2):
    rows = slice(c * CHUNK, (c + 1) * CHUNK)
    g, kk = _gates(z_ref[rows, :], lb)
    bc_s[rows, :] = _cumsum_time(g, tri2)
    z_ref[rows, :] = kk


def _scan_prep(c, k_ref, q_ref, ops_s, bc_s, reverse):
    r0 = c * CHUNK
    blk = lax.broadcasted_iota(jnp.int32, (NBLK, 1), 0)

    def per_block(row_of_block):
        out = None
        for i in reversed(range(NBLK)):
            if i + 1 < NBLK and row_of_block[i] == row_of_block[i + 1]:
                continue
            row = bc_s[r0 + row_of_block[i]:r0 + row_of_block[i] + 1, :]
            out = row if out is None else jnp.where(blk <= i, row, out)
        return jnp.broadcast_to(out, (NBLK, out.shape[1]))

    mid_off = BLK // 2 if reverse else BLK // 2 - 1
    mid = per_block([i * BLK + mid_off for i in range(NBLK)])
    tot_row = r0 if reverse else r0 + CHUNK - 1
    tot = bc_s[tot_row:tot_row + 1, :]
    scale = [jnp.exp(mid), jnp.exp(tot - mid)]
    for level in range(LEVELS):
        piv = per_block([_pivot_row(i, level, reverse) for i in range(NBLK)])
        scale.append(jnp.exp(-jnp.abs(mid - piv)))
    for i in range(NBLK):
        rows = slice(r0 + i * BLK, r0 + (i + 1) * BLK)
        d = bc_s[rows, :] - mid[i:i + 1, :]
        qd = q_ref[rows, :].astype(F32) * jnp.exp(jnp.minimum(d, EXP_CLAMP))
        kd = k_ref[rows, :] * jnp.exp(jnp.minimum(-d, EXP_CLAMP))
        ops_s[0, rows, :] = qd.astype(BF16)
        ops_s[1, rows, :] = kd.astype(BF16)
        ops_s[2, rows, :] = (qd * scale[0][i:i + 1, :]).astype(BF16)
        ops_s[3, rows, :] = (kd * scale[1][i:i + 1, :]).astype(BF16)
        for level in range(LEVELS):
            s = scale[2 + level][i:i + 1, :]
            if _q_side(i, level, reverse):
                ops_s[4 + 2 * level, rows, :] = (qd * s).astype(BF16)
            else:
                ops_s[5 + 2 * level, rows, :] = (kd * s).astype(BF16)
    return jnp.exp(tot)


def _pair_masks(reverse):
    t = lax.broadcasted_iota(jnp.int32, (CHUNK, CHUNK), 0)
    s = lax.broadcasted_iota(jnp.int32, (CHUNK, CHUNK), 1)
    causal = (s >= t) if reverse else (s <= t)
    same = [(t // (CHUNK >> level)) == (s // (CHUNK >> level)) for level in range(1, LEVELS + 1)]
    return causal, same


def _scan_mm(c, dec, v_ref, st_ref, ops_s, masks, emit):
    rows = slice(c * CHUNK, (c + 1) * CHUNK)
    causal, same = masks
    attn = []
    for hd in range(HEADS):
        sl = slice(hd * HEAD_DIM, (hd + 1) * HEAD_DIM)
        pair = lambda n: lax.dot_general(ops_s[n, rows, sl], ops_s[n + 1, rows, sl], _NT,
                                         preferred_element_type=F32)
        a = pair(4)
        for level in range(1, LEVELS):
            a = jnp.where(same[level - 1], pair(4 + 2 * level), a)
        a = jnp.where(same[LEVELS - 1], jnp.where(causal, pair(0), 0.0), a)
        attn.append(a.astype(BF16))
    for hd in range(HEADS):
        sl = slice(hd * HEAD_DIM, (hd + 1) * HEAD_DIM)
        st = st_ref[hd]
        v = v_ref[rows, sl]
        lhs = jnp.concatenate([ops_s[2, rows, sl], attn[hd]], axis=1)
        rhs = jnp.concatenate([st.astype(BF16), v], axis=0)
        emit(rows, sl, jnp.dot(lhs, rhs, preferred_element_type=F32))
        kv = lax.dot_general(ops_s[3, rows, sl], v, _TN, preferred_element_type=F32)
        dec_col = jnp.transpose(jnp.broadcast_to(dec[:, sl], (HEAD_DIM, HEAD_DIM)))
        st_ref[hd] = st * dec_col + kv


def _scan_tile(z_ref, q_ref, v_ref, lb, st_ref, ops_s, bc_s, emit, reverse,
               after_gates=None, before_matmuls=None):
    n_chunks = z_ref.shape[0] // CHUNK
    order = range(n_chunks - 1, -1, -1) if reverse else range(n_chunks)
    tri_b = _tri(CHUNK, reverse).astype(BF16)
    tri2 = jnp.concatenate([tri_b, tri_b], axis=1)
    masks = _pair_masks(reverse)
    for c in order:
        _scan_gates(c, z_ref, lb, bc_s, tri2)
    if after_gates is not None:
        after_gates()
    dec = {c: _scan_prep(c, z_ref, q_ref, ops_s, bc_s, reverse) for c in order}
    if before_matmuls is not None:
        before_matmuls()
    for c in order:
        _scan_mm(c, dec[c], v_ref, st_ref, ops_s, masks, emit)


def _init_scan_scratch(s0_ref, st_s, ops_s):
    @pl.when(pl.program_id(1) == 0)
    def _():
        st_s[...] = s0_ref[0]

    @pl.when((pl.program_id(0) == 0) & (pl.program_id(1) == 0))
    def _():
        ops_s[4:] = jnp.zeros((2 * LEVELS,) + ops_s.shape[1:], BF16)


_SCAN_SCRATCH = [
    pltpu.VMEM((HEADS, HEAD_DIM, HEAD_DIM), F32),
    pltpu.VMEM((N_OPS, TILE_T, HG_WIDTH), BF16),
    pltpu.VMEM((TILE_T, HG_WIDTH), F32),
    pltpu.VMEM((TILE_T, HG_WIDTH), F32),
]


def _bwd_kernel(x_ref, mod_ref, nw_ref, w_ref, lb_ref, s0_ref,
                q_ref, v_ref, ob_ref, st_s, ops_s, bc_s, z_s):
    _init_scan_scratch(s0_ref, st_s, ops_s)
    h = _modulated_norm(x_ref[0], nw_ref[...], mod_ref)
    z_s[...] = jnp.dot(h, w_ref[:, 2 * HG_WIDTH:3 * HG_WIDTH], preferred_element_type=F32)
    q_ref[0] = jnp.dot(h, w_ref[:, 0:HG_WIDTH], preferred_element_type=F32).astype(BF16)

    def value_proj():
        v_ref[0] = jnp.dot(h, w_ref[:, HG_WIDTH:2 * HG_WIDTH],
                           preferred_element_type=F32).astype(BF16)

    def emit(rows, sl, o):
        ob_ref[0, rows, sl] = o.astype(BF16)

    _scan_tile(z_s, q_ref.at[0], v_ref.at[0], lb_ref[1:2, :], st_s, ops_s, bc_s, emit, reverse=True,
               after_gates=value_proj)


def _bwd_call(x, mod3, nw, w_b, lb, sb):
    bsz, seq, _ = x.shape
    nt = seq // TILE_T
    tile = lambda b, t: (b, nt - 1 - t, 0)
    act = jax.ShapeDtypeStruct((bsz, seq, HG_WIDTH), BF16)
    act_spec = pl.BlockSpec((1, TILE_T, HG_WIDTH), tile)
    return pl.pallas_call(
        _bwd_kernel,
        grid=(bsz, nt),
        in_specs=[
            pl.BlockSpec((1, TILE_T, D_MODEL), tile),
            pl.BlockSpec((1, 3, D_MODEL), lambda b, t: (b, 0, 0)),
            pl.BlockSpec((1, D_MODEL), lambda b, t: (0, 0)),
            pl.BlockSpec(w_b.shape, lambda b, t: (0, 0)),
            pl.BlockSpec((2, HG_WIDTH), lambda b, t: (0, 0)),
            pl.BlockSpec((1, HEADS, HEAD_DIM, HEAD_DIM), lambda b, t: (b, 0, 0, 0)),
        ],
        out_specs=[act_spec, act_spec, act_spec],
        out_shape=[act, act, act],
        scratch_shapes=_SCAN_SCRATCH,
        compiler_params=pltpu.CompilerParams(dimension_semantics=("arbitrary", "arbitrary"),
                                             vmem_limit_bytes=VMEM_LIMIT),
        name="bwd_scan",
    )(x, mod3, nw, w_b, lb, sb)


def _fwd_kernel(x_ref, mod_ref, nw_ref, w_ref, lb_ref, s0_ref, q_ref, v_ref, ob_ref,
                onw_ref, cw_ref, wo_ref, fnw_ref, out_ref, st_s, ops_s, bc_s, z_s, o_s, mix_s, g_s, cv_s):
    _init_scan_scratch(s0_ref, st_s, ops_s)
    x = x_ref[0]
    h = _modulated_norm(x, nw_ref[...], mod_ref)

    z_s[...] = jnp.dot(h, w_ref[:, 0:HG_WIDTH], preferred_element_type=F32)

    def conv_proj():
        c0 = 2 * HG_WIDTH
        cv_s[...] = jnp.dot(h, w_ref[:, c0:c0 + 4 * CONV_WIDTH], preferred_element_type=F32)

    def conv_branch():
        g_s[...] = jnp.dot(h, w_ref[:, HG_WIDTH:2 * HG_WIDTH], preferred_element_type=F32)
        n_tok = cv_s.shape[0]
        u = cv_s[:, 2 * CONV_WIDTH:3 * CONV_WIDTH] * cv_s[:, 0:CONV_WIDTH]
        col = lax.broadcasted_iota(jnp.int32, (n_tok, 1), 0) % GRID_W
        u_prev = jnp.where(col == 0, 0.0, pltpu.roll(u, 1, axis=0))
        u_next = jnp.where(col == GRID_W - 1, 0.0, pltpu.roll(u, n_tok - 1, axis=0))
        conv = cw_ref[0:1, :] * u_prev + cw_ref[1:2, :] * u + cw_ref[2:3, :] * u_next
        cg = cv_s[:, 3 * CONV_WIDTH:4 * CONV_WIDTH]
        y_cv = cv_s[:, CONV_WIDTH:2 * CONV_WIDTH] * conv * (cg * _sigmoid(cg))
        mix_s[:, HG_WIDTH:] = y_cv.astype(BF16)

    def emit(rows, sl, o):
        o_s[rows, sl] = o + ob_ref[0, rows, sl].astype(F32)

    _scan_tile(z_s, q_ref.at[0], v_ref.at[0], lb_ref[0:1, :], st_s, ops_s, bc_s, emit, reverse=False,
               after_gates=conv_proj, before_matmuls=conv_branch)

    onw = onw_ref[...]
    for hd in range(HEADS):
        sl = slice(hd * HEAD_DIM, (hd + 1) * HEAD_DIM)
        o = o_s[:, sl]
        g = g_s[:, sl]
        y = (o * _rms_rows(o) * onw[:, sl]) * (g * _sigmoid(g))
        mix_s[:, sl] = y.astype(BF16)

    y = jnp.dot(mix_s[...], wo_ref[...], preferred_element_type=F32)
    r = x + mod_ref[0, 2:3, :] * y
    out_ref[0] = (r * _rms_rows(r)) * fnw_ref[...]


def _fwd_call(x, mod3, nw, w_f, lb, sf, q, v, ob, onw, conv_w, w_o, fnw):
    bsz, seq, _ = x.shape
    nt = seq // TILE_T
    tile = lambda b, t: (b, t, 0)
    const2 = lambda b, t: (0, 0)
    act_spec = pl.BlockSpec((1, TILE_T, HG_WIDTH), tile)
    return pl.pallas_call(
        _fwd_kernel,
        grid=(bsz, nt),
        in_specs=[
            pl.BlockSpec((1, TILE_T, D_MODEL), tile),
            pl.BlockSpec((1, 3, D_MODEL), lambda b, t: (b, 0, 0)),
            pl.BlockSpec((1, D_MODEL), const2),
            pl.BlockSpec(w_f.shape, const2),
            pl.BlockSpec((2, HG_WIDTH), const2),
            pl.BlockSpec((1, HEADS, HEAD_DIM, HEAD_DIM), lambda b, t: (b, 0, 0, 0)),
            act_spec, act_spec, act_spec,
            pl.BlockSpec((1, HG_WIDTH), const2),
            pl.BlockSpec(conv_w.shape, const2),
            pl.BlockSpec(w_o.shape, const2),
            pl.BlockSpec((1, D_MODEL), const2),
        ],
        out_specs=pl.BlockSpec((1, TILE_T, D_MODEL), tile),
        out_shape=jax.ShapeDtypeStruct((bsz, seq, D_MODEL), F32),
        scratch_shapes=_SCAN_SCRATCH + [
            pltpu.VMEM((TILE_T, HG_WIDTH), F32),
            pltpu.VMEM((TILE_T, HG_WIDTH + CONV_WIDTH), BF16),
            pltpu.VMEM((TILE_T, HG_WIDTH), F32),
            pltpu.VMEM((TILE_T, 4 * CONV_WIDTH), F32),
        ],
        compiler_params=pltpu.CompilerParams(dimension_semantics=("arbitrary", "arbitrary"),
                                             vmem_limit_bytes=VMEM_LIMIT),
        name="fwd_scan_out",
    )(x, mod3, nw, w_f, lb, sf, q, v, ob, onw, conv_w, w_o, fnw)


def kernel(x, c, ctx, c_ctx, norm_w, w_ada, b_ada, w_in, hg_lb_logits, hg_onorm_w, conv_w,
           w_out, final_norm_w):
    assert w_in.shape[0] == 1, "single-layer block"
    bsz = x.shape[0]
    W = HG_WIDTH
    cc = jnp.concatenate([c, c_ctx[None, :], jnp.zeros((8 - bsz - 1, D_MODEL), F32)], axis=0)
    mod, lb = _mod_call(cc, w_ada[0], b_ada[0][None, :], hg_lb_logits.reshape(-1, W))
    mod3 = mod.reshape(8, 3, D_MODEL)

    wi = w_in[0].astype(BF16)
    w_c = jnp.concatenate([wi[:, W:2 * W], wi[:, 2 * W:3 * W], wi[:, 3 * W:4 * W]], axis=1)
    w_b = jnp.concatenate([wi[:, 0:2 * W], wi[:, 3 * W:4 * W]], axis=1)
    w_f = jnp.concatenate([wi[:, 2 * W:3 * W], wi[:, 4 * W:]], axis=1)
    nw = norm_w[0][None, :]

    sf, sb = _ctx_call(ctx, mod3, nw, w_c, lb)
    q, v, ob = _bwd_call(x, mod3, nw, w_b, lb, sb)
    onw = jnp.tile(hg_onorm_w[0], HEADS)[None, :]
    return _fwd_call(x, mod3, nw, w_f, lb, sf, q, v, ob, onw, conv_w[0],
                     w_out[0].astype(BF16), final_norm_w[None, :])
```

```python
import jax
import jax.numpy as jnp
from jax import lax
from jax.experimental import pallas as pl
from jax.experimental.pallas import tpu as pltpu

D_MODEL = 1024
HEADS = 8
HEAD_DIM = 128
HG_WIDTH = HEADS * HEAD_DIM
CONV_WIDTH = 1024
GRID_W = 64
EPS = 1e-6

CHUNK = 128
BLK = 16
NBLK = CHUNK // BLK
LEVELS = NBLK.bit_length() - 1
N_OPS = 4 + 2 * LEVELS
TILE_T = 256
EXP_CLAMP = 70.0
VMEM_LIMIT = 56 * 1024 * 1024

COL_Q, COL_V, COL_ZF, COL_ZB, COL_GATE, COL_CONV = 0, 1, 2, 3, 4, 5

F32 = jnp.float32
BF16 = jnp.bfloat16

_NT = (((1,), (1,)), ((), ()))
_TN = (((0,), (0,)), ((), ()))


def _w_in_cols(group):
    return pl.BlockSpec((D_MODEL, HG_WIDTH), lambda *_: (0, group))


def _sigmoid(x):
    return 1.0 / (1.0 + jnp.exp(-x))


def _rms_rows(xf):
    return lax.rsqrt(jnp.mean(xf * xf, axis=-1, keepdims=True) + EPS)


def _tri(n, reverse):
    t = lax.broadcasted_iota(jnp.int32, (n, n), 0)
    r = lax.broadcasted_iota(jnp.int32, (n, n), 1)
    return (r >= t) if reverse else (r <= t)


def _cumsum_time(g, tri2):
    g_hi = g.astype(BF16)
    g_lo = (g - g_hi.astype(F32)).astype(BF16)
    return jnp.dot(tri2, jnp.concatenate([g_hi, g_lo], axis=0), preferred_element_type=F32)


def _gates(z, lb):
    f = lb + (1.0 - lb) * _sigmoid(z)
    return jnp.log(f), 1.0 - f


def _mod_kernel(cc_ref, w_ref, b_ref, lg_ref, mod_ref, lb_ref):
    cc = cc_ref[...]
    s = cc * _sigmoid(cc)
    mod_ref[...] = jnp.dot(s, w_ref[...], precision=lax.Precision.HIGHEST,
                           preferred_element_type=F32) + b_ref[...]
    n_rows = lg_ref.shape[0] // 2
    for d in range(2):
        lg = lg_ref[d * n_rows:(d + 1) * n_rows, :]
        e = jnp.exp(lg - jnp.max(lg, axis=0, keepdims=True))
        lb_ref[d:d + 1, :] = e[0:1, :] / jnp.sum(e, axis=0, keepdims=True)


def _mod_call(cc, w_ada, b_ada, lb_logits):
    n_out = w_ada.shape[1]
    bn = 768
    return pl.pallas_call(
        _mod_kernel,
        grid=(n_out // bn,),
        in_specs=[
            pl.BlockSpec((8, D_MODEL), lambda j: (0, 0)),
            pl.BlockSpec((D_MODEL, bn), lambda j: (0, j)),
            pl.BlockSpec((1, bn), lambda j: (0, j)),
            pl.BlockSpec(lb_logits.shape, lambda j: (0, 0)),
        ],
        out_specs=[
            pl.BlockSpec((8, bn), lambda j: (0, j)),
            pl.BlockSpec((2, HG_WIDTH), lambda j: (0, 0)),
        ],
        out_shape=[
            jax.ShapeDtypeStruct((8, n_out), F32),
            jax.ShapeDtypeStruct((2, HG_WIDTH), F32),
        ],
        compiler_params=pltpu.CompilerParams(dimension_semantics=("arbitrary",)),
        name="adaln_mod",
    )(cc, w_ada, b_ada, lb_logits)


def _modulated_norm(x, nw, mod_ref):
    shift = mod_ref[0, 0:1, :]
    scale = mod_ref[0, 1:2, :]
    y = (x * _rms_rows(x)) * nw
    return (y * (1.0 + scale) + shift).astype(BF16)


def _ctx_kernel(x_ref, mod_ref, nw_ref, wv_ref, wzf_ref, wzb_ref, lb_ref, sf_ref, sb_ref):
    n = x_ref.shape[1]
    h = _modulated_norm(x_ref[0], nw_ref[...], mod_ref)
    v = jnp.dot(h, wv_ref[...], preferred_element_type=F32).astype(BF16)
    for d, wz_ref, out_ref in ((0, wzf_ref, sf_ref), (1, wzb_ref, sb_ref)):
        z = jnp.dot(h, wz_ref[...], preferred_element_type=F32)
        g, kk = _gates(z, lb_ref[d:d + 1, :])
        tri = _tri(n, reverse=bool(d)).astype(BF16)
        bc = _cumsum_time(g, jnp.concatenate([tri, tri], axis=1))
        tot = bc[0:1, :] if d else bc[n - 1:n, :]
        ks = (kk * jnp.exp(tot - bc)).astype(BF16)
        for hd in range(HEADS):
            sl = slice(hd * HEAD_DIM, (hd + 1) * HEAD_DIM)
            out_ref[0, hd] = lax.dot_general(ks[:, sl], v[:, sl], _TN, preferred_element_type=F32)


def _ctx_call(ctx, mod3, nw, wi, lb):
    bsz, n, _ = ctx.shape
    st_shape = jax.ShapeDtypeStruct((bsz, HEADS, HEAD_DIM, HEAD_DIM), F32)
    st_spec = pl.BlockSpec((1, HEADS, HEAD_DIM, HEAD_DIM), lambda b: (b, 0, 0, 0))
    return pl.pallas_call(
        _ctx_kernel,
        grid=(bsz,),
        in_specs=[
            pl.BlockSpec((1, n, D_MODEL), lambda b: (b, 0, 0)),
            pl.BlockSpec((1, 3, D_MODEL), lambda b: (bsz, 0, 0)),
            pl.BlockSpec((1, D_MODEL), lambda b: (0, 0)),
            _w_in_cols(COL_V), _w_in_cols(COL_ZF), _w_in_cols(COL_ZB),
            pl.BlockSpec((2, HG_WIDTH), lambda b: (0, 0)),
        ],
        out_specs=[st_spec, st_spec],
        out_shape=[st_shape, st_shape],
        compiler_params=pltpu.CompilerParams(dimension_semantics=("arbitrary",),
                                             vmem_limit_bytes=VMEM_LIMIT),
        name="ctx_states",
    )(ctx, mod3, nw, wi, wi, wi, lb)


def _pivot_row(i, level, reverse):
    half = NBLK >> (level + 1)
    boundary = ((i // (2 * half)) * 2 + 1) * half * BLK
    return boundary if reverse else boundary - 1


def _q_side(i, level, reverse):
    later = (i // (NBLK >> (level + 1))) % 2 == 1
    return later != reverse


def _scan_gates(c, z_ref, lb, bc_s, tri2):
    rows = slice(c * CHUNK, (c + 1) * CHUNK)
    g, kk = _gates(z_ref[rows, :], lb)
    bc_s[rows, :] = _cumsum_time(g, tri2)
    z_ref[rows, :] = kk


def _scan_prep(c, k_ref, q_ref, ops_s, bc_s, reverse):
    r0 = c * CHUNK
    blk = lax.broadcasted_iota(jnp.int32, (NBLK, 1), 0)

    def per_block(row_of_block):
        out = None
        for i in reversed(range(NBLK)):
            if i + 1 < NBLK and row_of_block[i] == row_of_block[i + 1]:
                continue
            row = bc_s[r0 + row_of_block[i]:r0 + row_of_block[i] + 1, :]
            out = row if out is None else jnp.where(blk <= i, row, out)
        return jnp.broadcast_to(out, (NBLK, out.shape[1]))

    mid_off = BLK // 2 if reverse else BLK // 2 - 1
    mid = per_block([i * BLK + mid_off for i in range(NBLK)])
    tot_row = r0 if reverse else r0 + CHUNK - 1
    tot = bc_s[tot_row:tot_row + 1, :]
    scale = [jnp.exp(mid), jnp.exp(tot - mid)]
    for level in range(LEVELS):
        piv = per_block([_pivot_row(i, level, reverse) for i in range(NBLK)])
        scale.append(jnp.exp(-jnp.abs(mid - piv)))
    for i in range(NBLK):
        rows = slice(r0 + i * BLK, r0 + (i + 1) * BLK)
        d = bc_s[rows, :] - mid[i:i + 1, :]
        qd = q_ref[rows, :].astype(F32) * jnp.exp(jnp.minimum(d, EXP_CLAMP))
        kd = k_ref[rows, :] * jnp.exp(jnp.minimum(-d, EXP_CLAMP))
        ops_s[0, rows, :] = qd.astype(BF16)
        ops_s[1, rows, :] = kd.astype(BF16)
        ops_s[2, rows, :] = (qd * scale[0][i:i + 1, :]).astype(BF16)
        ops_s[3, rows, :] = (kd * scale[1][i:i + 1, :]).astype(BF16)
        for level in range(LEVELS):
            s = scale[2 + level][i:i + 1, :]
            if _q_side(i, level, reverse):
                ops_s[4 + 2 * level, rows, :] = (qd * s).astype(BF16)
            else:
                ops_s[5 + 2 * level, rows, :] = (kd * s).astype(BF16)
    return jnp.exp(tot)


def _pair_masks(reverse):
    t = lax.broadcasted_iota(jnp.int32, (CHUNK, CHUNK), 0)
    s = lax.broadcasted_iota(jnp.int32, (CHUNK, CHUNK), 1)
    causal = (s >= t) if reverse else (s <= t)
    same = [(t // (CHUNK >> level)) == (s // (CHUNK >> level)) for level in range(1, LEVELS + 1)]
    return causal, same


def _scan_mm(c, dec, v_ref, st_ref, ops_s, masks, emit):
    rows = slice(c * CHUNK, (c + 1) * CHUNK)
    causal, same = masks
    attn = []
    for hd in range(HEADS):
        sl = slice(hd * HEAD_DIM, (hd + 1) * HEAD_DIM)
        pair = lambda n: lax.dot_general(ops_s[n, rows, sl], ops_s[n + 1, rows, sl], _NT,
                                         preferred_element_type=F32)
        a = pair(4)
        for level in range(1, LEVELS):
            a = jnp.where(same[level - 1], pair(4 + 2 * level), a)
        a = jnp.where(same[LEVELS - 1], jnp.where(causal, pair(0), 0.0), a)
        attn.append(a.astype(BF16))
    for hd in range(HEADS):
        sl = slice(hd * HEAD_DIM, (hd + 1) * HEAD_DIM)
        st = st_ref[hd]
        v = v_ref[rows, sl]
        lhs = jnp.concatenate([ops_s[2, rows, sl], attn[hd]], axis=1)
        rhs = jnp.concatenate([st.astype(BF16), v], axis=0)
        emit(rows, sl, jnp.dot(lhs, rhs, preferred_element_type=F32))
        kv = lax.dot_general(ops_s[3, rows, sl], v, _TN, preferred_element_type=F32)
        dec_col = jnp.transpose(jnp.broadcast_to(dec[:, sl], (HEAD_DIM, HEAD_DIM)))
        st_ref[hd] = st * dec_col + kv


def _scan_tile(z_ref, q_ref, v_ref, lb, st_ref, ops_s, bc_s, emit, reverse,
               after_gates=None, before_matmuls=None):
    n_chunks = z_ref.shape[0] // CHUNK
    order = range(n_chunks - 1, -1, -1) if reverse else range(n_chunks)
    tri_b = _tri(CHUNK, reverse).astype(BF16)
    tri2 = jnp.concatenate([tri_b, tri_b], axis=1)
    masks = _pair_masks(reverse)
    for c in order:
        _scan_gates(c, z_ref, lb, bc_s, tri2)
    if after_gates is not None:
        after_gates()
    dec = {c: _scan_prep(c, z_ref, q_ref, ops_s, bc_s, reverse) for c in order}
    if before_matmuls is not None:
        before_matmuls()
    for c in order:
        _scan_mm(c, dec[c], v_ref, st_ref, ops_s, masks, emit)


def _init_scan_scratch(s0_ref, st_s, ops_s):
    @pl.when(pl.program_id(1) == 0)
    def _():
        st_s[...] = s0_ref[0]

    @pl.when((pl.program_id(0) == 0) & (pl.program_id(1) == 0))
    def _():
        ops_s[4:] = jnp.zeros((2 * LEVELS,) + ops_s.shape[1:], BF16)


_SCAN_SCRATCH = [
    pltpu.VMEM((HEADS, HEAD_DIM, HEAD_DIM), F32),
    pltpu.VMEM((N_OPS, TILE_T, HG_WIDTH), BF16),
    pltpu.VMEM((TILE_T, HG_WIDTH), F32),
    pltpu.VMEM((TILE_T, HG_WIDTH), F32),
]


def _bwd_kernel(x_ref, mod_ref, nw_ref, wq_ref, wv_ref, wz_ref, lb_ref, s0_ref,
                q_ref, v_ref, ob_ref, st_s, ops_s, bc_s, z_s):
    _init_scan_scratch(s0_ref, st_s, ops_s)
    h = _modulated_norm(x_ref[0], nw_ref[...], mod_ref)
    z_s[...] = jnp.dot(h, wz_ref[...], preferred_element_type=F32)
    q_ref[0] = jnp.dot(h, wq_ref[...], preferred_element_type=F32).astype(BF16)

    def value_proj():
        v_ref[0] = jnp.dot(h, wv_ref[...], preferred_element_type=F32).astype(BF16)

    def emit(rows, sl, o):
        ob_ref[0, rows, sl] = o.astype(BF16)

    _scan_tile(z_s, q_ref.at[0], v_ref.at[0], lb_ref[1:2, :], st_s, ops_s, bc_s, emit, reverse=True,
               after_gates=value_proj)


def _bwd_call(x, mod3, nw, wi, lb, sb):
    bsz, seq, _ = x.shape
    nt = seq // TILE_T
    tile = lambda b, t: (b, nt - 1 - t, 0)
    act = jax.ShapeDtypeStruct((bsz, seq, HG_WIDTH), BF16)
    act_spec = pl.BlockSpec((1, TILE_T, HG_WIDTH), tile)
    return pl.pallas_call(
        _bwd_kernel,
        grid=(bsz, nt),
        in_specs=[
            pl.BlockSpec((1, TILE_T, D_MODEL), tile),
            pl.BlockSpec((1, 3, D_MODEL), lambda b, t: (b, 0, 0)),
            pl.BlockSpec((1, D_MODEL), lambda b, t: (0, 0)),
            _w_in_cols(COL_Q), _w_in_cols(COL_V), _w_in_cols(COL_ZB),
            pl.BlockSpec((2, HG_WIDTH), lambda b, t: (0, 0)),
            pl.BlockSpec((1, HEADS, HEAD_DIM, HEAD_DIM), lambda b, t: (b, 0, 0, 0)),
        ],
        out_specs=[act_spec, act_spec, act_spec],
        out_shape=[act, act, act],
        scratch_shapes=_SCAN_SCRATCH,
        compiler_params=pltpu.CompilerParams(dimension_semantics=("arbitrary", "arbitrary"),
                                             vmem_limit_bytes=VMEM_LIMIT),
        name="bwd_scan",
    )(x, mod3, nw, wi, wi, wi, lb, sb)


def _fwd_kernel(x_ref, mod_ref, nw_ref, wz_ref, wg_ref, wc0_ref, wc1_ref, wc2_ref, wc3_ref,
                lb_ref, s0_ref, q_ref, v_ref, ob_ref, onw_ref, cw_ref, wo_ref, fnw_ref, out_ref,
                st_s, ops_s, bc_s, z_s, o_s, mix_s, g_s, cv_s):
    _init_scan_scratch(s0_ref, st_s, ops_s)
    x = x_ref[0]
    h = _modulated_norm(x, nw_ref[...], mod_ref)

    z_s[...] = jnp.dot(h, wz_ref[...], preferred_element_type=F32)

    def conv_proj():
        for j, wc_ref in enumerate((wc0_ref, wc1_ref, wc2_ref, wc3_ref)):
            cv_s[:, j * CONV_WIDTH:(j + 1) * CONV_WIDTH] = jnp.dot(h, wc_ref[...],
                                                                   preferred_element_type=F32)

    def conv_branch():
        g_s[...] = jnp.dot(h, wg_ref[...], preferred_element_type=F32)
        n_tok = cv_s.shape[0]
        u = cv_s[:, 2 * CONV_WIDTH:3 * CONV_WIDTH] * cv_s[:, 0:CONV_WIDTH]
        col = lax.broadcasted_iota(jnp.int32, (n_tok, 1), 0) % GRID_W
        u_prev = jnp.where(col == 0, 0.0, pltpu.roll(u, 1, axis=0))
        u_next = jnp.where(col == GRID_W - 1, 0.0, pltpu.roll(u, n_tok - 1, axis=0))
        conv = cw_ref[0:1, :] * u_prev + cw_ref[1:2, :] * u + cw_ref[2:3, :] * u_next
        cg = cv_s[:, 3 * CONV_WIDTH:4 * CONV_WIDTH]
        y_cv = cv_s[:, CONV_WIDTH:2 * CONV_WIDTH] * conv * (cg * _sigmoid(cg))
        mix_s[:, HG_WIDTH:] = y_cv.astype(BF16)

    def emit(rows, sl, o):
        o_s[rows, sl] = o + ob_ref[0, rows, sl].astype(F32)

    _scan_tile(z_s, q_ref.at[0], v_ref.at[0], lb_ref[0:1, :], st_s, ops_s, bc_s, emit, reverse=False,
               after_gates=conv_proj, before_matmuls=conv_branch)

    onw = onw_ref[...]
    for hd in range(HEADS):
        sl = slice(hd * HEAD_DIM, (hd + 1) * HEAD_DIM)
        o = o_s[:, sl]
        g = g_s[:, sl]
        y = (o * _rms_rows(o) * onw[:, sl]) * (g * _sigmoid(g))
        mix_s[:, sl] = y.astype(BF16)

    y = jnp.dot(mix_s[...], wo_ref[...], preferred_element_type=F32)
    r = x + mod_ref[0, 2:3, :] * y
    out_ref[0] = (r * _rms_rows(r)) * fnw_ref[...]


def _fwd_call(x, mod3, nw, wi, lb, sf, q, v, ob, onw, conv_w, w_o, fnw):
    bsz, seq, _ = x.shape
    nt = seq // TILE_T
    tile = lambda b, t: (b, t, 0)
    const2 = lambda b, t: (0, 0)
    act_spec = pl.BlockSpec((1, TILE_T, HG_WIDTH), tile)
    return pl.pallas_call(
        _fwd_kernel,
        grid=(bsz, nt),
        in_specs=[
            pl.BlockSpec((1, TILE_T, D_MODEL), tile),
            pl.BlockSpec((1, 3, D_MODEL), lambda b, t: (b, 0, 0)),
            pl.BlockSpec((1, D_MODEL), const2),
            _w_in_cols(COL_ZF), _w_in_cols(COL_GATE),
            _w_in_cols(COL_CONV), _w_in_cols(COL_CONV + 1), _w_in_cols(COL_CONV + 2),
            _w_in_cols(COL_CONV + 3),
            pl.BlockSpec((2, HG_WIDTH), const2),
            pl.BlockSpec((1, HEADS, HEAD_DIM, HEAD_DIM), lambda b, t: (b, 0, 0, 0)),
            act_spec, act_spec, act_spec,
            pl.BlockSpec((1, HG_WIDTH), const2),
            pl.BlockSpec(conv_w.shape, const2),
            pl.BlockSpec(w_o.shape, const2),
            pl.BlockSpec((1, D_MODEL), const2),
        ],
        out_specs=pl.BlockSpec((1, TILE_T, D_MODEL), tile),
        out_shape=jax.ShapeDtypeStruct((bsz, seq, D_MODEL), F32),
        scratch_shapes=_SCAN_SCRATCH + [
            pltpu.VMEM((TILE_T, HG_WIDTH), F32),
            pltpu.VMEM((TILE_T, HG_WIDTH + CONV_WIDTH), BF16),
            pltpu.VMEM((TILE_T, HG_WIDTH), F32),
            pltpu.VMEM((TILE_T, 4 * CONV_WIDTH), F32),
        ],
        compiler_params=pltpu.CompilerParams(dimension_semantics=("arbitrary", "arbitrary"),
                                             vmem_limit_bytes=VMEM_LIMIT),
        name="fwd_scan_out",
    )(x, mod3, nw, wi, wi, wi, wi, wi, wi, lb, sf, q, v, ob, onw, conv_w, w_o, fnw)


def kernel(x, c, ctx, c_ctx, norm_w, w_ada, b_ada, w_in, hg_lb_logits, hg_onorm_w, conv_w,
           w_out, final_norm_w):
    assert w_in.shape[0] == 1, "single-layer block"
    bsz = x.shape[0]
    cc = jnp.concatenate([c, c_ctx[None, :], jnp.zeros((8 - bsz - 1, D_MODEL), F32)], axis=0)
    mod, lb = _mod_call(cc, w_ada[0], b_ada[0][None, :], hg_lb_logits.reshape(-1, HG_WIDTH))
    mod3 = mod.reshape(8, 3, D_MODEL)

    wi = w_in[0].astype(BF16)
    nw = norm_w[0][None, :]

    sf, sb = _ctx_call(ctx, mod3, nw, wi, lb)
    q, v, ob = _bwd_call(x, mod3, nw, wi, lb, sb)
    onw = jnp.tile(hg_onorm_w[0], HEADS)[None, :]
    return _fwd_call(x, mod3, nw, wi, lb, sf, q, v, ob, onw, conv_w[0],
                     w_out[0].astype(BF16), final_norm_w[None, :])
```

```python
import jax
import jax.numpy as jnp
from jax import lax
from jax.experimental import pallas as pl
from jax.experimental.pallas import tpu as pltpu

D_MODEL = 1024
HEADS = 8
HEAD_DIM = 128
HG_WIDTH = HEADS * HEAD_DIM
CONV_WIDTH = 1024
GRID_W = 64
EPS = 1e-6

CHUNK = 128
BLK = 16
NBLK = CHUNK // BLK
LEVELS = NBLK.bit_length() - 1
N_OPS = 4 + LEVELS
TILE_T = 512
EXP_CLAMP = 70.0
VMEM_LIMIT = 56 * 1024 * 1024

COL_Q, COL_V, COL_ZF, COL_ZB, COL_GATE, COL_CONV = 0, 1, 2, 3, 4, 5

F32 = jnp.float32
BF16 = jnp.bfloat16

_NT = (((1,), (1,)), ((), ()))
_TN = (((0,), (0,)), ((), ()))


def _w_in_cols(group):
    return pl.BlockSpec((D_MODEL, HG_WIDTH), lambda *_: (0, group), pipeline_mode=pl.Buffered(1))


def _sigmoid(x):
    return 1.0 / (1.0 + jnp.exp(-x))


def _rms_rows(xf):
    return lax.rsqrt(jnp.mean(xf * xf, axis=-1, keepdims=True) + EPS)


def _tri(n, reverse):
    t = lax.broadcasted_iota(jnp.int32, (n, n), 0)
    r = lax.broadcasted_iota(jnp.int32, (n, n), 1)
    return (r >= t) if reverse else (r <= t)


def _cumsum_time(g, tri2):
    g_hi = g.astype(BF16)
    g_lo = (g - g_hi.astype(F32)).astype(BF16)
    return jnp.dot(tri2, jnp.concatenate([g_hi, g_lo], axis=0), preferred_element_type=F32)


def _gates(z, lb):
    f = lb + (1.0 - lb) * _sigmoid(z)
    return jnp.log(f), 1.0 - f


def _mod_kernel(cc_ref, w_ref, b_ref, lg_ref, mod_ref, lb_ref):
    cc = cc_ref[...]
    s = cc * _sigmoid(cc)
    mod_ref[...] = jnp.dot(s, w_ref[...], precision=lax.Precision.HIGHEST,
                           preferred_element_type=F32) + b_ref[...]
    n_rows = lg_ref.shape[0] // 2
    for d in range(2):
        lg = lg_ref[d * n_rows:(d + 1) * n_rows, :]
        e = jnp.exp(lg - jnp.max(lg, axis=0, keepdims=True))
        lb_ref[d:d + 1, :] = e[0:1, :] / jnp.sum(e, axis=0, keepdims=True)


def _mod_call(cc, w_ada, b_ada, lb_logits):
    n_out = w_ada.shape[1]
    bn = 768
    return pl.pallas_call(
        _mod_kernel,
        grid=(n_out // bn,),
        in_specs=[
            pl.BlockSpec((8, D_MODEL), lambda j: (0, 0)),
            pl.BlockSpec((D_MODEL, bn), lambda j: (0, j)),
            pl.BlockSpec((1, bn), lambda j: (0, j)),
            pl.BlockSpec(lb_logits.shape, lambda j: (0, 0)),
        ],
        out_specs=[
            pl.BlockSpec((8, bn), lambda j: (0, j)),
            pl.BlockSpec((2, HG_WIDTH), lambda j: (0, 0)),
        ],
        out_shape=[
            jax.ShapeDtypeStruct((8, n_out), F32),
            jax.ShapeDtypeStruct((2, HG_WIDTH), F32),
        ],
        compiler_params=pltpu.CompilerParams(dimension_semantics=("arbitrary",)),
        name="adaln_mod",
    )(cc, w_ada, b_ada, lb_logits)


def _modulated_norm(x, nw, mod_ref):
    shift = mod_ref[0, 0:1, :]
    scale = mod_ref[0, 1:2, :]
    y = (x * _rms_rows(x)) * nw
    return (y * (1.0 + scale) + shift).astype(BF16)


def _ctx_kernel(x_ref, mod_ref, nw_ref, wv_ref, wzf_ref, wzb_ref, lb_ref, sf_ref, sb_ref):
    n = x_ref.shape[1]
    h = _modulated_norm(x_ref[0], nw_ref[...], mod_ref)
    v = jnp.dot(h, wv_ref[...], preferred_element_type=F32).astype(BF16)
    for d, wz_ref, out_ref in ((0, wzf_ref, sf_ref), (1, wzb_ref, sb_ref)):
        z = jnp.dot(h, wz_ref[...], preferred_element_type=F32)
        g, kk = _gates(z, lb_ref[d:d + 1, :])
        tri = _tri(n, reverse=bool(d)).astype(BF16)
        bc = _cumsum_time(g, jnp.concatenate([tri, tri], axis=1))
        tot = bc[0:1, :] if d else bc[n - 1:n, :]
        ks = (kk * jnp.exp(tot - bc)).astype(BF16)
        for hd in range(HEADS):
            sl = slice(hd * HEAD_DIM, (hd + 1) * HEAD_DIM)
            out_ref[0, hd] = lax.dot_general(ks[:, sl], v[:, sl], _TN, preferred_element_type=F32)


def _ctx_call(ctx, mod3, nw, wi, lb):
    bsz, n, _ = ctx.shape
    st_shape = jax.ShapeDtypeStruct((bsz, HEADS, HEAD_DIM, HEAD_DIM), F32)
    st_spec = pl.BlockSpec((1, HEADS, HEAD_DIM, HEAD_DIM), lambda b: (b, 0, 0, 0))
    return pl.pallas_call(
        _ctx_kernel,
        grid=(bsz,),
        in_specs=[
            pl.BlockSpec((1, n, D_MODEL), lambda b: (b, 0, 0)),
            pl.BlockSpec((1, 3, D_MODEL), lambda b: (bsz, 0, 0)),
            pl.BlockSpec((1, D_MODEL), lambda b: (0, 0)),
            _w_in_cols(COL_V), _w_in_cols(COL_ZF), _w_in_cols(COL_ZB),
            pl.BlockSpec((2, HG_WIDTH), lambda b: (0, 0)),
        ],
        out_specs=[st_spec, st_spec],
        out_shape=[st_shape, st_shape],
        compiler_params=pltpu.CompilerParams(dimension_semantics=("arbitrary",),
                                             vmem_limit_bytes=VMEM_LIMIT),
        name="ctx_states",
    )(ctx, mod3, nw, wi, wi, wi, lb)


def _pivot_row(i, level, reverse):
    half = NBLK >> (level + 1)
    boundary = ((i // (2 * half)) * 2 + 1) * half * BLK
    return boundary if reverse else boundary - 1


def _q_side(i, level, reverse):
    later = (i // (NBLK >> (level + 1))) % 2 == 1
    return later != reverse


def _scan_gates(c, z_ref, lb, bc_s, tri2):
    rows = slice(c * CHUNK, (c + 1) * CHUNK)
    g, kk = _gates(z_ref[rows, :], lb)
    bc_s[rows, :] = _cumsum_time(g, tri2)
    z_ref[rows, :] = kk


def _scan_prep(c, k_ref, q_ref, ops_s, bc_s, reverse):
    r0 = c * CHUNK
    blk = lax.broadcasted_iota(jnp.int32, (NBLK, 1), 0)

    def per_block(row_of_block):
        out = None
        for i in reversed(range(NBLK)):
            if i + 1 < NBLK and row_of_block[i] == row_of_block[i + 1]:
                continue
            row = bc_s[r0 + row_of_block[i]:r0 + row_of_block[i] + 1, :]
            out = row if out is None else jnp.where(blk <= i, row, out)
        return jnp.broadcast_to(out, (NBLK, out.shape[1]))

    mid_off = BLK // 2 if reverse else BLK // 2 - 1
    mid = per_block([i * BLK + mid_off for i in range(NBLK)])
    tot_row = r0 if reverse else r0 + CHUNK - 1
    tot = bc_s[tot_row:tot_row + 1, :]
    scale = [jnp.exp(mid), jnp.exp(tot - mid)]
    for level in range(LEVELS):
        piv = per_block([_pivot_row(i, level, reverse) for i in range(NBLK)])
        scale.append(jnp.exp(-jnp.abs(mid - piv)))
    for i in range(NBLK):
        rows = slice(r0 + i * BLK, r0 + (i + 1) * BLK)
        d = bc_s[rows, :] - mid[i:i + 1, :]
        qd = q_ref[rows, :].astype(F32) * jnp.exp(jnp.minimum(d, EXP_CLAMP))
        kd = k_ref[rows, :] * jnp.exp(jnp.minimum(-d, EXP_CLAMP))
        ops_s[0, rows, :] = qd.astype(BF16)
        ops_s[1, rows, :] = kd.astype(BF16)
        ops_s[2, rows, :] = (qd * scale[0][i:i + 1, :]).astype(BF16)
        ops_s[3, rows, :] = (kd * scale[1][i:i + 1, :]).astype(BF16)
        for level in range(LEVELS):
            side = qd if _q_side(i, level, reverse) else kd
            ops_s[4 + level, rows, :] = (side * scale[2 + level][i:i + 1, :]).astype(BF16)
    return jnp.exp(tot)


def _pair_masks(reverse):
    t = lax.broadcasted_iota(jnp.int32, (CHUNK, CHUNK), 0)
    s = lax.broadcasted_iota(jnp.int32, (CHUNK, CHUNK), 1)
    early, late = (t, s) if reverse else (s, t)
    masks = [((t // BLK) == (s // BLK)) & (early <= late)]
    for level in range(LEVELS):
        half = CHUNK >> (level + 1)
        masks.append(((late // half) == (early // half) + 1) & ((early // half) % 2 == 0))
    return masks


def _scan_mm(c, dec, v_ref, st_ref, ops_s, masks, emit):
    rows = slice(c * CHUNK, (c + 1) * CHUNK)
    attn = []
    for hd in range(HEADS):
        sl = slice(hd * HEAD_DIM, (hd + 1) * HEAD_DIM)
        pair = lambda m, n: lax.dot_general(ops_s[m, rows, sl], ops_s[n, rows, sl], _NT,
                                            preferred_element_type=F32)
        a = jnp.where(masks[0], pair(0, 1), 0.0)
        for level in range(LEVELS):
            a = jnp.where(masks[1 + level], pair(4 + level, 4 + level), a)
        attn.append(a.astype(BF16))
    for hd in range(HEADS):
        sl = slice(hd * HEAD_DIM, (hd + 1) * HEAD_DIM)
        st = st_ref[hd]
        v = v_ref[rows, sl]
        lhs = jnp.concatenate([ops_s[2, rows, sl], attn[hd]], axis=1)
        rhs = jnp.concatenate([st.astype(BF16), v], axis=0)
        emit(rows, sl, jnp.dot(lhs, rhs, preferred_element_type=F32))
        kv = lax.dot_general(ops_s[3, rows, sl], v, _TN, preferred_element_type=F32)
        dec_col = jnp.transpose(jnp.broadcast_to(dec[:, sl], (HEAD_DIM, HEAD_DIM)))
        st_ref[hd] = st * dec_col + kv


def _scan_tile(z_ref, q_ref, v_ref, lb, st_ref, ops_s, bc_s, emit, reverse,
               after_gates=None, before_matmuls=None):
    n_chunks = z_ref.shape[0] // CHUNK
    order = range(n_chunks - 1, -1, -1) if reverse else range(n_chunks)
    tri_b = _tri(CHUNK, reverse).astype(BF16)
    tri2 = jnp.concatenate([tri_b, tri_b], axis=1)
    masks = _pair_masks(reverse)
    for c in order:
        _scan_gates(c, z_ref, lb, bc_s, tri2)
    if after_gates is not None:
        after_gates()
    dec = {c: _scan_prep(c, z_ref, q_ref, ops_s, bc_s, reverse) for c in order}
    if before_matmuls is not None:
        before_matmuls()
    for c in order:
        _scan_mm(c, dec[c], v_ref, st_ref, ops_s, masks, emit)


def _init_state(s0_ref, st_s):
    @pl.when(pl.program_id(1) == 0)
    def _():
        st_s[...] = s0_ref[0]


_SCAN_SCRATCH = [
    pltpu.VMEM((HEADS, HEAD_DIM, HEAD_DIM), F32),
    pltpu.VMEM((N_OPS, TILE_T, HG_WIDTH), BF16),
    pltpu.VMEM((TILE_T, HG_WIDTH), F32),
    pltpu.VMEM((TILE_T, HG_WIDTH), F32),
]


def _bwd_kernel(x_ref, mod_ref, nw_ref, wq_ref, wv_ref, wz_ref, lb_ref, s0_ref,
                q_ref, v_ref, ob_ref, st_s, ops_s, bc_s, z_s):
    _init_state(s0_ref, st_s)
    h = _modulated_norm(x_ref[0], nw_ref[...], mod_ref)
    z_s[...] = jnp.dot(h, wz_ref[...], preferred_element_type=F32)
    q_ref[0] = jnp.dot(h, wq_ref[...], preferred_element_type=F32).astype(BF16)

    def value_proj():
        v_ref[0] = jnp.dot(h, wv_ref[...], preferred_element_type=F32).astype(BF16)

    def emit(rows, sl, o):
        ob_ref[0, rows, sl] = o.astype(BF16)

    _scan_tile(z_s, q_ref.at[0], v_ref.at[0], lb_ref[1:2, :], st_s, ops_s, bc_s, emit, reverse=True,
               after_gates=value_proj)


def _bwd_call(x, mod3, nw, wi, lb, sb):
    bsz, seq, _ = x.shape
    nt = seq // TILE_T
    tile = lambda b, t: (b, nt - 1 - t, 0)
    act = jax.ShapeDtypeStruct((bsz, seq, HG_WIDTH), BF16)
    act_spec = pl.BlockSpec((1, TILE_T, HG_WIDTH), tile)
    return pl.pallas_call(
        _bwd_kernel,
        grid=(bsz, nt),
        in_specs=[
            pl.BlockSpec((1, TILE_T, D_MODEL), tile),
            pl.BlockSpec((1, 3, D_MODEL), lambda b, t: (b, 0, 0)),
            pl.BlockSpec((1, D_MODEL), lambda b, t: (0, 0)),
            _w_in_cols(COL_Q), _w_in_cols(COL_V), _w_in_cols(COL_ZB),
            pl.BlockSpec((2, HG_WIDTH), lambda b, t: (0, 0)),
            pl.BlockSpec((1, HEADS, HEAD_DIM, HEAD_DIM), lambda b, t: (b, 0, 0, 0)),
        ],
        out_specs=[act_spec, act_spec, act_spec],
        out_shape=[act, act, act],
        scratch_shapes=_SCAN_SCRATCH,
        compiler_params=pltpu.CompilerParams(dimension_semantics=("arbitrary", "arbitrary"),
                                             vmem_limit_bytes=VMEM_LIMIT),
        name="bwd_scan",
    )(x, mod3, nw, wi, wi, wi, lb, sb)


def _fwd_kernel(x_ref, mod_ref, nw_ref, wz_ref, wg_ref, wc0_ref, wc1_ref, wc2_ref, wc3_ref,
                lb_ref, s0_ref, q_ref, v_ref, ob_ref, onw_ref, cw_ref, wo_ref, fnw_ref, out_ref,
                st_s, ops_s, bc_s, z_s, mix_s, cv_s):
    _init_state(s0_ref, st_s)
    g_s = z_s
    o_s = bc_s
    x = x_ref[0]
    h = _modulated_norm(x, nw_ref[...], mod_ref)

    z_s[...] = jnp.dot(h, wz_ref[...], preferred_element_type=F32)

    def conv_proj():
        for j, wc_ref in enumerate((wc0_ref, wc1_ref, wc2_ref, wc3_ref)):
            cv_s[:, j * CONV_WIDTH:(j + 1) * CONV_WIDTH] = jnp.dot(h, wc_ref[...],
                                                                   preferred_element_type=F32)

    def conv_branch():
        g_s[...] = jnp.dot(h, wg_ref[...], preferred_element_type=F32)
        n_tok = cv_s.shape[0]
        u = cv_s[:, 2 * CONV_WIDTH:3 * CONV_WIDTH] * cv_s[:, 0:CONV_WIDTH]
        col = lax.broadcasted_iota(jnp.int32, (n_tok, 1), 0) % GRID_W
        u_prev = jnp.where(col == 0, 0.0, pltpu.roll(u, 1, axis=0))
        u_next = jnp.where(col == GRID_W - 1, 0.0, pltpu.roll(u, n_tok - 1, axis=0))
        conv = cw_ref[0:1, :] * u_prev + cw_ref[1:2, :] * u + cw_ref[2:3, :] * u_next
        cg = cv_s[:, 3 * CONV_WIDTH:4 * CONV_WIDTH]
        y_cv = cv_s[:, CONV_WIDTH:2 * CONV_WIDTH] * conv * (cg * _sigmoid(cg))
        mix_s[:, HG_WIDTH:] = y_cv.astype(BF16)

    def emit(rows, sl, o):
        o_s[rows, sl] = o + ob_ref[0, rows, sl].astype(F32)

    _scan_tile(z_s, q_ref.at[0], v_ref.at[0], lb_ref[0:1, :], st_s, ops_s, bc_s, emit, reverse=False,
               after_gates=conv_proj, before_matmuls=conv_branch)

    onw = onw_ref[...]
    for hd in range(HEADS):
        sl = slice(hd * HEAD_DIM, (hd + 1) * HEAD_DIM)
        o = o_s[:, sl]
        g = g_s[:, sl]
        y = (o * _rms_rows(o) * onw[:, sl]) * (g * _sigmoid(g))
        mix_s[:, sl] = y.astype(BF16)

    y = jnp.dot(mix_s[...], wo_ref[...], preferred_element_type=F32)
    r = x + mod_ref[0, 2:3, :] * y
    out_ref[0] = (r * _rms_rows(r)) * fnw_ref[...]


def _fwd_call(x, mod3, nw, wi, lb, sf, q, v, ob, onw, conv_w, w_o, fnw):
    bsz, seq, _ = x.shape
    nt = seq // TILE_T
    tile = lambda b, t: (b, t, 0)
    const2 = lambda b, t: (0, 0)
    act_spec = pl.BlockSpec((1, TILE_T, HG_WIDTH), tile)
    return pl.pallas_call(
        _fwd_kernel,
        grid=(bsz, nt),
        in_specs=[
            pl.BlockSpec((1, TILE_T, D_MODEL), tile),
            pl.BlockSpec((1, 3, D_MODEL), lambda b, t: (b, 0, 0)),
            pl.BlockSpec((1, D_MODEL), const2),
            _w_in_cols(COL_ZF), _w_in_cols(COL_GATE),
            _w_in_cols(COL_CONV), _w_in_cols(COL_CONV + 1), _w_in_cols(COL_CONV + 2),
            _w_in_cols(COL_CONV + 3),
            pl.BlockSpec((2, HG_WIDTH), const2),
            pl.BlockSpec((1, HEADS, HEAD_DIM, HEAD_DIM), lambda b, t: (b, 0, 0, 0)),
            act_spec, act_spec, act_spec,
            pl.BlockSpec((1, HG_WIDTH), const2),
            pl.BlockSpec(conv_w.shape, const2),
            pl.BlockSpec(w_o.shape, const2, pipeline_mode=pl.Buffered(1)),
            pl.BlockSpec((1, D_MODEL), const2),
        ],
        out_specs=pl.BlockSpec((1, TILE_T, D_MODEL), tile),
        out_shape=jax.ShapeDtypeStruct((bsz, seq, D_MODEL), F32),
        scratch_shapes=_SCAN_SCRATCH + [
            pltpu.VMEM((TILE_T, HG_WIDTH + CONV_WIDTH), BF16),
            pltpu.VMEM((TILE_T, 4 * CONV_WIDTH), F32),
        ],
        compiler_params=pltpu.CompilerParams(dimension_semantics=("arbitrary", "arbitrary"),
                                             vmem_limit_bytes=VMEM_LIMIT),
        name="fwd_scan_out",
    )(x, mod3, nw, wi, wi, wi, wi, wi, wi, lb, sf, q, v, ob, onw, conv_w, w_o, fnw)


def kernel(x, c, ctx, c_ctx, norm_w, w_ada, b_ada, w_in, hg_lb_logits, hg_onorm_w, conv_w,
           w_out, final_norm_w):
    assert w_in.shape[0] == 1, "single-layer block"
    bsz = x.shape[0]
    cc = jnp.concatenate([c, c_ctx[None, :], jnp.zeros((8 - bsz - 1, D_MODEL), F32)], axis=0)
    mod, lb = _mod_call(cc, w_ada[0], b_ada[0][None, :], hg_lb_logits.reshape(-1, HG_WIDTH))
    mod3 = mod.reshape(8, 3, D_MODEL)

    wi = w_in[0].astype(BF16)
    nw = norm_w[0][None, :]

    sf, sb = _ctx_call(ctx, mod3, nw, wi, lb)
    q, v, ob = _bwd_call(x, mod3, nw, wi, lb, sb)
    onw = jnp.tile(hg_onorm_w[0], HEADS)[None, :]
    return _fwd_call(x, mod3, nw, wi, lb, sf, q, v, ob, onw, conv_w[0],
                     w_out[0].astype(BF16), final_norm_w[None, :])
```

```python
import jax
import jax.numpy as jnp
from jax import lax
from jax.experimental import pallas as pl
from jax.experimental.pallas import tpu as pltpu

D_MODEL = 1024
HEADS = 8
HEAD_DIM = 128
HG_WIDTH = HEADS * HEAD_DIM
CONV_WIDTH = 1024
GRID_W = 64
EPS = 1e-6

CHUNK = 128
BLK = 16
NBLK = CHUNK // BLK
LEVELS = NBLK.bit_length() - 1
N_OPS = 4 + LEVELS
TILE_T = 512
EXP_CLAMP = 70.0
VMEM_LIMIT = 56 * 1024 * 1024

COL_Q, COL_V, COL_ZF, COL_ZB, COL_GATE, COL_CONV = 0, 1, 2, 3, 4, 5

F32 = jnp.float32
BF16 = jnp.bfloat16

_NT = (((1,), (1,)), ((), ()))
_TN = (((0,), (0,)), ((), ()))


def _w_in_cols(group):
    return pl.BlockSpec((D_MODEL, HG_WIDTH), lambda *_: (0, group), pipeline_mode=pl.Buffered(1))


def _sigmoid(x):
    return 1.0 / (1.0 + jnp.exp(-x))


def _rms_rows(xf):
    return lax.rsqrt(jnp.mean(xf * xf, axis=-1, keepdims=True) + EPS)


def _tri(n, reverse):
    t = lax.broadcasted_iota(jnp.int32, (n, n), 0)
    r = lax.broadcasted_iota(jnp.int32, (n, n), 1)
    return (r >= t) if reverse else (r <= t)


def _cumsum_time(g, tri2):
    g_hi = g.astype(BF16)
    g_lo = (g - g_hi.astype(F32)).astype(BF16)
    return jnp.dot(tri2, jnp.concatenate([g_hi, g_lo], axis=0), preferred_element_type=F32)


def _gates(z, lb):
    f = lb + (1.0 - lb) * _sigmoid(z)
    return jnp.log(f), 1.0 - f


def _mod_kernel(cc_ref, w_ref, b_ref, lg_ref, mod_ref, lb_ref):
    cc = cc_ref[...]
    s = cc * _sigmoid(cc)
    mod_ref[...] = jnp.dot(s, w_ref[...], precision=lax.Precision.HIGHEST,
                           preferred_element_type=F32) + b_ref[...]
    n_rows = lg_ref.shape[0] // 2
    for d in range(2):
        lg = lg_ref[d * n_rows:(d + 1) * n_rows, :]
        e = jnp.exp(lg - jnp.max(lg, axis=0, keepdims=True))
        lb_ref[d:d + 1, :] = e[0:1, :] / jnp.sum(e, axis=0, keepdims=True)


def _mod_call(cc, w_ada, b_ada, lb_logits):
    n_out = w_ada.shape[1]
    bn = 768
    return pl.pallas_call(
        _mod_kernel,
        grid=(n_out // bn,),
        in_specs=[
            pl.BlockSpec((8, D_MODEL), lambda j: (0, 0)),
            pl.BlockSpec((D_MODEL, bn), lambda j: (0, j)),
            pl.BlockSpec((1, bn), lambda j: (0, j)),
            pl.BlockSpec(lb_logits.shape, lambda j: (0, 0)),
        ],
        out_specs=[
            pl.BlockSpec((8, bn), lambda j: (0, j)),
            pl.BlockSpec((2, HG_WIDTH), lambda j: (0, 0)),
        ],
        out_shape=[
            jax.ShapeDtypeStruct((8, n_out), F32),
            jax.ShapeDtypeStruct((2, HG_WIDTH), F32),
        ],
        compiler_params=pltpu.CompilerParams(dimension_semantics=("arbitrary",)),
        name="adaln_mod",
    )(cc, w_ada, b_ada, lb_logits)


def _modulated_norm(x, nw, mod_ref):
    shift = mod_ref[0, 0:1, :]
    scale = mod_ref[0, 1:2, :]
    y = (x * _rms_rows(x)) * nw
    return (y * (1.0 + scale) + shift).astype(BF16)


def _ctx_kernel(x_ref, mod_ref, nw_ref, wv_ref, wzf_ref, wzb_ref, lb_ref, sf_ref, sb_ref):
    n = x_ref.shape[1]
    h = _modulated_norm(x_ref[0], nw_ref[...], mod_ref)
    v = jnp.dot(h, wv_ref[...], preferred_element_type=F32).astype(BF16)
    for d, wz_ref, out_ref in ((0, wzf_ref, sf_ref), (1, wzb_ref, sb_ref)):
        z = jnp.dot(h, wz_ref[...], preferred_element_type=F32)
        g, kk = _gates(z, lb_ref[d:d + 1, :])
        tri = _tri(n, reverse=bool(d)).astype(BF16)
        bc = _cumsum_time(g, jnp.concatenate([tri, tri], axis=1))
        tot = bc[0:1, :] if d else bc[n - 1:n, :]
        ks = (kk * jnp.exp(tot - bc)).astype(BF16)
        for hd in range(HEADS):
            sl = slice(hd * HEAD_DIM, (hd + 1) * HEAD_DIM)
            out_ref[0, hd] = lax.dot_general(ks[:, sl], v[:, sl], _TN, preferred_element_type=F32)


def _ctx_call(ctx, mod3, nw, wi, lb):
    bsz, n, _ = ctx.shape
    st_shape = jax.ShapeDtypeStruct((bsz, HEADS, HEAD_DIM, HEAD_DIM), F32)
    st_spec = pl.BlockSpec((1, HEADS, HEAD_DIM, HEAD_DIM), lambda b: (b, 0, 0, 0))
    return pl.pallas_call(
        _ctx_kernel,
        grid=(bsz,),
        in_specs=[
            pl.BlockSpec((1, n, D_MODEL), lambda b: (b, 0, 0)),
            pl.BlockSpec((1, 3, D_MODEL), lambda b: (bsz, 0, 0)),
            pl.BlockSpec((1, D_MODEL), lambda b: (0, 0)),
            _w_in_cols(COL_V), _w_in_cols(COL_ZF), _w_in_cols(COL_ZB),
            pl.BlockSpec((2, HG_WIDTH), lambda b: (0, 0)),
        ],
        out_specs=[st_spec, st_spec],
        out_shape=[st_shape, st_shape],
        compiler_params=pltpu.CompilerParams(dimension_semantics=("arbitrary",),
                                             vmem_limit_bytes=VMEM_LIMIT),
        name="ctx_states",
    )(ctx, mod3, nw, wi, wi, wi, lb)


def _pivot_row(i, level, reverse):
    half = NBLK >> (level + 1)
    boundary = ((i // (2 * half)) * 2 + 1) * half * BLK
    return boundary if reverse else boundary - 1


def _q_side(i, level, reverse):
    later = (i // (NBLK >> (level + 1))) % 2 == 1
    return later != reverse


def _scan_gates(c, z_ref, lb, bc_s, tri2):
    rows = slice(c * CHUNK, (c + 1) * CHUNK)
    g, kk = _gates(z_ref[rows, :], lb)
    bc_s[rows, :] = _cumsum_time(g, tri2)
    z_ref[rows, :] = kk


def _scan_prep(c, k_ref, q_ref, ops_s, bc_s, reverse):
    r0 = c * CHUNK
    blk = lax.broadcasted_iota(jnp.int32, (NBLK, 1), 0)

    def per_block(row_of_block):
        out = None
        for i in reversed(range(NBLK)):
            if i + 1 < NBLK and row_of_block[i] == row_of_block[i + 1]:
                continue
            row = bc_s[r0 + row_of_block[i]:r0 + row_of_block[i] + 1, :]
            out = row if out is None else jnp.where(blk <= i, row, out)
        return jnp.broadcast_to(out, (NBLK, out.shape[1]))

    mid_off = BLK // 2 if reverse else BLK // 2 - 1
    mid = per_block([i * BLK + mid_off for i in range(NBLK)])
    tot_row = r0 if reverse else r0 + CHUNK - 1
    tot = bc_s[tot_row:tot_row + 1, :]
    scale = [jnp.exp(mid), jnp.exp(tot - mid)]
    for level in range(LEVELS):
        piv = per_block([_pivot_row(i, level, reverse) for i in range(NBLK)])
        scale.append(jnp.exp(-jnp.abs(mid - piv)))
    for i in range(NBLK):
        rows = slice(r0 + i * BLK, r0 + (i + 1) * BLK)
        d = bc_s[rows, :] - mid[i:i + 1, :]
        qd = q_ref[rows, :].astype(F32) * jnp.exp(jnp.minimum(d, EXP_CLAMP))
        kd = k_ref[rows, :] * jnp.exp(jnp.minimum(-d, EXP_CLAMP))
        ops_s[0, rows, :] = qd.astype(BF16)
        ops_s[1, rows, :] = kd.astype(BF16)
        ops_s[2, rows, :] = (qd * scale[0][i:i + 1, :]).astype(BF16)
        ops_s[3, rows, :] = (kd * scale[1][i:i + 1, :]).astype(BF16)
        for level in range(LEVELS):
            side = qd if _q_side(i, level, reverse) else kd
            ops_s[4 + level, rows, :] = (side * scale[2 + level][i:i + 1, :]).astype(BF16)
    return jnp.exp(tot)


def _pair_masks(reverse):
    t = lax.broadcasted_iota(jnp.int32, (CHUNK, CHUNK), 0)
    s = lax.broadcasted_iota(jnp.int32, (CHUNK, CHUNK), 1)
    early, late = (t, s) if reverse else (s, t)
    masks = [((t // BLK) == (s // BLK)) & (early <= late)]
    for level in range(LEVELS):
        half = CHUNK >> (level + 1)
        masks.append(((late // half) == (early // half) + 1) & ((early // half) % 2 == 0))
    return masks


def _scan_mm(c, dec, v_ref, st_ref, ops_s, masks, emit):
    rows = slice(c * CHUNK, (c + 1) * CHUNK)
    attn = []
    for hd in range(HEADS):
        sl = slice(hd * HEAD_DIM, (hd + 1) * HEAD_DIM)
        pair = lambda m, n: lax.dot_general(ops_s[m, rows, sl], ops_s[n, rows, sl], _NT,
                                            preferred_element_type=F32)
        a = jnp.where(masks[0], pair(0, 1), 0.0)
        for level in range(LEVELS):
            a = jnp.where(masks[1 + level], pair(4 + level, 4 + level), a)
        attn.append(a.astype(BF16))
    for hd in range(HEADS):
        sl = slice(hd * HEAD_DIM, (hd + 1) * HEAD_DIM)
        st = st_ref[hd]
        v = v_ref[rows, sl]
        lhs = jnp.concatenate([ops_s[2, rows, sl], attn[hd]], axis=1)
        rhs = jnp.concatenate([st.astype(BF16), v], axis=0)
        emit(rows, sl, jnp.dot(lhs, rhs, preferred_element_type=F32))
        kv = lax.dot_general(ops_s[3, rows, sl], v, _TN, preferred_element_type=F32)
        dec_col = jnp.transpose(jnp.broadcast_to(dec[:, sl], (HEAD_DIM, HEAD_DIM)))
        st_ref[hd] = st * dec_col + kv


def _scan_tile(z_ref, q_ref, v_ref, lb, st_ref, ops_s, bc_s, emit, reverse,
               after_gates=None, before_matmuls=None, after_chunk=None):
    n_chunks = z_ref.shape[0] // CHUNK
    order = range(n_chunks - 1, -1, -1) if reverse else range(n_chunks)
    tri_b = _tri(CHUNK, reverse).astype(BF16)
    tri2 = jnp.concatenate([tri_b, tri_b], axis=1)
    masks = _pair_masks(reverse)
    for c in order:
        _scan_gates(c, z_ref, lb, bc_s, tri2)
    if after_gates is not None:
        after_gates()
    dec = {c: _scan_prep(c, z_ref, q_ref, ops_s, bc_s, reverse) for c in order}
    if before_matmuls is not None:
        before_matmuls()
    for c in order:
        _scan_mm(c, dec[c], v_ref, st_ref, ops_s, masks, emit)
        if after_chunk is not None:
            after_chunk(slice(c * CHUNK, (c + 1) * CHUNK))


def _init_state(s0_ref, st_s):
    @pl.when(pl.program_id(1) == 0)
    def _():
        st_s[...] = s0_ref[0]


_SCAN_SCRATCH = [
    pltpu.VMEM((HEADS, HEAD_DIM, HEAD_DIM), F32),
    pltpu.VMEM((N_OPS, TILE_T, HG_WIDTH), BF16),
    pltpu.VMEM((TILE_T, HG_WIDTH), F32),
    pltpu.VMEM((TILE_T, HG_WIDTH), F32),
]


def _bwd_kernel(x_ref, mod_ref, nw_ref, wq_ref, wv_ref, wz_ref, lb_ref, s0_ref,
                q_ref, v_ref, ob_ref, st_s, ops_s, bc_s, z_s):
    _init_state(s0_ref, st_s)
    h = _modulated_norm(x_ref[0], nw_ref[...], mod_ref)
    z_s[...] = jnp.dot(h, wz_ref[...], preferred_element_type=F32)
    q_ref[0] = jnp.dot(h, wq_ref[...], preferred_element_type=F32).astype(BF16)

    def value_proj():
        v_ref[0] = jnp.dot(h, wv_ref[...], preferred_element_type=F32).astype(BF16)

    def emit(rows, sl, o):
        ob_ref[0, rows, sl] = o.astype(BF16)

    _scan_tile(z_s, q_ref.at[0], v_ref.at[0], lb_ref[1:2, :], st_s, ops_s, bc_s, emit, reverse=True,
               after_gates=value_proj)


def _bwd_call(x, mod3, nw, wi, lb, sb):
    bsz, seq, _ = x.shape
    nt = seq // TILE_T
    tile = lambda b, t: (b, nt - 1 - t, 0)
    act = jax.ShapeDtypeStruct((bsz, seq, HG_WIDTH), BF16)
    act_spec = pl.BlockSpec((1, TILE_T, HG_WIDTH), tile)
    return pl.pallas_call(
        _bwd_kernel,
        grid=(bsz, nt),
        in_specs=[
            pl.BlockSpec((1, TILE_T, D_MODEL), tile),
            pl.BlockSpec((1, 3, D_MODEL), lambda b, t: (b, 0, 0)),
            pl.BlockSpec((1, D_MODEL), lambda b, t: (0, 0)),
            _w_in_cols(COL_Q), _w_in_cols(COL_V), _w_in_cols(COL_ZB),
            pl.BlockSpec((2, HG_WIDTH), lambda b, t: (0, 0)),
            pl.BlockSpec((1, HEADS, HEAD_DIM, HEAD_DIM), lambda b, t: (b, 0, 0, 0)),
        ],
        out_specs=[act_spec, act_spec, act_spec],
        out_shape=[act, act, act],
        scratch_shapes=_SCAN_SCRATCH,
        compiler_params=pltpu.CompilerParams(dimension_semantics=("arbitrary", "arbitrary"),
                                             vmem_limit_bytes=VMEM_LIMIT),
        name="bwd_scan",
    )(x, mod3, nw, wi, wi, wi, lb, sb)


def _fwd_kernel(x_ref, mod_ref, nw_ref, wz_ref, wg_ref, wc0_ref, wc1_ref, wc2_ref, wc3_ref,
                lb_ref, s0_ref, q_ref, v_ref, ob_ref, onw_ref, cw_ref, wo_ref, fnw_ref, out_ref,
                st_s, ops_s, bc_s, z_s, mix_s, cv_s):
    _init_state(s0_ref, st_s)
    g_s = z_s
    o_s = bc_s
    h = _modulated_norm(x_ref[0], nw_ref[...], mod_ref)

    z_s[...] = jnp.dot(h, wz_ref[...], preferred_element_type=F32)

    def conv_proj(j, wc_ref):
        cv_s[:, j * CONV_WIDTH:(j + 1) * CONV_WIDTH] = jnp.dot(h, wc_ref[...],
                                                               preferred_element_type=F32)

    conv_proj(0, wc0_ref)

    def conv_proj_rest():
        for j, wc_ref in enumerate((wc1_ref, wc2_ref, wc3_ref), start=1):
            conv_proj(j, wc_ref)

    def conv_branch():
        g_s[...] = jnp.dot(h, wg_ref[...], preferred_element_type=F32)
        n_tok = cv_s.shape[0]
        u = cv_s[:, 2 * CONV_WIDTH:3 * CONV_WIDTH] * cv_s[:, 0:CONV_WIDTH]
        col = lax.broadcasted_iota(jnp.int32, (n_tok, 1), 0) % GRID_W
        u_prev = jnp.where(col == 0, 0.0, pltpu.roll(u, 1, axis=0))
        u_next = jnp.where(col == GRID_W - 1, 0.0, pltpu.roll(u, n_tok - 1, axis=0))
        conv = cw_ref[0:1, :] * u_prev + cw_ref[1:2, :] * u + cw_ref[2:3, :] * u_next
        cg = cv_s[:, 3 * CONV_WIDTH:4 * CONV_WIDTH]
        y_cv = cv_s[:, CONV_WIDTH:2 * CONV_WIDTH] * conv * (cg * _sigmoid(cg))
        mix_s[:, HG_WIDTH:] = y_cv.astype(BF16)

    def emit(rows, sl, o):
        o_s[rows, sl] = o + ob_ref[0, rows, sl].astype(F32)

    def gated_readout(rows):
        onw = onw_ref[...]
        for hd in range(HEADS):
            sl = slice(hd * HEAD_DIM, (hd + 1) * HEAD_DIM)
            o = o_s[rows, sl]
            g = g_s[rows, sl]
            y = (o * _rms_rows(o) * onw[:, sl]) * (g * _sigmoid(g))
            mix_s[rows, sl] = y.astype(BF16)

    _scan_tile(z_s, q_ref.at[0], v_ref.at[0], lb_ref[0:1, :], st_s, ops_s, bc_s, emit, reverse=False,
               after_gates=conv_proj_rest, before_matmuls=conv_branch, after_chunk=gated_readout)

    half = x_ref.shape[1] // 2
    for rows in (slice(0, half), slice(half, 2 * half)):
        y = jnp.dot(mix_s[rows, :], wo_ref[...], preferred_element_type=F32)
        r = x_ref[0, rows, :] + mod_ref[0, 2:3, :] * y
        out_ref[0, rows, :] = (r * _rms_rows(r)) * fnw_ref[...]


def _fwd_call(x, mod3, nw, wi, lb, sf, q, v, ob, onw, conv_w, w_o, fnw):
    bsz, seq, _ = x.shape
    nt = seq // TILE_T
    tile = lambda b, t: (b, t, 0)
    const2 = lambda b, t: (0, 0)
    act_spec = pl.BlockSpec((1, TILE_T, HG_WIDTH), tile)
    return pl.pallas_call(
        _fwd_kernel,
        grid=(bsz, nt),
        in_specs=[
            pl.BlockSpec((1, TILE_T, D_MODEL), tile),
            pl.BlockSpec((1, 3, D_MODEL), lambda b, t: (b, 0, 0)),
            pl.BlockSpec((1, D_MODEL), const2),
            _w_in_cols(COL_ZF), _w_in_cols(COL_GATE),
            _w_in_cols(COL_CONV), _w_in_cols(COL_CONV + 1), _w_in_cols(COL_CONV + 2),
            _w_in_cols(COL_CONV + 3),
            pl.BlockSpec((2, HG_WIDTH), const2),
            pl.BlockSpec((1, HEADS, HEAD_DIM, HEAD_DIM), lambda b, t: (b, 0, 0, 0)),
            act_spec, act_spec, act_spec,
            pl.BlockSpec((1, HG_WIDTH), const2),
            pl.BlockSpec(conv_w.shape, const2),
            pl.BlockSpec(w_o.shape, const2, pipeline_mode=pl.Buffered(1)),
            pl.BlockSpec((1, D_MODEL), const2),
        ],
        out_specs=pl.BlockSpec((1, TILE_T, D_MODEL), tile),
        out_shape=jax.ShapeDtypeStruct((bsz, seq, D_MODEL), F32),
        scratch_shapes=_SCAN_SCRATCH + [
            pltpu.VMEM((TILE_T, HG_WIDTH + CONV_WIDTH), BF16),
            pltpu.VMEM((TILE_T, 4 * CONV_WIDTH), F32),
        ],
        compiler_params=pltpu.CompilerParams(dimension_semantics=("arbitrary", "arbitrary"),
                                             vmem_limit_bytes=VMEM_LIMIT),
        name="fwd_scan_out",
    )(x, mod3, nw, wi, wi, wi, wi, wi, wi, lb, sf, q, v, ob, onw, conv_w, w_o, fnw)


def kernel(x, c, ctx, c_ctx, norm_w, w_ada, b_ada, w_in, hg_lb_logits, hg_onorm_w, conv_w,
           w_out, final_norm_w):
    assert w_in.shape[0] == 1, "single-layer block"
    bsz = x.shape[0]
    cc = jnp.concatenate([c, c_ctx[None, :], jnp.zeros((8 - bsz - 1, D_MODEL), F32)], axis=0)
    mod, lb = _mod_call(cc, w_ada[0], b_ada[0][None, :], hg_lb_logits.reshape(-1, HG_WIDTH))
    mod3 = mod.reshape(8, 3, D_MODEL)

    wi = w_in[0].astype(BF16)
    nw = norm_w[0][None, :]

    sf, sb = _ctx_call(ctx, mod3, nw, wi, lb)
    q, v, ob = _bwd_call(x, mod3, nw, wi, lb, sb)
    onw = jnp.tile(hg_onorm_w[0], HEADS)[None, :]
    return _fwd_call(x, mod3, nw, wi, lb, sf, q, v, ob, onw, conv_w[0],
                     w_out[0].astype(BF16), final_norm_w[None, :])
```

```python
import jax
import jax.numpy as jnp
from jax import lax
from jax.experimental import pallas as pl
from jax.experimental.pallas import tpu as pltpu

D_MODEL = 1024
HEADS = 8
HEAD_DIM = 128
HG_WIDTH = HEADS * HEAD_DIM
CONV_WIDTH = 1024
GRID_W = 64
EPS = 1e-6

CHUNK = 128
BLK = 16
NBLK = CHUNK // BLK
LEVELS = NBLK.bit_length() - 1
N_OPS = 4 + LEVELS
TILE_T = 512
COL_TILE = 256
EXP_CLAMP = 100.0
VMEM_LIMIT = 56 * 1024 * 1024
ROW_PAD = 4

COL_Q, COL_V, COL_ZF, COL_ZB, COL_GATE, COL_CONV = 0, 1, 2, 3, 4, 5

F32 = jnp.float32
BF16 = jnp.bfloat16

_NT = (((1,), (1,)), ((), ()))
_TN = (((0,), (0,)), ((), ()))


def _w_in_cols(group):
    return pl.BlockSpec((D_MODEL, HG_WIDTH), lambda *_: (0, group), pipeline_mode=pl.Buffered(1))


def _pad_rows(a):
    return jnp.pad(a, ((0, ROW_PAD - a.shape[0]), (0, 0)))


def _sigmoid(x):
    return 1.0 / (1.0 + jnp.exp(-x))


def _rms_rows(xf):
    return lax.rsqrt(jnp.mean(xf * xf, axis=-1, keepdims=True) + EPS)


def _tri(n, reverse):
    t = lax.broadcasted_iota(jnp.int32, (n, n), 0)
    r = lax.broadcasted_iota(jnp.int32, (n, n), 1)
    return (r >= t) if reverse else (r <= t)


def _cumsum_time(g, tri2):
    g_hi = g.astype(BF16)
    g_lo = (g - g_hi.astype(F32)).astype(BF16)
    return jnp.dot(tri2, jnp.concatenate([g_hi, g_lo], axis=0), preferred_element_type=F32)


def _gates(z, lb):
    f = lb + (1.0 - lb) * _sigmoid(z)
    return jnp.log2(f), 1.0 - f


def _mod_kernel(cc_ref, w_ref, b_ref, lg_ref, mod_ref, lb_ref):
    cc = cc_ref[...]
    s = cc * _sigmoid(cc)
    mod_ref[...] = jnp.dot(s, w_ref[...], precision=lax.Precision.HIGHEST,
                           preferred_element_type=F32) + b_ref[...]
    n_rows = lg_ref.shape[0] // 2
    for d in range(2):
        lg = lg_ref[d * n_rows:(d + 1) * n_rows, :]
        e = jnp.exp(lg - jnp.max(lg, axis=0, keepdims=True))
        lb_ref[d:d + 1, :] = e[0:1, :] / jnp.sum(e, axis=0, keepdims=True)
    lb_ref[2:, :] = jnp.zeros((ROW_PAD - 2, lb_ref.shape[1]), F32)


def _mod_call(cc, w_ada, b_ada, lb_logits):
    n_out = w_ada.shape[1]
    bn = 768
    return pl.pallas_call(
        _mod_kernel,
        grid=(n_out // bn,),
        in_specs=[
            pl.BlockSpec((8, D_MODEL), lambda j: (0, 0)),
            pl.BlockSpec((D_MODEL, bn), lambda j: (0, j)),
            pl.BlockSpec((1, bn), lambda j: (0, j)),
            pl.BlockSpec(lb_logits.shape, lambda j: (0, 0)),
        ],
        out_specs=[
            pl.BlockSpec((8, bn), lambda j: (0, j)),
            pl.BlockSpec((ROW_PAD, HG_WIDTH), lambda j: (0, 0)),
        ],
        out_shape=[
            jax.ShapeDtypeStruct((8, n_out), F32),
            jax.ShapeDtypeStruct((ROW_PAD, HG_WIDTH), F32),
        ],
        compiler_params=pltpu.CompilerParams(dimension_semantics=("arbitrary",)),
        name="adaln_mod",
    )(cc, w_ada, b_ada, lb_logits)


def _modulated_norm(x, nw, mod_ref):
    shift = mod_ref[0, 0:1, :]
    scale = mod_ref[0, 1:2, :]
    return ((x * _rms_rows(x)) * (nw * (1.0 + scale)) + shift).astype(BF16)


def _ctx_kernel(x_ref, mod_ref, nw_ref, wv_ref, wzf_ref, wzb_ref, lb_ref, sf_ref, sb_ref):
    n = x_ref.shape[1]
    h = _modulated_norm(x_ref[0], nw_ref[0:1, :], mod_ref)
    v = jnp.dot(h, wv_ref[...], preferred_element_type=F32).astype(BF16)
    for d, wz_ref, out_ref in ((0, wzf_ref, sf_ref), (1, wzb_ref, sb_ref)):
        z = jnp.dot(h, wz_ref[...], preferred_element_type=F32)
        g, kk = _gates(z, lb_ref[d:d + 1, :])
        tri = _tri(n, reverse=bool(d)).astype(BF16)
        bc = _cumsum_time(g, jnp.concatenate([tri, tri], axis=1))
        tot = bc[0:1, :] if d else bc[n - 1:n, :]
        ks = (kk * jnp.exp2(tot - bc)).astype(BF16)
        for hd in range(HEADS):
            sl = slice(hd * HEAD_DIM, (hd + 1) * HEAD_DIM)
            out_ref[0, hd] = lax.dot_general(ks[:, sl], v[:, sl], _TN, preferred_element_type=F32)


def _ctx_call(ctx, mod3, nw, wi, lb):
    bsz, n, _ = ctx.shape
    st_shape = jax.ShapeDtypeStruct((bsz, HEADS, HEAD_DIM, HEAD_DIM), F32)
    st_spec = pl.BlockSpec((1, HEADS, HEAD_DIM, HEAD_DIM), lambda b: (b, 0, 0, 0))
    return pl.pallas_call(
        _ctx_kernel,
        grid=(bsz,),
        in_specs=[
            pl.BlockSpec((1, n, D_MODEL), lambda b: (b, 0, 0)),
            pl.BlockSpec((1, 3, D_MODEL), lambda b: (bsz, 0, 0)),
            pl.BlockSpec((ROW_PAD, D_MODEL), lambda b: (0, 0)),
            _w_in_cols(COL_V), _w_in_cols(COL_ZF), _w_in_cols(COL_ZB),
            pl.BlockSpec((ROW_PAD, HG_WIDTH), lambda b: (0, 0)),
        ],
        out_specs=[st_spec, st_spec],
        out_shape=[st_shape, st_shape],
        compiler_params=pltpu.CompilerParams(dimension_semantics=("arbitrary",),
                                             vmem_limit_bytes=VMEM_LIMIT),
        name="ctx_states",
    )(ctx, mod3, nw, wi, wi, wi, lb)


def _pivot_row(i, level, reverse):
    half = NBLK >> (level + 1)
    boundary = ((i // (2 * half)) * 2 + 1) * half * BLK
    return boundary if reverse else boundary - 1


def _q_side(i, level, reverse):
    later = (i // (NBLK >> (level + 1))) % 2 == 1
    return later != reverse


def _tiles(r0, n_rows, row_tile, width, col_tile):
    return [(slice(r0 + r, r0 + r + row_tile), slice(c, c + col_tile))
            for r in range(0, n_rows, row_tile) for c in range(0, width, col_tile)]


def _scan_gates(c, z_ref, lb, bc_s, g2_s, tri2):
    for rows, cols in _tiles(c * CHUNK, CHUNK, BLK, HG_WIDTH, COL_TILE):
        g, kk = _gates(z_ref[rows, cols], lb[:, cols])
        g_hi = g.astype(BF16)
        g2_s[0, rows, cols] = g_hi
        g2_s[1, rows, cols] = (g - g_hi.astype(F32)).astype(BF16)
        z_ref[rows, cols] = kk
    rows = slice(c * CHUNK, (c + 1) * CHUNK)
    g2 = jnp.concatenate([g2_s[0, rows, :], g2_s[1, rows, :]], axis=0)
    bc_s[rows, :] = jnp.dot(tri2, g2, preferred_element_type=F32)


def _scan_prep(c, k_ref, q_ref, ops_s, bc_s, reverse):
    r0 = c * CHUNK
    blk = lax.broadcasted_iota(jnp.int32, (NBLK, 1), 0)

    def per_block(row_of_block):
        out = None
        for i in reversed(range(NBLK)):
            if i + 1 < NBLK and row_of_block[i] == row_of_block[i + 1]:
                continue
            row = bc_s[r0 + row_of_block[i]:r0 + row_of_block[i] + 1, :]
            out = row if out is None else jnp.where(blk <= i, row, out)
        return jnp.broadcast_to(out, (NBLK, out.shape[1]))

    mid_off = BLK // 2 if reverse else BLK // 2 - 1
    mid = per_block([i * BLK + mid_off for i in range(NBLK)])
    tot_row = r0 if reverse else r0 + CHUNK - 1
    tot = bc_s[tot_row:tot_row + 1, :]
    scale = [jnp.exp2(mid), jnp.exp2(tot - mid)]
    for level in range(LEVELS):
        piv = per_block([_pivot_row(i, level, reverse) for i in range(NBLK)])
        scale.append(jnp.exp2(-jnp.abs(mid - piv)))
    for rows, cols in _tiles(r0, CHUNK, BLK, HG_WIDTH, COL_TILE):
        i = (rows.start - r0) // BLK
        d = bc_s[rows, cols] - mid[i:i + 1, cols]
        qd = q_ref[rows, cols].astype(F32) * jnp.exp2(jnp.minimum(d, EXP_CLAMP))
        kd = k_ref[rows, cols] * jnp.exp2(jnp.minimum(-d, EXP_CLAMP))
        ops_s[0, rows, cols] = qd.astype(BF16)
        ops_s[1, rows, cols] = kd.astype(BF16)
        ops_s[2, rows, cols] = (qd * scale[0][i:i + 1, cols]).astype(BF16)
        ops_s[3, rows, cols] = (kd * scale[1][i:i + 1, cols]).astype(BF16)
        for level in range(LEVELS):
            side = qd if _q_side(i, level, reverse) else kd
            ops_s[4 + level, rows, cols] = (side * scale[2 + level][i:i + 1, cols]).astype(BF16)
    return jnp.exp2(tot)


def _pair_masks(reverse):
    t = lax.broadcasted_iota(jnp.int32, (CHUNK, CHUNK), 0)
    s = lax.broadcasted_iota(jnp.int32, (CHUNK, CHUNK), 1)
    early, late = (t, s) if reverse else (s, t)
    masks = [((t // BLK) == (s // BLK)) & (early <= late)]
    for level in range(LEVELS):
        half = CHUNK >> (level + 1)
        masks.append(((late // half) == (early // half) + 1) & ((early // half) % 2 == 0))
    return masks


def _scan_mm(c, dec, v_ref, st_ref, ops_s, masks, emit):
    rows = slice(c * CHUNK, (c + 1) * CHUNK)
    attn = []
    for hd in range(HEADS):
        sl = slice(hd * HEAD_DIM, (hd + 1) * HEAD_DIM)
        pair = lambda m, n: lax.dot_general(ops_s[m, rows, sl], ops_s[n, rows, sl], _NT,
                                            preferred_element_type=F32)
        a = jnp.where(masks[0], pair(0, 1), 0.0)
        for level in range(LEVELS):
            a = jnp.where(masks[1 + level], pair(4 + level, 4 + level), a)
        attn.append(a.astype(BF16))
    for hd in range(HEADS):
        sl = slice(hd * HEAD_DIM, (hd + 1) * HEAD_DIM)
        st = st_ref[hd]
        v = v_ref[rows, sl]
        lhs = jnp.concatenate([ops_s[2, rows, sl], attn[hd]], axis=1)
        rhs = jnp.concatenate([st.astype(BF16), v], axis=0)
        emit(rows, sl, jnp.dot(lhs, rhs, preferred_element_type=F32))
        kv = lax.dot_general(ops_s[3, rows, sl], v, _TN, preferred_element_type=F32)
        dec_col = jnp.transpose(jnp.broadcast_to(dec[:, sl], (HEAD_DIM, HEAD_DIM)))
        st_ref[hd] = st * dec_col + kv


def _scan_tile(z_ref, q_ref, v_ref, lb, st_ref, ops_s, bc_s, g2_s, emit, reverse,
               after_gates=None, before_matmuls=None, after_chunk=None):
    n_chunks = z_ref.shape[0] // CHUNK
    order = range(n_chunks - 1, -1, -1) if reverse else range(n_chunks)
    tri_b = _tri(CHUNK, reverse).astype(BF16)
    tri2 = jnp.concatenate([tri_b, tri_b], axis=1)
    masks = _pair_masks(reverse)
    for c in order:
        _scan_gates(c, z_ref, lb, bc_s, g2_s, tri2)
    if after_gates is not None:
        after_gates()
    dec = {c: _scan_prep(c, z_ref, q_ref, ops_s, bc_s, reverse) for c in order}
    if before_matmuls is not None:
        before_matmuls()
    for c in order:
        _scan_mm(c, dec[c], v_ref, st_ref, ops_s, masks, emit)
        if after_chunk is not None:
            after_chunk(slice(c * CHUNK, (c + 1) * CHUNK))


def _init_state(s0_ref, st_s):
    @pl.when(pl.program_id(1) == 0)
    def _():
        st_s[...] = s0_ref[0]


_SCAN_SCRATCH = [
    pltpu.VMEM((HEADS, HEAD_DIM, HEAD_DIM), F32),
    pltpu.VMEM((N_OPS, TILE_T, HG_WIDTH), BF16),
    pltpu.VMEM((TILE_T, HG_WIDTH), F32),
    pltpu.VMEM((TILE_T, HG_WIDTH), F32),
    pltpu.VMEM((2, TILE_T, HG_WIDTH), BF16),
]


def _bwd_kernel(x_ref, mod_ref, nw_ref, wq_ref, wv_ref, wz_ref, lb_ref, s0_ref,
                q_ref, v_ref, ob_ref, st_s, ops_s, bc_s, z_s, g2_s):
    _init_state(s0_ref, st_s)
    h = _modulated_norm(x_ref[0], nw_ref[0:1, :], mod_ref)
    z_s[...] = jnp.dot(h, wz_ref[...], preferred_element_type=F32)
    q_ref[0] = jnp.dot(h, wq_ref[...], preferred_element_type=F32).astype(BF16)

    def value_proj():
        v_ref[0] = jnp.dot(h, wv_ref[...], preferred_element_type=F32).astype(BF16)

    def emit(rows, sl, o):
        ob_ref[0, rows, sl] = o.astype(BF16)

    _scan_tile(z_s, q_ref.at[0], v_ref.at[0], lb_ref[1:2, :], st_s, ops_s, bc_s, g2_s, emit,
               reverse=True,
               after_gates=value_proj)


def _bwd_call(x, mod3, nw, wi, lb, sb):
    bsz, seq, _ = x.shape
    nt = seq // TILE_T
    tile = lambda b, t: (b, nt - 1 - t, 0)
    act = jax.ShapeDtypeStruct((bsz, seq, HG_WIDTH), BF16)
    act_spec = pl.BlockSpec((1, TILE_T, HG_WIDTH), tile)
    return pl.pallas_call(
        _bwd_kernel,
        grid=(bsz, nt),
        in_specs=[
            pl.BlockSpec((1, TILE_T, D_MODEL), tile),
            pl.BlockSpec((1, 3, D_MODEL), lambda b, t: (b, 0, 0)),
            pl.BlockSpec((ROW_PAD, D_MODEL), lambda b, t: (0, 0)),
            _w_in_cols(COL_Q), _w_in_cols(COL_V), _w_in_cols(COL_ZB),
            pl.BlockSpec((ROW_PAD, HG_WIDTH), lambda b, t: (0, 0)),
            pl.BlockSpec((1, HEADS, HEAD_DIM, HEAD_DIM), lambda b, t: (b, 0, 0, 0)),
        ],
        out_specs=[act_spec, act_spec, act_spec],
        out_shape=[act, act, act],
        scratch_shapes=_SCAN_SCRATCH,
        compiler_params=pltpu.CompilerParams(dimension_semantics=("arbitrary", "arbitrary"),
                                             vmem_limit_bytes=VMEM_LIMIT),
        name="bwd_scan",
    )(x, mod3, nw, wi, wi, wi, lb, sb)


def _fwd_kernel(x_ref, mod_ref, nw_ref, wz_ref, wg_ref, wc0_ref, wc1_ref, wc2_ref, wc3_ref,
                lb_ref, s0_ref, q_ref, v_ref, ob_ref, onw_ref, cw_ref, wo_ref, fnw_ref, out_ref,
                st_s, ops_s, bc_s, z_s, g2_s, mix_s, cv_s):
    _init_state(s0_ref, st_s)
    g_s = z_s
    o_s = bc_s
    h = _modulated_norm(x_ref[0], nw_ref[0:1, :], mod_ref)

    z_s[...] = jnp.dot(h, wz_ref[...], preferred_element_type=F32)

    def conv_proj(j, wc_ref):
        cv_s[:, j * CONV_WIDTH:(j + 1) * CONV_WIDTH] = jnp.dot(h, wc_ref[...],
                                                               preferred_element_type=F32)

    conv_proj(0, wc0_ref)

    def conv_proj_rest():
        for j, wc_ref in enumerate((wc1_ref, wc2_ref, wc3_ref), start=1):
            conv_proj(j, wc_ref)

    def conv_branch():
        g_s[...] = jnp.dot(h, wg_ref[...], preferred_element_type=F32)
        n_tok = cv_s.shape[0]
        u = cv_s[:, 2 * CONV_WIDTH:3 * CONV_WIDTH] * cv_s[:, 0:CONV_WIDTH]
        col = lax.broadcasted_iota(jnp.int32, (n_tok, 1), 0) % GRID_W
        u_prev = jnp.where(col == 0, 0.0, pltpu.roll(u, 1, axis=0))
        u_next = jnp.where(col == GRID_W - 1, 0.0, pltpu.roll(u, n_tok - 1, axis=0))
        conv = cw_ref[0:1, :] * u_prev + cw_ref[1:2, :] * u + cw_ref[2:3, :] * u_next
        cg = cv_s[:, 3 * CONV_WIDTH:4 * CONV_WIDTH]
        y_cv = cv_s[:, CONV_WIDTH:2 * CONV_WIDTH] * conv * (cg * _sigmoid(cg))
        mix_s[:, HG_WIDTH:] = y_cv.astype(BF16)

    def emit(rows, sl, o):
        o_s[rows, sl] = o + ob_ref[0, rows, sl].astype(F32)

    def gated_readout(rows):
        onw = onw_ref[0:1, :]
        for hd in range(HEADS):
            sl = slice(hd * HEAD_DIM, (hd + 1) * HEAD_DIM)
            o = o_s[rows, sl]
            g = g_s[rows, sl]
            y = (o * _rms_rows(o) * onw[:, sl]) * (g * _sigmoid(g))
            mix_s[rows, sl] = y.astype(BF16)

    _scan_tile(z_s, q_ref.at[0], v_ref.at[0], lb_ref[0:1, :], st_s, ops_s, bc_s, g2_s, emit,
               reverse=False,
               after_gates=conv_proj_rest, before_matmuls=conv_branch, after_chunk=gated_readout)

    half = x_ref.shape[1] // 2
    for rows in (slice(0, half), slice(half, 2 * half)):
        y = jnp.dot(mix_s[rows, :], wo_ref[...], preferred_element_type=F32)
        r = x_ref[0, rows, :] + mod_ref[0, 2:3, :] * y
        out_ref[0, rows, :] = (r * _rms_rows(r)) * fnw_ref[0:1, :]


def _fwd_call(x, mod3, nw, wi, lb, sf, q, v, ob, onw, conv_w, w_o, fnw):
    bsz, seq, _ = x.shape
    nt = seq // TILE_T
    tile = lambda b, t: (b, t, 0)
    const2 = lambda b, t: (0, 0)
    act_spec = pl.BlockSpec((1, TILE_T, HG_WIDTH), tile)
    return pl.pallas_call(
        _fwd_kernel,
        grid=(bsz, nt),
        in_specs=[
            pl.BlockSpec((1, TILE_T, D_MODEL), tile),
            pl.BlockSpec((1, 3, D_MODEL), lambda b, t: (b, 0, 0)),
            pl.BlockSpec((ROW_PAD, D_MODEL), const2),
            _w_in_cols(COL_ZF), _w_in_cols(COL_GATE),
            _w_in_cols(COL_CONV), _w_in_cols(COL_CONV + 1), _w_in_cols(COL_CONV + 2),
            _w_in_cols(COL_CONV + 3),
            pl.BlockSpec((ROW_PAD, HG_WIDTH), const2),
            pl.BlockSpec((1, HEADS, HEAD_DIM, HEAD_DIM), lambda b, t: (b, 0, 0, 0)),
            act_spec, act_spec, act_spec,
            pl.BlockSpec((ROW_PAD, HG_WIDTH), const2),
            pl.BlockSpec(conv_w.shape, const2),
            pl.BlockSpec(w_o.shape, const2, pipeline_mode=pl.Buffered(1)),
            pl.BlockSpec((ROW_PAD, D_MODEL), const2),
        ],
        out_specs=pl.BlockSpec((1, TILE_T, D_MODEL), tile),
        out_shape=jax.ShapeDtypeStruct((bsz, seq, D_MODEL), F32),
        scratch_shapes=_SCAN_SCRATCH + [
            pltpu.VMEM((TILE_T, HG_WIDTH + CONV_WIDTH), BF16),
            pltpu.VMEM((TILE_T, 4 * CONV_WIDTH), F32),
        ],
        compiler_params=pltpu.CompilerParams(dimension_semantics=("arbitrary", "arbitrary"),
                                             vmem_limit_bytes=VMEM_LIMIT),
        name="fwd_scan_out",
    )(x, mod3, nw, wi, wi, wi, wi, wi, wi, lb, sf, q, v, ob, onw, conv_w, w_o, fnw)


def kernel(x, c, ctx, c_ctx, norm_w, w_ada, b_ada, w_in, hg_lb_logits, hg_onorm_w, conv_w,
           w_out, final_norm_w):
    assert w_in.shape[0] == 1, "single-layer block"
    bsz = x.shape[0]
    cc = jnp.concatenate([c, c_ctx[None, :], jnp.zeros((8 - bsz - 1, D_MODEL), F32)], axis=0)
    mod, lb = _mod_call(cc, w_ada[0], b_ada[0][None, :], hg_lb_logits.reshape(-1, HG_WIDTH))
    mod3 = mod.reshape(8, 3, D_MODEL)

    wi = w_in[0].astype(BF16)
    nw = _pad_rows(norm_w[0][None, :])

    sf, sb = _ctx_call(ctx, mod3, nw, wi, lb)
    q, v, ob = _bwd_call(x, mod3, nw, wi, lb, sb)
    onw = _pad_rows(jnp.tile(hg_onorm_w[0], HEADS)[None, :])
    return _fwd_call(x, mod3, nw, wi, lb, sf, q, v, ob, onw, conv_w[0],
                     w_out[0].astype(BF16), _pad_rows(final_norm_w[None, :]))
```

```python
import jax
import jax.numpy as jnp
from jax import lax
from jax.experimental import pallas as pl
from jax.experimental.pallas import tpu as pltpu

D_MODEL = 1024
HEADS = 8
HEAD_DIM = 128
HG_WIDTH = HEADS * HEAD_DIM
CONV_WIDTH = 1024
GRID_W = 64
EPS = 1e-6

CHUNK = 128
BLK = 16
NBLK = CHUNK // BLK
LEVELS = NBLK.bit_length() - 1
N_OPS = 4 + LEVELS
_KEY_OPS = (1,) + tuple(range(4, 4 + LEVELS))
TILE_T = 512
COL_TILE = 256
EXP_CLAMP = 100.0
VMEM_LIMIT = 60 * 1024 * 1024
ROW_PAD = 4

COL_Q, COL_V, COL_ZF, COL_ZB, COL_GATE, COL_CONV = 0, 1, 2, 3, 4, 5

F32 = jnp.float32
BF16 = jnp.bfloat16

_NT = (((1,), (1,)), ((), ()))
_TN = (((0,), (0,)), ((), ()))


def _w_in_cols(group):
    return pl.BlockSpec((D_MODEL, HG_WIDTH), lambda *_: (0, group), pipeline_mode=pl.Buffered(1))


def _pad_rows(a):
    return jnp.pad(a, ((0, ROW_PAD - a.shape[0]), (0, 0)))


def _sigmoid(x):
    return 1.0 / (1.0 + jnp.exp(-x))


def _rms_rows(xf):
    return lax.rsqrt(jnp.mean(xf * xf, axis=-1, keepdims=True) + EPS)


def _tri(n, reverse):
    t = lax.broadcasted_iota(jnp.int32, (n, n), 0)
    r = lax.broadcasted_iota(jnp.int32, (n, n), 1)
    return (r >= t) if reverse else (r <= t)


def _split_bf16(a):
    hi = a.astype(BF16)
    return hi, (a - hi.astype(F32)).astype(BF16)


def _cumsum_time(g, tri2):
    return jnp.dot(tri2, jnp.concatenate(_split_bf16(g), axis=0), preferred_element_type=F32)


def _gates(z, lb):
    f = lb + (1.0 - lb) * _sigmoid(z)
    return jnp.log2(f), 1.0 - f


def _mod_kernel(cc_ref, w_ref, b_ref, lg_ref, mod_ref, lb_ref):
    cc = cc_ref[...]
    s = cc * _sigmoid(cc)
    s_hi, s_lo = _split_bf16(s)
    w_hi, w_lo = _split_bf16(w_ref[...])
    dot = lambda a, b: jnp.dot(a, b, preferred_element_type=F32)
    mod_ref[...] = dot(s_hi, w_hi) + (dot(s_hi, w_lo) + dot(s_lo, w_hi)) + b_ref[...]
    n_rows = lg_ref.shape[0] // 2
    for d in range(2):
        lg = lg_ref[d * n_rows:(d + 1) * n_rows, :]
        e = jnp.exp(lg - jnp.max(lg, axis=0, keepdims=True))
        lb_ref[d:d + 1, :] = e[0:1, :] / jnp.sum(e, axis=0, keepdims=True)
    lb_ref[2:, :] = jnp.zeros((ROW_PAD - 2, lb_ref.shape[1]), F32)


def _mod_call(cc, w_ada, b_ada, lb_logits):
    n_out = w_ada.shape[1]
    bn = 768
    return pl.pallas_call(
        _mod_kernel,
        grid=(n_out // bn,),
        in_specs=[
            pl.BlockSpec((8, D_MODEL), lambda j: (0, 0)),
            pl.BlockSpec((D_MODEL, bn), lambda j: (0, j)),
            pl.BlockSpec((1, bn), lambda j: (0, j)),
            pl.BlockSpec(lb_logits.shape, lambda j: (0, 0)),
        ],
        out_specs=[
            pl.BlockSpec((8, bn), lambda j: (0, j)),
            pl.BlockSpec((ROW_PAD, HG_WIDTH), lambda j: (0, 0)),
        ],
        out_shape=[
            jax.ShapeDtypeStruct((8, n_out), F32),
            jax.ShapeDtypeStruct((ROW_PAD, HG_WIDTH), F32),
        ],
        compiler_params=pltpu.CompilerParams(dimension_semantics=("arbitrary",)),
        name="adaln_mod",
    )(cc, w_ada, b_ada, lb_logits)


def _modulated_norm(x, nw, mod_ref):
    shift = mod_ref[0, 0:1, :]
    scale = mod_ref[0, 1:2, :]
    return ((x * _rms_rows(x)) * (nw * (1.0 + scale)) + shift).astype(BF16)


def _ctx_kernel(x_ref, mod_ref, nw_ref, wv_ref, wzf_ref, wzb_ref, lb_ref, sf_ref, sb_ref):
    n = x_ref.shape[1]
    h = _modulated_norm(x_ref[0], nw_ref[0:1, :], mod_ref)
    v = jnp.dot(h, wv_ref[...], preferred_element_type=F32).astype(BF16)
    for d, wz_ref, out_ref in ((0, wzf_ref, sf_ref), (1, wzb_ref, sb_ref)):
        z = jnp.dot(h, wz_ref[...], preferred_element_type=F32)
        g, kk = _gates(z, lb_ref[d:d + 1, :])
        tri = _tri(n, reverse=bool(d)).astype(BF16)
        bc = _cumsum_time(g, jnp.concatenate([tri, tri], axis=1))
        tot = bc[0:1, :] if d else bc[n - 1:n, :]
        ks = (kk * jnp.exp2(tot - bc)).astype(BF16)
        for hd in range(HEADS):
            sl = slice(hd * HEAD_DIM, (hd + 1) * HEAD_DIM)
            out_ref[0, hd] = lax.dot_general(ks[:, sl], v[:, sl], _TN, preferred_element_type=F32)


def _ctx_call(ctx, mod3, nw, wi, lb):
    bsz, n, _ = ctx.shape
    st_shape = jax.ShapeDtypeStruct((bsz, HEADS, HEAD_DIM, HEAD_DIM), F32)
    st_spec = pl.BlockSpec((1, HEADS, HEAD_DIM, HEAD_DIM), lambda b: (b, 0, 0, 0))
    return pl.pallas_call(
        _ctx_kernel,
        grid=(bsz,),
        in_specs=[
            pl.BlockSpec((1, n, D_MODEL), lambda b: (b, 0, 0)),
            pl.BlockSpec((1, 3, D_MODEL), lambda b: (bsz, 0, 0)),
            pl.BlockSpec((ROW_PAD, D_MODEL), lambda b: (0, 0)),
            _w_in_cols(COL_V), _w_in_cols(COL_ZF), _w_in_cols(COL_ZB),
            pl.BlockSpec((ROW_PAD, HG_WIDTH), lambda b: (0, 0)),
        ],
        out_specs=[st_spec, st_spec],
        out_shape=[st_shape, st_shape],
        compiler_params=pltpu.CompilerParams(dimension_semantics=("arbitrary",),
                                             vmem_limit_bytes=VMEM_LIMIT),
        name="ctx_states",
    )(ctx, mod3, nw, wi, wi, wi, lb)


def _pivot_row(i, level, reverse):
    half = NBLK >> (level + 1)
    boundary = ((i // (2 * half)) * 2 + 1) * half * BLK
    return boundary if reverse else boundary - 1


def _q_side(i, level, reverse):
    later = (i // (NBLK >> (level + 1))) % 2 == 1
    return later != reverse


def _tiles(r0, n_rows, row_tile, width, col_tile):
    return [(slice(r0 + r, r0 + r + row_tile), slice(c, c + col_tile))
            for r in range(0, n_rows, row_tile) for c in range(0, width, col_tile)]


def _scan_gates(c, z_ref, lb, bc_s, g2_s, tri2):
    for rows, cols in _tiles(c * CHUNK, CHUNK, BLK, HG_WIDTH, COL_TILE):
        g, kk = _gates(z_ref[rows, cols], lb[:, cols])
        g_hi = g.astype(BF16)
        g2_s[0, rows, cols] = g_hi
        g2_s[1, rows, cols] = (g - g_hi.astype(F32)).astype(BF16)
        z_ref[rows, cols] = kk
    rows = slice(c * CHUNK, (c + 1) * CHUNK)
    g2 = jnp.concatenate([g2_s[0, rows, :], g2_s[1, rows, :]], axis=0)
    bc_s[rows, :] = jnp.dot(tri2, g2, preferred_element_type=F32)


def _scan_prep(c, k_ref, q_ref, ops_s, opst_s, bc_s, reverse):
    r0 = c * CHUNK
    blk = lax.broadcasted_iota(jnp.int32, (NBLK, 1), 0)

    def per_block(row_of_block):
        out = None
        for i in reversed(range(NBLK)):
            if i + 1 < NBLK and row_of_block[i] == row_of_block[i + 1]:
                continue
            row = bc_s[r0 + row_of_block[i]:r0 + row_of_block[i] + 1, :]
            out = row if out is None else jnp.where(blk <= i, row, out)
        return jnp.broadcast_to(out, (NBLK, out.shape[1]))

    mid_off = BLK // 2 if reverse else BLK // 2 - 1
    mid = per_block([i * BLK + mid_off for i in range(NBLK)])
    tot_row = r0 if reverse else r0 + CHUNK - 1
    tot = bc_s[tot_row:tot_row + 1, :]
    scale = [jnp.exp2(mid), jnp.exp2(tot - mid)]
    for level in range(LEVELS):
        piv = per_block([_pivot_row(i, level, reverse) for i in range(NBLK)])
        scale.append(jnp.exp2(-jnp.abs(mid - piv)))
    for rows, cols in _tiles(r0, CHUNK, BLK, HG_WIDTH, COL_TILE):
        i = (rows.start - r0) // BLK
        d = bc_s[rows, cols] - mid[i:i + 1, cols]
        qd = q_ref[rows, cols].astype(F32) * jnp.exp2(jnp.minimum(d, EXP_CLAMP))
        kd = k_ref[rows, cols] * jnp.exp2(jnp.minimum(-d, EXP_CLAMP))
        ops_s[0, rows, cols] = qd.astype(BF16)
        ops_s[1, rows, cols] = kd.astype(BF16)
        ops_s[2, rows, cols] = (qd * scale[0][i:i + 1, cols]).astype(BF16)
        ops_s[3, rows, cols] = (kd * scale[1][i:i + 1, cols]).astype(BF16)
        for level in range(LEVELS):
            side = qd if _q_side(i, level, reverse) else kd
            ops_s[4 + level, rows, cols] = (side * scale[2 + level][i:i + 1, cols]).astype(BF16)
    rows = slice(r0, r0 + CHUNK)
    for hd in range(HEADS):
        sl = slice(hd * HEAD_DIM, (hd + 1) * HEAD_DIM)
        for j, n in enumerate(_KEY_OPS):
            opst_s[j, c * HEADS + hd] = ops_s[n, rows, sl].T
    return jnp.exp2(tot)


def _pair_masks(reverse):
    t = lax.broadcasted_iota(jnp.int32, (CHUNK, CHUNK), 0)
    s = lax.broadcasted_iota(jnp.int32, (CHUNK, CHUNK), 1)
    early, late = (t, s) if reverse else (s, t)
    masks = [((t // BLK) == (s // BLK)) & (early <= late)]
    for level in range(LEVELS):
        half = CHUNK >> (level + 1)
        masks.append(((late // half) == (early // half) + 1) & ((early // half) % 2 == 0))
    return masks


def _scan_mm(c, dec, v_ref, st_ref, ops_s, opst_s, masks, emit, reverse):
    r0 = c * CHUNK
    rows = slice(r0, r0 + CHUNK)
    q_blocks = [[i for i in range(NBLK) if _q_side(i, level, reverse)] for level in range(LEVELS)]
    attn = []
    for hd in range(HEADS):
        sl = slice(hd * HEAD_DIM, (hd + 1) * HEAD_DIM)
        same = jnp.dot(ops_s[0, rows, sl], opst_s[0, c * HEADS + hd], preferred_element_type=F32)
        cross = []
        for level in range(LEVELS):
            lhs = jnp.concatenate([ops_s[4 + level, r0 + i * BLK:r0 + (i + 1) * BLK, sl]
                                   for i in q_blocks[level]], axis=0)
            cross.append(jnp.dot(lhs, opst_s[1 + level, c * HEADS + hd],
                                 preferred_element_type=F32))
        blocks = []
        for i in range(NBLK):
            blk = slice(i * BLK, (i + 1) * BLK)
            a = jnp.where(masks[0][blk], same[blk], 0.0)
            for level in range(LEVELS):
                if i in q_blocks[level]:
                    j = q_blocks[level].index(i)
                    a = jnp.where(masks[1 + level][blk], cross[level][j * BLK:(j + 1) * BLK], a)
            blocks.append(a)
        attn.append(jnp.concatenate(blocks, axis=0).astype(BF16))
    for hd in range(HEADS):
        sl = slice(hd * HEAD_DIM, (hd + 1) * HEAD_DIM)
        st = st_ref[hd]
        v = v_ref[rows, sl]
        lhs = jnp.concatenate([ops_s[2, rows, sl], attn[hd]], axis=1)
        rhs = jnp.concatenate([st.astype(BF16), v], axis=0)
        emit(rows, sl, jnp.dot(lhs, rhs, preferred_element_type=F32))
        kv = lax.dot_general(ops_s[3, rows, sl], v, _TN, preferred_element_type=F32)
        dec_col = jnp.transpose(jnp.broadcast_to(dec[:, sl], (HEAD_DIM, HEAD_DIM)))
        st_ref[hd] = st * dec_col + kv


def _scan_tile(z_ref, q_ref, v_ref, lb, st_ref, ops_s, opst_s, bc_s, g2_s, emit, reverse,
               after_gates=None, before_matmuls=None, after_chunk=None):
    n_chunks = z_ref.shape[0] // CHUNK
    order = range(n_chunks - 1, -1, -1) if reverse else range(n_chunks)
    tri_b = _tri(CHUNK, reverse).astype(BF16)
    tri2 = jnp.concatenate([tri_b, tri_b], axis=1)
    masks = _pair_masks(reverse)
    for c in order:
        _scan_gates(c, z_ref, lb, bc_s, g2_s, tri2)
    if after_gates is not None:
        after_gates()
    dec = {c: _scan_prep(c, z_ref, q_ref, ops_s, opst_s, bc_s, reverse) for c in order}
    if before_matmuls is not None:
        before_matmuls()
    for c in order:
        _scan_mm(c, dec[c], v_ref, st_ref, ops_s, opst_s, masks, emit, reverse)
        if after_chunk is not None:
            after_chunk(slice(c * CHUNK, (c + 1) * CHUNK))


def _init_state(s0_ref, st_s):
    @pl.when(pl.program_id(1) == 0)
    def _():
        st_s[...] = s0_ref[0]


_SCAN_SCRATCH = [
    pltpu.VMEM((HEADS, HEAD_DIM, HEAD_DIM), F32),
    pltpu.VMEM((N_OPS, TILE_T, HG_WIDTH), BF16),
    pltpu.VMEM((len(_KEY_OPS), TILE_T // CHUNK * HEADS, HEAD_DIM, CHUNK), BF16),
    pltpu.VMEM((TILE_T, HG_WIDTH), F32),
    pltpu.VMEM((TILE_T, HG_WIDTH), F32),
    pltpu.VMEM((2, TILE_T, HG_WIDTH), BF16),
]


def _bwd_kernel(x_ref, mod_ref, nw_ref, wq_ref, wv_ref, wz_ref, lb_ref, s0_ref,
                q_ref, v_ref, ob_ref, st_s, ops_s, opst_s, bc_s, z_s, g2_s):
    _init_state(s0_ref, st_s)
    h = _modulated_norm(x_ref[0], nw_ref[0:1, :], mod_ref)
    z_s[...] = jnp.dot(h, wz_ref[...], preferred_element_type=F32)
    q_ref[0] = jnp.dot(h, wq_ref[...], preferred_element_type=F32).astype(BF16)

    def value_proj():
        v_ref[0] = jnp.dot(h, wv_ref[...], preferred_element_type=F32).astype(BF16)

    def emit(rows, sl, o):
        ob_ref[0, rows, sl] = o.astype(BF16)

    _scan_tile(z_s, q_ref.at[0], v_ref.at[0], lb_ref[1:2, :], st_s, ops_s, opst_s, bc_s, g2_s, emit,
               reverse=True,
               after_gates=value_proj)


def _bwd_call(x, mod3, nw, wi, lb, sb):
    bsz, seq, _ = x.shape
    nt = seq // TILE_T
    tile = lambda b, t: (b, nt - 1 - t, 0)
    act = jax.ShapeDtypeStruct((bsz, seq, HG_WIDTH), BF16)
    act_spec = pl.BlockSpec((1, TILE_T, HG_WIDTH), tile)
    return pl.pallas_call(
        _bwd_kernel,
        grid=(bsz, nt),
        in_specs=[
            pl.BlockSpec((1, TILE_T, D_MODEL), tile),
            pl.BlockSpec((1, 3, D_MODEL), lambda b, t: (b, 0, 0)),
            pl.BlockSpec((ROW_PAD, D_MODEL), lambda b, t: (0, 0)),
            _w_in_cols(COL_Q), _w_in_cols(COL_V), _w_in_cols(COL_ZB),
            pl.BlockSpec((ROW_PAD, HG_WIDTH), lambda b, t: (0, 0)),
            pl.BlockSpec((1, HEADS, HEAD_DIM, HEAD_DIM), lambda b, t: (b, 0, 0, 0)),
        ],
        out_specs=[act_spec, act_spec, act_spec],
        out_shape=[act, act, act],
        scratch_shapes=_SCAN_SCRATCH,
        compiler_params=pltpu.CompilerParams(dimension_semantics=("arbitrary", "arbitrary"),
                                             vmem_limit_bytes=VMEM_LIMIT),
        name="bwd_scan",
    )(x, mod3, nw, wi, wi, wi, lb, sb)


def _fwd_kernel(x_ref, mod_ref, nw_ref, wz_ref, wg_ref, wc0_ref, wc1_ref, wc2_ref, wc3_ref,
                lb_ref, s0_ref, q_ref, v_ref, ob_ref, onw_ref, cw_ref, wo_ref, fnw_ref, out_ref,
                st_s, ops_s, opst_s, bc_s, z_s, g2_s, mix_s, cv_s):
    _init_state(s0_ref, st_s)
    g_s = z_s
    o_s = bc_s
    h = _modulated_norm(x_ref[0], nw_ref[0:1, :], mod_ref)

    z_s[...] = jnp.dot(h, wz_ref[...], preferred_element_type=F32)

    def conv_proj(j, wc_ref):
        cv_s[:, j * CONV_WIDTH:(j + 1) * CONV_WIDTH] = jnp.dot(h, wc_ref[...],
                                                               preferred_element_type=F32)

    conv_proj(0, wc0_ref)

    def conv_proj_rest():
        for j, wc_ref in enumerate((wc1_ref, wc2_ref, wc3_ref), start=1):
            conv_proj(j, wc_ref)

    def conv_branch():
        g_s[...] = jnp.dot(h, wg_ref[...], preferred_element_type=F32)
        n_tok = cv_s.shape[0]
        u = cv_s[:, 2 * CONV_WIDTH:3 * CONV_WIDTH] * cv_s[:, 0:CONV_WIDTH]
        col = lax.broadcasted_iota(jnp.int32, (n_tok, 1), 0) % GRID_W
        u_prev = jnp.where(col == 0, 0.0, pltpu.roll(u, 1, axis=0))
        u_next = jnp.where(col == GRID_W - 1, 0.0, pltpu.roll(u, n_tok - 1, axis=0))
        conv = cw_ref[0:1, :] * u_prev + cw_ref[1:2, :] * u + cw_ref[2:3, :] * u_next
        cg = cv_s[:, 3 * CONV_WIDTH:4 * CONV_WIDTH]
        y_cv = cv_s[:, CONV_WIDTH:2 * CONV_WIDTH] * conv * (cg * _sigmoid(cg))
        mix_s[:, HG_WIDTH:] = y_cv.astype(BF16)

    def emit(rows, sl, o):
        o_s[rows, sl] = o + ob_ref[0, rows, sl].astype(F32)

    def gated_readout(rows):
        onw = onw_ref[0:1, :]
        for hd in range(HEADS):
            sl = slice(hd * HEAD_DIM, (hd + 1) * HEAD_DIM)
            o = o_s[rows, sl]
            g = g_s[rows, sl]
            y = (o * _rms_rows(o) * onw[:, sl]) * (g * _sigmoid(g))
            mix_s[rows, sl] = y.astype(BF16)

    _scan_tile(z_s, q_ref.at[0], v_ref.at[0], lb_ref[0:1, :], st_s, ops_s, opst_s, bc_s, g2_s, emit,
               reverse=False,
               after_gates=conv_proj_rest, before_matmuls=conv_branch, after_chunk=gated_readout)

    half = x_ref.shape[1] // 2
    for rows in (slice(0, half), slice(half, 2 * half)):
        y = jnp.dot(mix_s[rows, :], wo_ref[...], preferred_element_type=F32)
        r = x_ref[0, rows, :] + mod_ref[0, 2:3, :] * y
        out_ref[0, rows, :] = (r * _rms_rows(r)) * fnw_ref[0:1, :]


def _fwd_call(x, mod3, nw, wi, lb, sf, q, v, ob, onw, conv_w, w_o, fnw):
    bsz, seq, _ = x.shape
    nt = seq // TILE_T
    tile = lambda b, t: (b, t, 0)
    const2 = lambda b, t: (0, 0)
    act_spec = pl.BlockSpec((1, TILE_T, HG_WIDTH), tile)
    return pl.pallas_call(
        _fwd_kernel,
        grid=(bsz, nt),
        in_specs=[
            pl.BlockSpec((1, TILE_T, D_MODEL), tile),
            pl.BlockSpec((1, 3, D_MODEL), lambda b, t: (b, 0, 0)),
            pl.BlockSpec((ROW_PAD, D_MODEL), const2),
            _w_in_cols(COL_ZF), _w_in_cols(COL_GATE),
            _w_in_cols(COL_CONV), _w_in_cols(COL_CONV + 1), _w_in_cols(COL_CONV + 2),
            _w_in_cols(COL_CONV + 3),
            pl.BlockSpec((ROW_PAD, HG_WIDTH), const2),
            pl.BlockSpec((1, HEADS, HEAD_DIM, HEAD_DIM), lambda b, t: (b, 0, 0, 0)),
            act_spec, act_spec, act_spec,
            pl.BlockSpec((ROW_PAD, HG_WIDTH), const2),
            pl.BlockSpec(conv_w.shape, const2),
            pl.BlockSpec(w_o.shape, const2, pipeline_mode=pl.Buffered(1)),
            pl.BlockSpec((ROW_PAD, D_MODEL), const2),
        ],
        out_specs=pl.BlockSpec((1, TILE_T, D_MODEL), tile),
        out_shape=jax.ShapeDtypeStruct((bsz, seq, D_MODEL), F32),
        scratch_shapes=_SCAN_SCRATCH + [
            pltpu.VMEM((TILE_T, HG_WIDTH + CONV_WIDTH), BF16),
            pltpu.VMEM((TILE_T, 4 * CONV_WIDTH), F32),
        ],
        compiler_params=pltpu.CompilerParams(dimension_semantics=("arbitrary", "arbitrary"),
                                             vmem_limit_bytes=VMEM_LIMIT),
        name="fwd_scan_out",
    )(x, mod3, nw, wi, wi, wi, wi, wi, wi, lb, sf, q, v, ob, onw, conv_w, w_o, fnw)


def kernel(x, c, ctx, c_ctx, norm_w, w_ada, b_ada, w_in, hg_lb_logits, hg_onorm_w, conv_w,
           w_out, final_norm_w):
    assert w_in.shape[0] == 1, "single-layer block"
    bsz = x.shape[0]
    cc = jnp.concatenate([c, c_ctx[None, :], jnp.zeros((8 - bsz - 1, D_MODEL), F32)], axis=0)
    mod, lb = _mod_call(cc, w_ada[0], b_ada[0][None, :], hg_lb_logits.reshape(-1, HG_WIDTH))
    mod3 = mod.reshape(8, 3, D_MODEL)

    wi = w_in[0].astype(BF16)
    nw = _pad_rows(norm_w[0][None, :])

    sf, sb = _ctx_call(ctx, mod3, nw, wi, lb)
    q, v, ob = _bwd_call(x, mod3, nw, wi, lb, sb)
    onw = _pad_rows(jnp.tile(hg_onorm_w[0], HEADS)[None, :])
    return _fwd_call(x, mod3, nw, wi, lb, sf, q, v, ob, onw, conv_w[0],
                     w_out[0].astype(BF16), _pad_rows(final_norm_w[None, :]))
```

```python
import jax
import jax.numpy as jnp
from jax import lax
from jax.experimental import pallas as pl
from jax.experimental.pallas import tpu as pltpu

D_MODEL = 1024
HEADS = 8
HEAD_DIM = 128
HG_WIDTH = HEADS * HEAD_DIM
CONV_WIDTH = 1024
GRID_W = 64
EPS = 1e-6

CHUNK = 128
BLK = 16
NBLK = CHUNK // BLK
LEVELS = NBLK.bit_length() - 1
N_OPS = 4 + LEVELS
_KEY_OPS = (1,) + tuple(range(4, 4 + LEVELS))
TILE_T = 512
COL_TILE = 256
EXP_RANGE = 115.0
LOG2F_FLOOR = 2.0 * EXP_RANGE / (BLK - 1)
VMEM_LIMIT = 60 * 1024 * 1024
ROW_PAD = 4

COL_Q, COL_V, COL_ZF, COL_ZB, COL_GATE, COL_CONV = 0, 1, 2, 3, 4, 5

F32 = jnp.float32
BF16 = jnp.bfloat16

_NT = (((1,), (1,)), ((), ()))
_TN = (((0,), (0,)), ((), ()))


def _w_in_cols(group):
    return pl.BlockSpec((D_MODEL, HG_WIDTH), lambda *_: (0, group), pipeline_mode=pl.Buffered(1))


def _pad_rows(a):
    return jnp.pad(a, ((0, ROW_PAD - a.shape[0]), (0, 0)))


def _sigmoid(x):
    return 1.0 / (1.0 + jnp.exp(-x))


def _rms_rows(xf):
    return lax.rsqrt(jnp.mean(xf * xf, axis=-1, keepdims=True) + EPS)


def _tri(n, reverse):
    t = lax.broadcasted_iota(jnp.int32, (n, n), 0)
    r = lax.broadcasted_iota(jnp.int32, (n, n), 1)
    return (r >= t) if reverse else (r <= t)


def _split_bf16(a):
    hi = a.astype(BF16)
    return hi, (a - hi.astype(F32)).astype(BF16)


def _cumsum_time(g, tri2):
    return jnp.dot(tri2, jnp.concatenate(_split_bf16(g), axis=0), preferred_element_type=F32)


def _gates(z, lb):
    f = lb + (1.0 - lb) * _sigmoid(z)
    return jnp.maximum(jnp.log2(f), -LOG2F_FLOOR), 1.0 - f


def _mod_kernel(cc_ref, w_ref, b_ref, lg_ref, mod_ref, lb_ref):
    cc = cc_ref[...]
    s = cc * _sigmoid(cc)
    s_hi, s_lo = _split_bf16(s)
    w_hi, w_lo = _split_bf16(w_ref[...])
    dot = lambda a, b: jnp.dot(a, b, preferred_element_type=F32)
    mod_ref[...] = dot(s_hi, w_hi) + (dot(s_hi, w_lo) + dot(s_lo, w_hi)) + b_ref[...]
    n_rows = lg_ref.shape[0] // 2
    for d in range(2):
        lg = lg_ref[d * n_rows:(d + 1) * n_rows, :]
        e = jnp.exp(lg - jnp.max(lg, axis=0, keepdims=True))
        lb_ref[d:d + 1, :] = e[0:1, :] / jnp.sum(e, axis=0, keepdims=True)
    lb_ref[2:, :] = jnp.zeros((ROW_PAD - 2, lb_ref.shape[1]), F32)


def _mod_call(cc, w_ada, b_ada, lb_logits):
    n_out = w_ada.shape[1]
    bn = 768
    return pl.pallas_call(
        _mod_kernel,
        grid=(n_out // bn,),
        in_specs=[
            pl.BlockSpec((8, D_MODEL), lambda j: (0, 0)),
            pl.BlockSpec((D_MODEL, bn), lambda j: (0, j)),
            pl.BlockSpec((1, bn), lambda j: (0, j)),
            pl.BlockSpec(lb_logits.shape, lambda j: (0, 0)),
        ],
        out_specs=[
            pl.BlockSpec((8, bn), lambda j: (0, j)),
            pl.BlockSpec((ROW_PAD, HG_WIDTH), lambda j: (0, 0)),
        ],
        out_shape=[
            jax.ShapeDtypeStruct((8, n_out), F32),
            jax.ShapeDtypeStruct((ROW_PAD, HG_WIDTH), F32),
        ],
        compiler_params=pltpu.CompilerParams(dimension_semantics=("arbitrary",)),
        name="adaln_mod",
    )(cc, w_ada, b_ada, lb_logits)


def _modulated_norm(x, nw, mod_ref):
    shift = mod_ref[0, 0:1, :]
    scale = mod_ref[0, 1:2, :]
    return ((x * _rms_rows(x)) * (nw * (1.0 + scale)) + shift).astype(BF16)


def _ctx_kernel(x_ref, mod_ref, nw_ref, wv_ref, wzf_ref, wzb_ref, lb_ref, sf_ref, sb_ref):
    n = x_ref.shape[1]
    h = _modulated_norm(x_ref[0], nw_ref[0:1, :], mod_ref)
    v = jnp.dot(h, wv_ref[...], preferred_element_type=F32).astype(BF16)
    for d, wz_ref, out_ref in ((0, wzf_ref, sf_ref), (1, wzb_ref, sb_ref)):
        z = jnp.dot(h, wz_ref[...], preferred_element_type=F32)
        g, kk = _gates(z, lb_ref[d:d + 1, :])
        tri = _tri(n, reverse=bool(d)).astype(BF16)
        bc = _cumsum_time(g, jnp.concatenate([tri, tri], axis=1))
        tot = bc[0:1, :] if d else bc[n - 1:n, :]
        ks = (kk * jnp.exp2(tot - bc)).astype(BF16)
        for hd in range(HEADS):
            sl = slice(hd * HEAD_DIM, (hd + 1) * HEAD_DIM)
            out_ref[0, hd] = lax.dot_general(ks[:, sl], v[:, sl], _TN, preferred_element_type=F32)


def _ctx_call(ctx, mod3, nw, wi, lb):
    bsz, n, _ = ctx.shape
    st_shape = jax.ShapeDtypeStruct((bsz, HEADS, HEAD_DIM, HEAD_DIM), F32)
    st_spec = pl.BlockSpec((1, HEADS, HEAD_DIM, HEAD_DIM), lambda b: (b, 0, 0, 0))
    return pl.pallas_call(
        _ctx_kernel,
        grid=(bsz,),
        in_specs=[
            pl.BlockSpec((1, n, D_MODEL), lambda b: (b, 0, 0)),
            pl.BlockSpec((1, 3, D_MODEL), lambda b: (bsz, 0, 0)),
            pl.BlockSpec((ROW_PAD, D_MODEL), lambda b: (0, 0)),
            _w_in_cols(COL_V), _w_in_cols(COL_ZF), _w_in_cols(COL_ZB),
            pl.BlockSpec((ROW_PAD, HG_WIDTH), lambda b: (0, 0)),
        ],
        out_specs=[st_spec, st_spec],
        out_shape=[st_shape, st_shape],
        compiler_params=pltpu.CompilerParams(dimension_semantics=("arbitrary",),
                                             vmem_limit_bytes=VMEM_LIMIT),
        name="ctx_states",
    )(ctx, mod3, nw, wi, wi, wi, lb)


def _pivot_row(i, level, reverse):
    half = NBLK >> (level + 1)
    boundary = ((i // (2 * half)) * 2 + 1) * half * BLK
    return boundary if reverse else boundary - 1


def _q_side(i, level, reverse):
    later = (i // (NBLK >> (level + 1))) % 2 == 1
    return later != reverse


def _tiles(r0, n_rows, row_tile, width, col_tile):
    return [(slice(r0 + r, r0 + r + row_tile), slice(c, c + col_tile))
            for r in range(0, n_rows, row_tile) for c in range(0, width, col_tile)]


def _scan_gates(c, z_ref, lb, bc_s, g2_s, tri2):
    for rows, cols in _tiles(c * CHUNK, CHUNK, BLK, HG_WIDTH, COL_TILE):
        g, kk = _gates(z_ref[rows, cols], lb[:, cols])
        g_hi = g.astype(BF16)
        g2_s[0, rows, cols] = g_hi
        g2_s[1, rows, cols] = (g - g_hi.astype(F32)).astype(BF16)
        z_ref[rows, cols] = kk
    rows = slice(c * CHUNK, (c + 1) * CHUNK)
    g2 = jnp.concatenate([g2_s[0, rows, :], g2_s[1, rows, :]], axis=0)
    bc_s[rows, :] = jnp.dot(tri2, g2, preferred_element_type=F32)


def _scan_prep(c, k_ref, q_ref, ops_s, opst_s, bc_s, reverse):
    r0 = c * CHUNK
    blk = lax.broadcasted_iota(jnp.int32, (NBLK, 1), 0)

    def per_block(row_of_block):
        out = None
        for i in reversed(range(NBLK)):
            if i + 1 < NBLK and row_of_block[i] == row_of_block[i + 1]:
                continue
            row = bc_s[r0 + row_of_block[i]:r0 + row_of_block[i] + 1, :]
            out = row if out is None else jnp.where(blk <= i, row, out)
        return jnp.broadcast_to(out, (NBLK, out.shape[1]))

    mid = 0.5 * (per_block([i * BLK for i in range(NBLK)])
                 + per_block([i * BLK + BLK - 1 for i in range(NBLK)]))
    tot_row = r0 if reverse else r0 + CHUNK - 1
    tot = bc_s[tot_row:tot_row + 1, :]
    scale = [jnp.exp2(mid), jnp.exp2(tot - mid)]
    for level in range(LEVELS):
        piv = per_block([_pivot_row(i, level, reverse) for i in range(NBLK)])
        scale.append(jnp.exp2(-jnp.abs(mid - piv)))
    for rows, cols in _tiles(r0, CHUNK, BLK, HG_WIDTH, COL_TILE):
        i = (rows.start - r0) // BLK
        d = bc_s[rows, cols] - mid[i:i + 1, cols]
        qd = q_ref[rows, cols].astype(F32) * jnp.exp2(d)
        kd = k_ref[rows, cols] * jnp.exp2(-d)
        ops_s[0, rows, cols] = qd.astype(BF16)
        ops_s[1, rows, cols] = kd.astype(BF16)
        ops_s[2, rows, cols] = (qd * scale[0][i:i + 1, cols]).astype(BF16)
        ops_s[3, rows, cols] = (kd * scale[1][i:i + 1, cols]).astype(BF16)
        for level in range(LEVELS):
            side = qd if _q_side(i, level, reverse) else kd
            ops_s[4 + level, rows, cols] = (side * scale[2 + level][i:i + 1, cols]).astype(BF16)
    rows = slice(r0, r0 + CHUNK)
    for hd in range(HEADS):
        sl = slice(hd * HEAD_DIM, (hd + 1) * HEAD_DIM)
        for j, n in enumerate(_KEY_OPS):
            opst_s[j, c * HEADS + hd] = ops_s[n, rows, sl].T
    return jnp.exp2(tot)


def _pair_masks(reverse):
    t = lax.broadcasted_iota(jnp.int32, (CHUNK, CHUNK), 0)
    s = lax.broadcasted_iota(jnp.int32, (CHUNK, CHUNK), 1)
    early, late = (t, s) if reverse else (s, t)
    masks = [((t // BLK) == (s // BLK)) & (early <= late)]
    for level in range(LEVELS):
        half = CHUNK >> (level + 1)
        masks.append(((late // half) == (early // half) + 1) & ((early // half) % 2 == 0))
    return masks


def _scan_mm(c, dec, v_ref, st_ref, ops_s, opst_s, masks, emit, reverse):
    r0 = c * CHUNK
    rows = slice(r0, r0 + CHUNK)
    q_blocks = [[i for i in range(NBLK) if _q_side(i, level, reverse)] for level in range(LEVELS)]
    attn = []
    for hd in range(HEADS):
        sl = slice(hd * HEAD_DIM, (hd + 1) * HEAD_DIM)
        same = jnp.dot(ops_s[0, rows, sl], opst_s[0, c * HEADS + hd], preferred_element_type=F32)
        cross = []
        for level in range(LEVELS):
            lhs = jnp.concatenate([ops_s[4 + level, r0 + i * BLK:r0 + (i + 1) * BLK, sl]
                                   for i in q_blocks[level]], axis=0)
            cross.append(jnp.dot(lhs, opst_s[1 + level, c * HEADS + hd],
                                 preferred_element_type=F32))
        blocks = []
        for i in range(NBLK):
            blk = slice(i * BLK, (i + 1) * BLK)
            a = jnp.where(masks[0][blk], same[blk], 0.0)
            for level in range(LEVELS):
                if i in q_blocks[level]:
                    j = q_blocks[level].index(i)
                    a = jnp.where(masks[1 + level][blk], cross[level][j * BLK:(j + 1) * BLK], a)
            blocks.append(a)
        attn.append(jnp.concatenate(blocks, axis=0).astype(BF16))
    for hd in range(HEADS):
        sl = slice(hd * HEAD_DIM, (hd + 1) * HEAD_DIM)
        st = st_ref[hd]
        v = v_ref[rows, sl]
        lhs = jnp.concatenate([ops_s[2, rows, sl], attn[hd]], axis=1)
        rhs = jnp.concatenate([st.astype(BF16), v], axis=0)
        emit(rows, sl, jnp.dot(lhs, rhs, preferred_element_type=F32))
        kv = lax.dot_general(ops_s[3, rows, sl], v, _TN, preferred_element_type=F32)
        dec_col = jnp.transpose(jnp.broadcast_to(dec[:, sl], (HEAD_DIM, HEAD_DIM)))
        st_ref[hd] = st * dec_col + kv


def _scan_tile(z_ref, q_ref, v_ref, lb, st_ref, ops_s, opst_s, bc_s, g2_s, emit, reverse,
               after_gates=None, before_matmuls=None, after_chunk=None):
    n_chunks = z_ref.shape[0] // CHUNK
    order = range(n_chunks - 1, -1, -1) if reverse else range(n_chunks)
    tri_b = _tri(CHUNK, reverse).astype(BF16)
    tri2 = jnp.concatenate([tri_b, tri_b], axis=1)
    masks = _pair_masks(reverse)
    for c in order:
        _scan_gates(c, z_ref, lb, bc_s, g2_s, tri2)
    if after_gates is not None:
        after_gates()
    dec = {c: _scan_prep(c, z_ref, q_ref, ops_s, opst_s, bc_s, reverse) for c in order}
    if before_matmuls is not None:
        before_matmuls()
    for c in order:
        _scan_mm(c, dec[c], v_ref, st_ref, ops_s, opst_s, masks, emit, reverse)
        if after_chunk is not None:
            after_chunk(slice(c * CHUNK, (c + 1) * CHUNK))


def _init_state(s0_ref, st_s):
    @pl.when(pl.program_id(1) == 0)
    def _():
        st_s[...] = s0_ref[0]


_SCAN_SCRATCH = [
    pltpu.VMEM((HEADS, HEAD_DIM, HEAD_DIM), F32),
    pltpu.VMEM((N_OPS, TILE_T, HG_WIDTH), BF16),
    pltpu.VMEM((len(_KEY_OPS), TILE_T // CHUNK * HEADS, HEAD_DIM, CHUNK), BF16),
    pltpu.VMEM((TILE_T, HG_WIDTH), F32),
    pltpu.VMEM((TILE_T, HG_WIDTH), F32),
    pltpu.VMEM((2, TILE_T, HG_WIDTH), BF16),
]


def _bwd_kernel(x_ref, mod_ref, nw_ref, wq_ref, wv_ref, wz_ref, lb_ref, s0_ref,
                q_ref, v_ref, ob_ref, st_s, ops_s, opst_s, bc_s, z_s, g2_s):
    _init_state(s0_ref, st_s)
    h = _modulated_norm(x_ref[0], nw_ref[0:1, :], mod_ref)
    z_s[...] = jnp.dot(h, wz_ref[...], preferred_element_type=F32)
    q_ref[0] = jnp.dot(h, wq_ref[...], preferred_element_type=F32).astype(BF16)

    def value_proj():
        v_ref[0] = jnp.dot(h, wv_ref[...], preferred_element_type=F32).astype(BF16)

    def emit(rows, sl, o):
        ob_ref[0, rows, sl] = o.astype(BF16)

    _scan_tile(z_s, q_ref.at[0], v_ref.at[0], lb_ref[1:2, :], st_s, ops_s, opst_s, bc_s, g2_s, emit,
               reverse=True,
               after_gates=value_proj)


def _bwd_call(x, mod3, nw, wi, lb, sb):
    bsz, seq, _ = x.shape
    nt = seq // TILE_T
    tile = lambda b, t: (b, nt - 1 - t, 0)
    act = jax.ShapeDtypeStruct((bsz, seq, HG_WIDTH), BF16)
    act_spec = pl.BlockSpec((1, TILE_T, HG_WIDTH), tile)
    return pl.pallas_call(
        _bwd_kernel,
        grid=(bsz, nt),
        in_specs=[
            pl.BlockSpec((1, TILE_T, D_MODEL), tile),
            pl.BlockSpec((1, 3, D_MODEL), lambda b, t: (b, 0, 0)),
            pl.BlockSpec((ROW_PAD, D_MODEL), lambda b, t: (0, 0)),
            _w_in_cols(COL_Q), _w_in_cols(COL_V), _w_in_cols(COL_ZB),
            pl.BlockSpec((ROW_PAD, HG_WIDTH), lambda b, t: (0, 0)),
            pl.BlockSpec((1, HEADS, HEAD_DIM, HEAD_DIM), lambda b, t: (b, 0, 0, 0)),
        ],
        out_specs=[act_spec, act_spec, act_spec],
        out_shape=[act, act, act],
        scratch_shapes=_SCAN_SCRATCH,
        compiler_params=pltpu.CompilerParams(dimension_semantics=("arbitrary", "arbitrary"),
                                             vmem_limit_bytes=VMEM_LIMIT),
        name="bwd_scan",
    )(x, mod3, nw, wi, wi, wi, lb, sb)


def _fwd_kernel(x_ref, mod_ref, nw_ref, wz_ref, wg_ref, wc0_ref, wc1_ref, wc2_ref, wc3_ref,
                lb_ref, s0_ref, q_ref, v_ref, ob_ref, onw_ref, cw_ref, wo_ref, fnw_ref, out_ref,
                st_s, ops_s, opst_s, bc_s, z_s, g2_s, mix_s, cv_s):
    _init_state(s0_ref, st_s)
    g_s = z_s
    o_s = bc_s
    h = _modulated_norm(x_ref[0], nw_ref[0:1, :], mod_ref)

    z_s[...] = jnp.dot(h, wz_ref[...], preferred_element_type=F32)

    def conv_proj(j, wc_ref):
        cv_s[:, j * CONV_WIDTH:(j + 1) * CONV_WIDTH] = jnp.dot(h, wc_ref[...],
                                                               preferred_element_type=F32)

    conv_proj(0, wc0_ref)

    def conv_proj_rest():
        for j, wc_ref in enumerate((wc1_ref, wc2_ref, wc3_ref), start=1):
            conv_proj(j, wc_ref)

    def conv_branch():
        g_s[...] = jnp.dot(h, wg_ref[...], preferred_element_type=F32)
        n_tok = cv_s.shape[0]
        u = cv_s[:, 2 * CONV_WIDTH:3 * CONV_WIDTH] * cv_s[:, 0:CONV_WIDTH]
        col = lax.broadcasted_iota(jnp.int32, (n_tok, 1), 0) % GRID_W
        u_prev = jnp.where(col == 0, 0.0, pltpu.roll(u, 1, axis=0))
        u_next = jnp.where(col == GRID_W - 1, 0.0, pltpu.roll(u, n_tok - 1, axis=0))
        conv = cw_ref[0:1, :] * u_prev + cw_ref[1:2, :] * u + cw_ref[2:3, :] * u_next
        cg = cv_s[:, 3 * CONV_WIDTH:4 * CONV_WIDTH]
        y_cv = cv_s[:, CONV_WIDTH:2 * CONV_WIDTH] * conv * (cg * _sigmoid(cg))
        mix_s[:, HG_WIDTH:] = y_cv.astype(BF16)

    def emit(rows, sl, o):
        o_s[rows, sl] = o + ob_ref[0, rows, sl].astype(F32)

    def gated_readout(rows):
        onw = onw_ref[0:1, :]
        for hd in range(HEADS):
            sl = slice(hd * HEAD_DIM, (hd + 1) * HEAD_DIM)
            o = o_s[rows, sl]
            g = g_s[rows, sl]
            y = (o * _rms_rows(o) * onw[:, sl]) * (g * _sigmoid(g))
            mix_s[rows, sl] = y.astype(BF16)

    _scan_tile(z_s, q_ref.at[0], v_ref.at[0], lb_ref[0:1, :], st_s, ops_s, opst_s, bc_s, g2_s, emit,
               reverse=False,
               after_gates=conv_proj_rest, before_matmuls=conv_branch, after_chunk=gated_readout)

    half = x_ref.shape[1] // 2
    for rows in (slice(0, half), slice(half, 2 * half)):
        y = jnp.dot(mix_s[rows, :], wo_ref[...], preferred_element_type=F32)
        r = x_ref[0, rows, :] + mod_ref[0, 2:3, :] * y
        out_ref[0, rows, :] = (r * _rms_rows(r)) * fnw_ref[0:1, :]


def _fwd_call(x, mod3, nw, wi, lb, sf, q, v, ob, onw, conv_w, w_o, fnw):
    bsz, seq, _ = x.shape
    nt = seq // TILE_T
    tile = lambda b, t: (b, t, 0)
    const2 = lambda b, t: (0, 0)
    act_spec = pl.BlockSpec((1, TILE_T, HG_WIDTH), tile)
    return pl.pallas_call(
        _fwd_kernel,
        grid=(bsz, nt),
        in_specs=[
            pl.BlockSpec((1, TILE_T, D_MODEL), tile),
            pl.BlockSpec((1, 3, D_MODEL), lambda b, t: (b, 0, 0)),
            pl.BlockSpec((ROW_PAD, D_MODEL), const2),
            _w_in_cols(COL_ZF), _w_in_cols(COL_GATE),
            _w_in_cols(COL_CONV), _w_in_cols(COL_CONV + 1), _w_in_cols(COL_CONV + 2),
            _w_in_cols(COL_CONV + 3),
            pl.BlockSpec((ROW_PAD, HG_WIDTH), const2),
            pl.BlockSpec((1, HEADS, HEAD_DIM, HEAD_DIM), lambda b, t: (b, 0, 0, 0)),
            act_spec, act_spec, act_spec,
            pl.BlockSpec((ROW_PAD, HG_WIDTH), const2),
            pl.BlockSpec(conv_w.shape, const2),
            pl.BlockSpec(w_o.shape, const2, pipeline_mode=pl.Buffered(1)),
            pl.BlockSpec((ROW_PAD, D_MODEL), const2),
        ],
        out_specs=pl.BlockSpec((1, TILE_T, D_MODEL), tile),
        out_shape=jax.ShapeDtypeStruct((bsz, seq, D_MODEL), F32),
        scratch_shapes=_SCAN_SCRATCH + [
            pltpu.VMEM((TILE_T, HG_WIDTH + CONV_WIDTH), BF16),
            pltpu.VMEM((TILE_T, 4 * CONV_WIDTH), F32),
        ],
        compiler_params=pltpu.CompilerParams(dimension_semantics=("arbitrary", "arbitrary"),
                                             vmem_limit_bytes=VMEM_LIMIT),
        name="fwd_scan_out",
    )(x, mod3, nw, wi, wi, wi, wi, wi, wi, lb, sf, q, v, ob, onw, conv_w, w_o, fnw)


def kernel(x, c, ctx, c_ctx, norm_w, w_ada, b_ada, w_in, hg_lb_logits, hg_onorm_w, conv_w,
           w_out, final_norm_w):
    assert w_in.shape[0] == 1, "single-layer block"
    bsz = x.shape[0]
    cc = jnp.concatenate([c, c_ctx[None, :], jnp.zeros((8 - bsz - 1, D_MODEL), F32)], axis=0)
    mod, lb = _mod_call(cc, w_ada[0], b_ada[0][None, :], hg_lb_logits.reshape(-1, HG_WIDTH))
    mod3 = mod.reshape(8, 3, D_MODEL)

    wi = w_in[0].astype(BF16)
    nw = _pad_rows(norm_w[0][None, :])

    sf, sb = _ctx_call(ctx, mod3, nw, wi, lb)
    q, v, ob = _bwd_call(x, mod3, nw, wi, lb, sb)
    onw = _pad_rows(jnp.tile(hg_onorm_w[0], HEADS)[None, :])
    return _fwd_call(x, mod3, nw, wi, lb, sf, q, v, ob, onw, conv_w[0],
                     w_out[0].astype(BF16), _pad_rows(final_norm_w[None, :]))
```

```python
import jax
import jax.numpy as jnp
from jax import lax
from jax.experimental import pallas as pl
from jax.experimental.pallas import tpu as pltpu

D_MODEL = 1024
HEADS = 8
HEAD_DIM = 128
HG_WIDTH = HEADS * HEAD_DIM
CONV_WIDTH = 1024
GRID_W = 64
EPS = 1e-6

CHUNK = 128
BLK = 16
NBLK = CHUNK // BLK
LEVELS = NBLK.bit_length() - 1
N_OPS = 4 + LEVELS
_KEY_OPS = (1,) + tuple(range(4, 4 + LEVELS))
TILE_T = 512
COL_TILE = 256
EXP_RANGE = 115.0
LOG2F_FLOOR = 2.0 * EXP_RANGE / (BLK - 1)
VMEM_LIMIT = 60 * 1024 * 1024
ROW_PAD = 4

COL_Q, COL_V, COL_ZF, COL_ZB, COL_GATE, COL_CONV = 0, 1, 2, 3, 4, 5

F32 = jnp.float32
BF16 = jnp.bfloat16

_NT = (((1,), (1,)), ((), ()))
_TN = (((0,), (0,)), ((), ()))


def _w_in_cols(group):
    return pl.BlockSpec((D_MODEL, HG_WIDTH), lambda *_: (0, group), pipeline_mode=pl.Buffered(1))


def _pad_rows(a):
    return jnp.pad(a, ((0, ROW_PAD - a.shape[0]), (0, 0)))


def _sigmoid(x):
    return 1.0 / (1.0 + jnp.exp(-x))


def _rms_rows(xf):
    return lax.rsqrt(jnp.mean(xf * xf, axis=-1, keepdims=True) + EPS)


def _tri(n, reverse):
    t = lax.broadcasted_iota(jnp.int32, (n, n), 0)
    r = lax.broadcasted_iota(jnp.int32, (n, n), 1)
    return (r >= t) if reverse else (r <= t)


def _split_bf16(a):
    hi = lax.bitcast_convert_type(lax.bitcast_convert_type(a, jnp.uint32) & jnp.uint32(0xFFFF0000), F32)
    return hi.astype(BF16), (a - hi).astype(BF16)


def _cumsum_time(g, tri2):
    return jnp.dot(tri2, jnp.concatenate(_split_bf16(g), axis=0), preferred_element_type=F32)


def _gates(z, lb):
    f = lb + (1.0 - lb) * _sigmoid(z)
    return jnp.maximum(jnp.log2(f), -LOG2F_FLOOR), 1.0 - f


def _mod_kernel(cc_ref, w_ref, b_ref, lg_ref, mod_ref, lb_ref):
    cc = cc_ref[...]
    s = cc * _sigmoid(cc)
    s_hi, s_lo = _split_bf16(s)
    w_hi, w_lo = _split_bf16(w_ref[...])
    dot = lambda a, b: jnp.dot(a, b, preferred_element_type=F32)
    mod_ref[...] = dot(s_hi, w_hi) + (dot(s_hi, w_lo) + dot(s_lo, w_hi)) + b_ref[...]
    n_rows = lg_ref.shape[0] // 2
    for d in range(2):
        lg = lg_ref[d * n_rows:(d + 1) * n_rows, :]
        e = jnp.exp(lg - jnp.max(lg, axis=0, keepdims=True))
        lb_ref[d:d + 1, :] = e[0:1, :] / jnp.sum(e, axis=0, keepdims=True)
    lb_ref[2:, :] = jnp.zeros((ROW_PAD - 2, lb_ref.shape[1]), F32)


def _mod_call(cc, w_ada, b_ada, lb_logits):
    n_out = w_ada.shape[1]
    bn = 768
    return pl.pallas_call(
        _mod_kernel,
        grid=(n_out // bn,),
        in_specs=[
            pl.BlockSpec((8, D_MODEL), lambda j: (0, 0)),
            pl.BlockSpec((D_MODEL, bn), lambda j: (0, j)),
            pl.BlockSpec((1, bn), lambda j: (0, j)),
            pl.BlockSpec(lb_logits.shape, lambda j: (0, 0)),
        ],
        out_specs=[
            pl.BlockSpec((8, bn), lambda j: (0, j)),
            pl.BlockSpec((ROW_PAD, HG_WIDTH), lambda j: (0, 0)),
        ],
        out_shape=[
            jax.ShapeDtypeStruct((8, n_out), F32),
            jax.ShapeDtypeStruct((ROW_PAD, HG_WIDTH), F32),
        ],
        compiler_params=pltpu.CompilerParams(dimension_semantics=("arbitrary",)),
        name="adaln_mod",
    )(cc, w_ada, b_ada, lb_logits)


def _modulated_norm(x, nw, mod_ref):
    shift = mod_ref[0, 0:1, :]
    scale = mod_ref[0, 1:2, :]
    return ((x * _rms_rows(x)) * (nw * (1.0 + scale)) + shift).astype(BF16)


def _ctx_kernel(x_ref, mod_ref, nw_ref, wv_ref, wzf_ref, wzb_ref, lb_ref, sf_ref, sb_ref):
    n = x_ref.shape[1]
    h = _modulated_norm(x_ref[0], nw_ref[0:1, :], mod_ref)
    v = jnp.dot(h, wv_ref[...], preferred_element_type=F32).astype(BF16)
    for d, wz_ref, out_ref in ((0, wzf_ref, sf_ref), (1, wzb_ref, sb_ref)):
        z = jnp.dot(h, wz_ref[...], preferred_element_type=F32)
        g, kk = _gates(z, lb_ref[d:d + 1, :])
        tri = _tri(n, reverse=bool(d)).astype(BF16)
        bc = _cumsum_time(g, jnp.concatenate([tri, tri], axis=1))
        tot = bc[0:1, :] if d else bc[n - 1:n, :]
        ks = (kk * jnp.exp2(tot - bc)).astype(BF16)
        for hd in range(HEADS):
            sl = slice(hd * HEAD_DIM, (hd + 1) * HEAD_DIM)
            out_ref[0, hd] = lax.dot_general(ks[:, sl], v[:, sl], _TN, preferred_element_type=F32)


def _ctx_call(ctx, mod3, nw, wi, lb):
    bsz, n, _ = ctx.shape
    st_shape = jax.ShapeDtypeStruct((bsz, HEADS, HEAD_DIM, HEAD_DIM), F32)
    st_spec = pl.BlockSpec((1, HEADS, HEAD_DIM, HEAD_DIM), lambda b: (b, 0, 0, 0))
    return pl.pallas_call(
        _ctx_kernel,
        grid=(bsz,),
        in_specs=[
            pl.BlockSpec((1, n, D_MODEL), lambda b: (b, 0, 0)),
            pl.BlockSpec((1, 3, D_MODEL), lambda b: (bsz, 0, 0)),
            pl.BlockSpec((ROW_PAD, D_MODEL), lambda b: (0, 0)),
            _w_in_cols(COL_V), _w_in_cols(COL_ZF), _w_in_cols(COL_ZB),
            pl.BlockSpec((ROW_PAD, HG_WIDTH), lambda b: (0, 0)),
        ],
        out_specs=[st_spec, st_spec],
        out_shape=[st_shape, st_shape],
        compiler_params=pltpu.CompilerParams(dimension_semantics=("arbitrary",),
                                             vmem_limit_bytes=VMEM_LIMIT),
        name="ctx_states",
    )(ctx, mod3, nw, wi, wi, wi, lb)


def _pivot_row(i, level, reverse):
    half = NBLK >> (level + 1)
    boundary = ((i // (2 * half)) * 2 + 1) * half * BLK
    return boundary if reverse else boundary - 1


def _q_side(i, level, reverse):
    later = (i // (NBLK >> (level + 1))) % 2 == 1
    return later != reverse


def _tiles(r0, n_rows, row_tile, width, col_tile):
    return [(slice(r0 + r, r0 + r + row_tile), slice(c, c + col_tile))
            for r in range(0, n_rows, row_tile) for c in range(0, width, col_tile)]


def _scan_gates(c, z_ref, lb, bc_s, g2_s, tri2):
    for rows, cols in _tiles(c * CHUNK, CHUNK, BLK, HG_WIDTH, COL_TILE):
        g, kk = _gates(z_ref[rows, cols], lb[:, cols])
        g2_s[0, rows, cols], g2_s[1, rows, cols] = _split_bf16(g)
        z_ref[rows, cols] = kk
    rows = slice(c * CHUNK, (c + 1) * CHUNK)
    g2 = jnp.concatenate([g2_s[0, rows, :], g2_s[1, rows, :]], axis=0)
    bc_s[rows, :] = jnp.dot(tri2, g2, preferred_element_type=F32)


def _scan_prep(c, k_ref, q_ref, ops_s, opst_s, bc_s, reverse):
    r0 = c * CHUNK
    blk = lax.broadcasted_iota(jnp.int32, (NBLK, 1), 0)

    def per_block(row_of_block):
        out = None
        for i in reversed(range(NBLK)):
            if i + 1 < NBLK and row_of_block[i] == row_of_block[i + 1]:
                continue
            row = bc_s[r0 + row_of_block[i]:r0 + row_of_block[i] + 1, :]
            out = row if out is None else jnp.where(blk <= i, row, out)
        return jnp.broadcast_to(out, (NBLK, out.shape[1]))

    mid = 0.5 * (per_block([i * BLK for i in range(NBLK)])
                 + per_block([i * BLK + BLK - 1 for i in range(NBLK)]))
    tot_row = r0 if reverse else r0 + CHUNK - 1
    tot = bc_s[tot_row:tot_row + 1, :]
    scale = [jnp.exp2(mid), jnp.exp2(tot - mid)]
    for level in range(LEVELS):
        piv = per_block([_pivot_row(i, level, reverse) for i in range(NBLK)])
        scale.append(jnp.exp2(-jnp.abs(mid - piv)))
    for rows, cols in _tiles(r0, CHUNK, BLK, HG_WIDTH, COL_TILE):
        i = (rows.start - r0) // BLK
        d = bc_s[rows, cols] - mid[i:i + 1, cols]
        qd = q_ref[rows, cols].astype(F32) * jnp.exp2(d)
        kd = k_ref[rows, cols] * jnp.exp2(-d)
        ops_s[0, rows, cols] = qd.astype(BF16)
        ops_s[1, rows, cols] = kd.astype(BF16)
        ops_s[2, rows, cols] = (qd * scale[0][i:i + 1, cols]).astype(BF16)
        ops_s[3, rows, cols] = (kd * scale[1][i:i + 1, cols]).astype(BF16)
        for level in range(LEVELS):
            side = qd if _q_side(i, level, reverse) else kd
            ops_s[4 + level, rows, cols] = (side * scale[2 + level][i:i + 1, cols]).astype(BF16)
    rows = slice(r0, r0 + CHUNK)
    for hd in range(HEADS):
        sl = slice(hd * HEAD_DIM, (hd + 1) * HEAD_DIM)
        for j, n in enumerate(_KEY_OPS):
            opst_s[j, c * HEADS + hd] = ops_s[n, rows, sl].T
    return jnp.exp2(tot)


def _pair_masks(reverse):
    t = lax.broadcasted_iota(jnp.int32, (CHUNK, CHUNK), 0)
    s = lax.broadcasted_iota(jnp.int32, (CHUNK, CHUNK), 1)
    early, late = (t, s) if reverse else (s, t)
    masks = [((t // BLK) == (s // BLK)) & (early <= late)]
    for level in range(LEVELS):
        half = CHUNK >> (level + 1)
        masks.append(((late // half) == (early // half) + 1) & ((early // half) % 2 == 0))
    return masks


def _scan_mm(c, dec, v_ref, st_ref, ops_s, opst_s, masks, emit, reverse):
    r0 = c * CHUNK
    rows = slice(r0, r0 + CHUNK)
    q_blocks = [[i for i in range(NBLK) if _q_side(i, level, reverse)] for level in range(LEVELS)]
    attn = []
    for hd in range(HEADS):
        sl = slice(hd * HEAD_DIM, (hd + 1) * HEAD_DIM)
        same = jnp.dot(ops_s[0, rows, sl], opst_s[0, c * HEADS + hd], preferred_element_type=F32)
        cross = []
        for level in range(LEVELS):
            lhs = jnp.concatenate([ops_s[4 + level, r0 + i * BLK:r0 + (i + 1) * BLK, sl]
                                   for i in q_blocks[level]], axis=0)
            cross.append(jnp.dot(lhs, opst_s[1 + level, c * HEADS + hd],
                                 preferred_element_type=F32))
        blocks = []
        for i in range(NBLK):
            blk = slice(i * BLK, (i + 1) * BLK)
            a = jnp.where(masks[0][blk], same[blk], 0.0)
            for level in range(LEVELS):
                if i in q_blocks[level]:
                    j = q_blocks[level].index(i)
                    a = jnp.where(masks[1 + level][blk], cross[level][j * BLK:(j + 1) * BLK], a)
            blocks.append(a)
        attn.append(jnp.concatenate(blocks, axis=0).astype(BF16))
    for hd in range(HEADS):
        sl = slice(hd * HEAD_DIM, (hd + 1) * HEAD_DIM)
        st = st_ref[hd]
        v = v_ref[rows, sl]
        lhs = jnp.concatenate([ops_s[2, rows, sl], attn[hd]], axis=1)
        rhs = jnp.concatenate([st.astype(BF16), v], axis=0)
        emit(rows, sl, jnp.dot(lhs, rhs, preferred_element_type=F32))
        kv = lax.dot_general(ops_s[3, rows, sl], v, _TN, preferred_element_type=F32)
        dec_col = jnp.transpose(jnp.broadcast_to(dec[:, sl], (HEAD_DIM, HEAD_DIM)))
        st_ref[hd] = st * dec_col + kv


def _scan_tile(z_ref, q_ref, v_ref, lb, st_ref, ops_s, opst_s, bc_s, g2_s, emit, reverse,
               after_gates=None, before_matmuls=None, after_chunk=None):
    n_chunks = z_ref.shape[0] // CHUNK
    order = range(n_chunks - 1, -1, -1) if reverse else range(n_chunks)
    tri_b = _tri(CHUNK, reverse).astype(BF16)
    tri2 = jnp.concatenate([tri_b, tri_b], axis=1)
    masks = _pair_masks(reverse)
    for c in order:
        _scan_gates(c, z_ref, lb, bc_s, g2_s, tri2)
    if after_gates is not None:
        after_gates()
    dec = {c: _scan_prep(c, z_ref, q_ref, ops_s, opst_s, bc_s, reverse) for c in order}
    if before_matmuls is not None:
        before_matmuls()
    for c in order:
        _scan_mm(c, dec[c], v_ref, st_ref, ops_s, opst_s, masks, emit, reverse)
        if after_chunk is not None:
            after_chunk(slice(c * CHUNK, (c + 1) * CHUNK))


def _init_state(s0_ref, st_s):
    @pl.when(pl.program_id(1) == 0)
    def _():
        st_s[...] = s0_ref[0]


_SCAN_SCRATCH = [
    pltpu.VMEM((HEADS, HEAD_DIM, HEAD_DIM), F32),
    pltpu.VMEM((N_OPS, TILE_T, HG_WIDTH), BF16),
    pltpu.VMEM((len(_KEY_OPS), TILE_T // CHUNK * HEADS, HEAD_DIM, CHUNK), BF16),
    pltpu.VMEM((TILE_T, HG_WIDTH), F32),
    pltpu.VMEM((TILE_T, HG_WIDTH), F32),
    pltpu.VMEM((2, TILE_T, HG_WIDTH), BF16),
]


def _bwd_kernel(x_ref, mod_ref, nw_ref, wq_ref, wv_ref, wz_ref, lb_ref, s0_ref,
                h_ref, q_ref, v_ref, ob_ref, st_s, ops_s, opst_s, bc_s, z_s, g2_s):
    _init_state(s0_ref, st_s)
    h = _modulated_norm(x_ref[0], nw_ref[0:1, :], mod_ref)
    h_ref[0] = h
    z_s[...] = jnp.dot(h, wz_ref[...], preferred_element_type=F32)
    q_ref[0] = jnp.dot(h, wq_ref[...], preferred_element_type=F32).astype(BF16)

    def value_proj():
        v_ref[0] = jnp.dot(h, wv_ref[...], preferred_element_type=F32).astype(BF16)

    def emit(rows, sl, o):
        ob_ref[0, rows, sl] = o.astype(BF16)

    _scan_tile(z_s, q_ref.at[0], v_ref.at[0], lb_ref[1:2, :], st_s, ops_s, opst_s, bc_s, g2_s, emit,
               reverse=True,
               after_gates=value_proj)


def _bwd_call(x, mod3, nw, wi, lb, sb):
    bsz, seq, _ = x.shape
    nt = seq // TILE_T
    tile = lambda b, t: (b, nt - 1 - t, 0)
    act = jax.ShapeDtypeStruct((bsz, seq, HG_WIDTH), BF16)
    act_spec = pl.BlockSpec((1, TILE_T, HG_WIDTH), tile)
    return pl.pallas_call(
        _bwd_kernel,
        grid=(bsz, nt),
        in_specs=[
            pl.BlockSpec((1, TILE_T, D_MODEL), tile),
            pl.BlockSpec((1, 3, D_MODEL), lambda b, t: (b, 0, 0)),
            pl.BlockSpec((ROW_PAD, D_MODEL), lambda b, t: (0, 0)),
            _w_in_cols(COL_Q), _w_in_cols(COL_V), _w_in_cols(COL_ZB),
            pl.BlockSpec((ROW_PAD, HG_WIDTH), lambda b, t: (0, 0)),
            pl.BlockSpec((1, HEADS, HEAD_DIM, HEAD_DIM), lambda b, t: (b, 0, 0, 0)),
        ],
        out_specs=[act_spec, act_spec, act_spec, act_spec],
        out_shape=[act, act, act, act],
        scratch_shapes=_SCAN_SCRATCH,
        compiler_params=pltpu.CompilerParams(dimension_semantics=("arbitrary", "arbitrary"),
                                             vmem_limit_bytes=VMEM_LIMIT),
        name="bwd_scan",
    )(x, mod3, nw, wi, wi, wi, lb, sb)


def _fwd_kernel(x_ref, h_ref, mod_ref, wz_ref, wg_ref, wc0_ref, wc1_ref, wc2_ref, wc3_ref,
                lb_ref, s0_ref, q_ref, v_ref, ob_ref, onw_ref, cw_ref, wo_ref, fnw_ref, out_ref,
                st_s, ops_s, opst_s, bc_s, z_s, g2_s, mix_s, cv_s):
    _init_state(s0_ref, st_s)
    g_s = z_s
    o_s = bc_s
    h = h_ref[0]

    z_s[...] = jnp.dot(h, wz_ref[...], preferred_element_type=F32)

    def conv_proj(j, wc_ref):
        cv_s[:, j * CONV_WIDTH:(j + 1) * CONV_WIDTH] = jnp.dot(h, wc_ref[...],
                                                               preferred_element_type=F32)

    conv_proj(0, wc0_ref)

    def conv_proj_rest():
        for j, wc_ref in enumerate((wc1_ref, wc2_ref, wc3_ref), start=1):
            conv_proj(j, wc_ref)

    def conv_branch():
        g_s[...] = jnp.dot(h, wg_ref[...], preferred_element_type=F32)
        n_tok = cv_s.shape[0]
        u = cv_s[:, 2 * CONV_WIDTH:3 * CONV_WIDTH] * cv_s[:, 0:CONV_WIDTH]
        col = lax.broadcasted_iota(jnp.int32, (n_tok, 1), 0) % GRID_W
        u_prev = jnp.where(col == 0, 0.0, pltpu.roll(u, 1, axis=0))
        u_next = jnp.where(col == GRID_W - 1, 0.0, pltpu.roll(u, n_tok - 1, axis=0))
        conv = cw_ref[0:1, :] * u_prev + cw_ref[1:2, :] * u + cw_ref[2:3, :] * u_next
        cg = cv_s[:, 3 * CONV_WIDTH:4 * CONV_WIDTH]
        y_cv = cv_s[:, CONV_WIDTH:2 * CONV_WIDTH] * conv * (cg * _sigmoid(cg))
        mix_s[:, HG_WIDTH:] = y_cv.astype(BF16)

    def emit(rows, sl, o):
        o_s[rows, sl] = o + ob_ref[0, rows, sl].astype(F32)

    def gated_readout(rows):
        onw = onw_ref[0:1, :]
        for hd in range(HEADS):
            sl = slice(hd * HEAD_DIM, (hd + 1) * HEAD_DIM)
            o = o_s[rows, sl]
            g = g_s[rows, sl]
            y = (o * _rms_rows(o) * onw[:, sl]) * (g * _sigmoid(g))
            mix_s[rows, sl] = y.astype(BF16)

    _scan_tile(z_s, q_ref.at[0], v_ref.at[0], lb_ref[0:1, :], st_s, ops_s, opst_s, bc_s, g2_s, emit,
               reverse=False,
               after_gates=conv_proj_rest, before_matmuls=conv_branch, after_chunk=gated_readout)

    half = x_ref.shape[1] // 2
    for rows in (slice(0, half), slice(half, 2 * half)):
        y = jnp.dot(mix_s[rows, :], wo_ref[...], preferred_element_type=F32)
        r = x_ref[0, rows, :] + mod_ref[0, 2:3, :] * y
        out_ref[0, rows, :] = (r * _rms_rows(r)) * fnw_ref[0:1, :]


def _fwd_call(x, h, mod3, wi, lb, sf, q, v, ob, onw, conv_w, w_o, fnw):
    bsz, seq, _ = x.shape
    nt = seq // TILE_T
    tile = lambda b, t: (b, t, 0)
    const2 = lambda b, t: (0, 0)
    act_spec = pl.BlockSpec((1, TILE_T, HG_WIDTH), tile)
    return pl.pallas_call(
        _fwd_kernel,
        grid=(bsz, nt),
        in_specs=[
            pl.BlockSpec((1, TILE_T, D_MODEL), tile),
            act_spec,
            pl.BlockSpec((1, 3, D_MODEL), lambda b, t: (b, 0, 0)),
            _w_in_cols(COL_ZF), _w_in_cols(COL_GATE),
            _w_in_cols(COL_CONV), _w_in_cols(COL_CONV + 1), _w_in_cols(COL_CONV + 2),
            _w_in_cols(COL_CONV + 3),
            pl.BlockSpec((ROW_PAD, HG_WIDTH), const2),
            pl.BlockSpec((1, HEADS, HEAD_DIM, HEAD_DIM), lambda b, t: (b, 0, 0, 0)),
            act_spec, act_spec, act_spec,
            pl.BlockSpec((ROW_PAD, HG_WIDTH), const2),
            pl.BlockSpec(conv_w.shape, const2),
            pl.BlockSpec(w_o.shape, const2, pipeline_mode=pl.Buffered(1)),
            pl.BlockSpec((ROW_PAD, D_MODEL), const2),
        ],
        out_specs=pl.BlockSpec((1, TILE_T, D_MODEL), tile),
        out_shape=jax.ShapeDtypeStruct((bsz, seq, D_MODEL), F32),
        scratch_shapes=_SCAN_SCRATCH + [
            pltpu.VMEM((TILE_T, HG_WIDTH + CONV_WIDTH), BF16),
            pltpu.VMEM((TILE_T, 4 * CONV_WIDTH), F32),
        ],
        compiler_params=pltpu.CompilerParams(dimension_semantics=("arbitrary", "arbitrary"),
                                             vmem_limit_bytes=VMEM_LIMIT),
        name="fwd_scan_out",
    )(x, h, mod3, wi, wi, wi, wi, wi, wi, lb, sf, q, v, ob, onw, conv_w, w_o, fnw)


def kernel(x, c, ctx, c_ctx, norm_w, w_ada, b_ada, w_in, hg_lb_logits, hg_onorm_w, conv_w,
           w_out, final_norm_w):
    assert w_in.shape[0] == 1, "single-layer block"
    bsz = x.shape[0]
    cc = jnp.concatenate([c, c_ctx[None, :], jnp.zeros((8 - bsz - 1, D_MODEL), F32)], axis=0)
    mod, lb = _mod_call(cc, w_ada[0], b_ada[0][None, :], hg_lb_logits.reshape(-1, HG_WIDTH))
    mod3 = mod.reshape(8, 3, D_MODEL)

    wi = w_in[0].astype(BF16)
    nw = _pad_rows(norm_w[0][None, :])

    sf, sb = _ctx_call(ctx, mod3, nw, wi, lb)
    h, q, v, ob = _bwd_call(x, mod3, nw, wi, lb, sb)
    onw = _pad_rows(jnp.tile(hg_onorm_w[0], HEADS)[None, :])
    return _fwd_call(x, h, mod3, wi, lb, sf, q, v, ob, onw, conv_w[0],
                     w_out[0].astype(BF16), _pad_rows(final_norm_w[None, :]))
```

```python
import jax
import jax.numpy as jnp
from jax import lax
from jax.experimental import pallas as pl
from jax.experimental.pallas import tpu as pltpu

D_MODEL = 1024
HEADS = 8
HEAD_DIM = 128
HG_WIDTH = HEADS * HEAD_DIM
CONV_WIDTH = 1024
GRID_W = 64
EPS = 1e-6

CHUNK = 128
BLK = 16
NBLK = CHUNK // BLK
LEVELS = NBLK.bit_length() - 1
N_OPS = 4 + LEVELS
_KEY_OPS = (1,) + tuple(range(4, 4 + LEVELS))
TILE_T = 512
LANES = 128
LANE_TILES = HG_WIDTH // LANES
COL_TILE = 256
EXP_RANGE = 115.0
LOG2F_FLOOR = 2.0 * EXP_RANGE / (BLK - 1)
VMEM_LIMIT = 60 * 1024 * 1024
ROW_PAD = 4

COL_Q, COL_V, COL_ZF, COL_ZB, COL_GATE, COL_CONV = 0, 1, 2, 3, 4, 5

F32 = jnp.float32
BF16 = jnp.bfloat16

_TN = (((0,), (0,)), ((), ()))


def _w_in_cols(group):
    return pl.BlockSpec((D_MODEL, HG_WIDTH), lambda *_: (0, group), pipeline_mode=pl.Buffered(1))


def _pad_rows(a):
    return jnp.pad(a, ((0, ROW_PAD - a.shape[0]), (0, 0)))


def _sigmoid(x):
    return 1.0 / (1.0 + jnp.exp(-x))


def _rms_rows(xf):
    return lax.rsqrt(jnp.mean(xf * xf, axis=-1, keepdims=True) + EPS)


def _tri(n, reverse):
    t = lax.broadcasted_iota(jnp.int32, (n, n), 0)
    r = lax.broadcasted_iota(jnp.int32, (n, n), 1)
    return (r >= t) if reverse else (r <= t)


def _split_bf16(a):
    hi = a.astype(BF16)
    return hi, (a - hi.astype(F32)).astype(BF16)


def _cumsum_time(g, tri2):
    return jnp.dot(tri2, jnp.concatenate(_split_bf16(g), axis=0), preferred_element_type=F32)


def _gates(z, lb):
    f = lb + (1.0 - lb) * _sigmoid(z)
    return jnp.maximum(jnp.log2(f), -LOG2F_FLOOR), 1.0 - f


def _mod_kernel(cc_ref, w_ref, b_ref, lg_ref, mod_ref, lb_ref):
    cc = cc_ref[...]
    s = cc * _sigmoid(cc)
    s_hi, s_lo = _split_bf16(s)
    w_hi, w_lo = _split_bf16(w_ref[...])
    dot = lambda a, b: jnp.dot(a, b, preferred_element_type=F32)
    mod_ref[...] = dot(s_hi, w_hi) + (dot(s_hi, w_lo) + dot(s_lo, w_hi)) + b_ref[...]
    n_rows = lg_ref.shape[0] // 2
    for d in range(2):
        lg = lg_ref[d * n_rows:(d + 1) * n_rows, :]
        e = jnp.exp(lg - jnp.max(lg, axis=0, keepdims=True))
        lb_ref[d:d + 1, :] = e[0:1, :] / jnp.sum(e, axis=0, keepdims=True)
    lb_ref[2:, :] = jnp.zeros((ROW_PAD - 2, lb_ref.shape[1]), F32)


def _mod_call(cc, w_ada, b_ada, lb_logits):
    n_out = w_ada.shape[1]
    bn = 768
    return pl.pallas_call(
        _mod_kernel,
        grid=(n_out // bn,),
        in_specs=[
            pl.BlockSpec((8, D_MODEL), lambda j: (0, 0)),
            pl.BlockSpec((D_MODEL, bn), lambda j: (0, j)),
            pl.BlockSpec((1, bn), lambda j: (0, j)),
            pl.BlockSpec(lb_logits.shape, lambda j: (0, 0)),
        ],
        out_specs=[
            pl.BlockSpec((8, bn), lambda j: (0, j)),
            pl.BlockSpec((ROW_PAD, HG_WIDTH), lambda j: (0, 0)),
        ],
        out_shape=[
            jax.ShapeDtypeStruct((8, n_out), F32),
            jax.ShapeDtypeStruct((ROW_PAD, HG_WIDTH), F32),
        ],
        compiler_params=pltpu.CompilerParams(dimension_semantics=("arbitrary",)),
        name="adaln_mod",
    )(cc, w_ada, b_ada, lb_logits)


def _modulated_norm(x, nw, mod_ref):
    shift = mod_ref[0, 0:1, :]
    scale = mod_ref[0, 1:2, :]
    return ((x * _rms_rows(x)) * (nw * (1.0 + scale)) + shift).astype(BF16)


def _ctx_kernel(x_ref, mod_ref, nw_ref, wv_ref, wzf_ref, wzb_ref, lb_ref, sf_ref, sb_ref):
    n = x_ref.shape[1]
    h = _modulated_norm(x_ref[0], nw_ref[0:1, :], mod_ref)
    v = jnp.dot(h, wv_ref[...], preferred_element_type=F32).astype(BF16)
    for d, wz_ref, out_ref in ((0, wzf_ref, sf_ref), (1, wzb_ref, sb_ref)):
        z = jnp.dot(h, wz_ref[...], preferred_element_type=F32)
        g, kk = _gates(z, lb_ref[d:d + 1, :])
        tri = _tri(n, reverse=bool(d)).astype(BF16)
        bc = _cumsum_time(g, jnp.concatenate([tri, tri], axis=1))
        tot = bc[0:1, :] if d else bc[n - 1:n, :]
        ks = (kk * jnp.exp2(tot - bc)).astype(BF16)
        for hd in range(HEADS):
            sl = slice(hd * HEAD_DIM, (hd + 1) * HEAD_DIM)
            out_ref[0, hd] = lax.dot_general(ks[:, sl], v[:, sl], _TN, preferred_element_type=F32)


def _ctx_call(ctx, mod3, nw, wi, lb):
    bsz, n, _ = ctx.shape
    st_shape = jax.ShapeDtypeStruct((bsz, HEADS, HEAD_DIM, HEAD_DIM), F32)
    st_spec = pl.BlockSpec((1, HEADS, HEAD_DIM, HEAD_DIM), lambda b: (b, 0, 0, 0))
    return pl.pallas_call(
        _ctx_kernel,
        grid=(bsz,),
        in_specs=[
            pl.BlockSpec((1, n, D_MODEL), lambda b: (b, 0, 0)),
            pl.BlockSpec((1, 3, D_MODEL), lambda b: (bsz, 0, 0)),
            pl.BlockSpec((ROW_PAD, D_MODEL), lambda b: (0, 0)),
            _w_in_cols(COL_V), _w_in_cols(COL_ZF), _w_in_cols(COL_ZB),
            pl.BlockSpec((ROW_PAD, HG_WIDTH), lambda b: (0, 0)),
        ],
        out_specs=[st_spec, st_spec],
        out_shape=[st_shape, st_shape],
        compiler_params=pltpu.CompilerParams(dimension_semantics=("arbitrary",),
                                             vmem_limit_bytes=VMEM_LIMIT),
        name="ctx_states",
    )(ctx, mod3, nw, wi, wi, wi, lb)


def _pivot_row(i, level, reverse):
    half = NBLK >> (level + 1)
    boundary = ((i // (2 * half)) * 2 + 1) * half * BLK
    return boundary if reverse else boundary - 1


def _q_side(i, level, reverse):
    later = (i // (NBLK >> (level + 1))) % 2 == 1
    return later != reverse


def _tiles(r0, n_rows, row_tile, width, col_tile):
    return [(slice(r0 + r, r0 + r + row_tile), slice(c, c + col_tile))
            for r in range(0, n_rows, row_tile) for c in range(0, width, col_tile)]


def _scan_gates(c, z_ref, lb, bc_s, g2_s, tri2):
    for rows, cols in _tiles(c * CHUNK, CHUNK, BLK, HG_WIDTH, COL_TILE):
        g, kk = _gates(z_ref[rows, cols], lb[:, cols])
        g_hi = g.astype(BF16)
        g2_s[0, rows, cols] = g_hi
        g2_s[1, rows, cols] = (g - g_hi.astype(F32)).astype(BF16)
        z_ref[rows, cols] = kk
    rows = slice(c * CHUNK, (c + 1) * CHUNK)
    g2 = jnp.concatenate([g2_s[0, rows, :], g2_s[1, rows, :]], axis=0)
    bc_s[rows, :] = jnp.dot(tri2, g2, preferred_element_type=F32)


def _scan_prep(c, k_ref, q_ref, ops_s, opst_s, bc_s, vec_s, reverse):
    r0 = c * CHUNK
    blk = lax.broadcasted_iota(jnp.int32, (NBLK, 1), 0)

    def per_block(row_of_block):
        out = None
        for i in reversed(range(NBLK)):
            if i + 1 < NBLK and row_of_block[i] == row_of_block[i + 1]:
                continue
            row = bc_s[r0 + row_of_block[i]:r0 + row_of_block[i] + 1, :]
            out = row if out is None else jnp.where(blk <= i, row, out)
        return jnp.broadcast_to(out, (NBLK, out.shape[1]))

    mid = 0.5 * (per_block([i * BLK for i in range(NBLK)])
                 + per_block([i * BLK + BLK - 1 for i in range(NBLK)]))
    tot_row = r0 if reverse else r0 + CHUNK - 1
    tot = bc_s[tot_row:tot_row + 1, :]
    scale = [jnp.exp2(mid), jnp.exp2(tot - mid)]
    for level in range(LEVELS):
        piv = per_block([_pivot_row(i, level, reverse) for i in range(NBLK)])
        scale.append(jnp.exp2(-jnp.abs(mid - piv)))
    vecs = [mid] + scale
    for k, vec in enumerate(vecs):
        for j in range(LANE_TILES):
            base = ((c * len(vecs) + k) * LANE_TILES + j) * NBLK
            vec_s[base:base + NBLK, :] = vec[:, j * LANES:(j + 1) * LANES]

    def block_row(k, i, cols):
        tiles = [vec_s[pl.ds(((c * len(vecs) + k) * LANE_TILES + j) * NBLK + i, BLK, stride=0), :]
                 for j in range(cols.start // LANES, cols.stop // LANES)]
        return jnp.concatenate(tiles, axis=1)

    for rows, cols in _tiles(r0, CHUNK, BLK, HG_WIDTH, COL_TILE):
        i = (rows.start - r0) // BLK
        d = bc_s[rows, cols] - block_row(0, i, cols)
        qd = q_ref[rows, cols].astype(F32) * jnp.exp2(d)
        kd = k_ref[rows, cols] * jnp.exp2(-d)
        ops_s[0, rows, cols] = qd.astype(BF16)
        ops_s[1, rows, cols] = kd.astype(BF16)
        ops_s[2, rows, cols] = (qd * block_row(1, i, cols)).astype(BF16)
        ops_s[3, rows, cols] = (kd * block_row(2, i, cols)).astype(BF16)
        for level in range(LEVELS):
            side = qd if _q_side(i, level, reverse) else kd
            ops_s[4 + level, rows, cols] = (side * block_row(3 + level, i, cols)).astype(BF16)
    rows = slice(r0, r0 + CHUNK)
    for hd in range(HEADS):
        sl = slice(hd * HEAD_DIM, (hd + 1) * HEAD_DIM)
        for j, n in enumerate(_KEY_OPS):
            opst_s[j, c * HEADS + hd] = ops_s[n, rows, sl].T
    return jnp.exp2(tot)


def _pair_masks(reverse):
    t = lax.broadcasted_iota(jnp.int32, (CHUNK, CHUNK), 0)
    s = lax.broadcasted_iota(jnp.int32, (CHUNK, CHUNK), 1)
    early, late = (t, s) if reverse else (s, t)
    masks = [((t // BLK) == (s // BLK)) & (early <= late)]
    for level in range(LEVELS):
        half = CHUNK >> (level + 1)
        masks.append(((late // half) == (early // half) + 1) & ((early // half) % 2 == 0))
    return masks


def _scan_mm(c, dec, v_ref, st_ref, ops_s, opst_s, masks, emit, reverse):
    r0 = c * CHUNK
    rows = slice(r0, r0 + CHUNK)
    q_blocks = [[i for i in range(NBLK) if _q_side(i, level, reverse)] for level in range(LEVELS)]
    attn = []
    for hd in range(HEADS):
        sl = slice(hd * HEAD_DIM, (hd + 1) * HEAD_DIM)
        same = jnp.dot(ops_s[0, rows, sl], opst_s[0, c * HEADS + hd], preferred_element_type=F32)
        cross = []
        for level in range(LEVELS):
            lhs = jnp.concatenate([ops_s[4 + level, r0 + i * BLK:r0 + (i + 1) * BLK, sl]
                                   for i in q_blocks[level]], axis=0)
            cross.append(jnp.dot(lhs, opst_s[1 + level, c * HEADS + hd],
                                 preferred_element_type=F32))
        blocks = []
        for i in range(NBLK):
            blk = slice(i * BLK, (i + 1) * BLK)
            a = jnp.where(masks[0][blk], same[blk], 0.0)
            for level in range(LEVELS):
                if i in q_blocks[level]:
                    j = q_blocks[level].index(i)
                    a = jnp.where(masks[1 + level][blk], cross[level][j * BLK:(j + 1) * BLK], a)
            blocks.append(a)
        attn.append(jnp.concatenate(blocks, axis=0).astype(BF16))
    for hd in range(HEADS):
        sl = slice(hd * HEAD_DIM, (hd + 1) * HEAD_DIM)
        st = st_ref[hd]
        v = v_ref[rows, sl]
        lhs = jnp.concatenate([ops_s[2, rows, sl], attn[hd]], axis=1)
        rhs = jnp.concatenate([st.astype(BF16), v], axis=0)
        emit(rows, sl, jnp.dot(lhs, rhs, preferred_element_type=F32))
        kv = lax.dot_general(ops_s[3, rows, sl], v, _TN, preferred_element_type=F32)
        dec_col = jnp.transpose(jnp.broadcast_to(dec[:, sl], (HEAD_DIM, HEAD_DIM)))
        st_ref[hd] = st * dec_col + kv


def _scan_tile(z_ref, q_ref, v_ref, lb, st_ref, ops_s, opst_s, bc_s, g2_s, vec_s, emit, reverse,
               after_gates=None, before_matmuls=None, after_chunk=None):
    n_chunks = z_ref.shape[0] // CHUNK
    order = range(n_chunks - 1, -1, -1) if reverse else range(n_chunks)
    tri_b = _tri(CHUNK, reverse).astype(BF16)
    tri2 = jnp.concatenate([tri_b, tri_b], axis=1)
    masks = _pair_masks(reverse)
    for c in order:
        _scan_gates(c, z_ref, lb, bc_s, g2_s, tri2)
    if after_gates is not None:
        after_gates()
    dec = {c: _scan_prep(c, z_ref, q_ref, ops_s, opst_s, bc_s, vec_s, reverse) for c in order}
    if before_matmuls is not None:
        before_matmuls()
    for c in order:
        _scan_mm(c, dec[c], v_ref, st_ref, ops_s, opst_s, masks, emit, reverse)
        if after_chunk is not None:
            after_chunk(slice(c * CHUNK, (c + 1) * CHUNK))


def _init_state(s0_ref, st_s):
    @pl.when(pl.program_id(1) == 0)
    def _():
        st_s[...] = s0_ref[0]


_SCAN_SCRATCH = [
    pltpu.VMEM((HEADS, HEAD_DIM, HEAD_DIM), F32),
    pltpu.VMEM((N_OPS, TILE_T, HG_WIDTH), BF16),
    pltpu.VMEM((len(_KEY_OPS), TILE_T // CHUNK * HEADS, HEAD_DIM, CHUNK), BF16),
    pltpu.VMEM((TILE_T, HG_WIDTH), F32),
    pltpu.VMEM((TILE_T, HG_WIDTH), F32),
    pltpu.VMEM((2, TILE_T, HG_WIDTH), BF16),
    pltpu.VMEM((TILE_T // CHUNK * (3 + LEVELS) * LANE_TILES * NBLK, LANES), F32),
]


def _bwd_kernel(x_ref, mod_ref, nw_ref, wq_ref, wv_ref, wz_ref, lb_ref, s0_ref,
                q_ref, v_ref, ob_ref, st_s, ops_s, opst_s, bc_s, z_s, g2_s, vec_s):
    _init_state(s0_ref, st_s)
    h = _modulated_norm(x_ref[0], nw_ref[0:1, :], mod_ref)
    z_s[...] = jnp.dot(h, wz_ref[...], preferred_element_type=F32)
    q_ref[0] = jnp.dot(h, wq_ref[...], preferred_element_type=F32).astype(BF16)

    def value_proj():
        v_ref[0] = jnp.dot(h, wv_ref[...], preferred_element_type=F32).astype(BF16)

    def emit(rows, sl, o):
        ob_ref[0, rows, sl] = o.astype(BF16)

    _scan_tile(z_s, q_ref.at[0], v_ref.at[0], lb_ref[1:2, :], st_s, ops_s, opst_s, bc_s, g2_s, vec_s,
               emit,
               reverse=True,
               after_gates=value_proj)


def _bwd_call(x, mod3, nw, wi, lb, sb):
    bsz, seq, _ = x.shape
    nt = seq // TILE_T
    tile = lambda b, t: (b, nt - 1 - t, 0)
    act = jax.ShapeDtypeStruct((bsz, seq, HG_WIDTH), BF16)
    act_spec = pl.BlockSpec((1, TILE_T, HG_WIDTH), tile)
    return pl.pallas_call(
        _bwd_kernel,
        grid=(bsz, nt),
        in_specs=[
            pl.BlockSpec((1, TILE_T, D_MODEL), tile),
            pl.BlockSpec((1, 3, D_MODEL), lambda b, t: (b, 0, 0)),
            pl.BlockSpec((ROW_PAD, D_MODEL), lambda b, t: (0, 0)),
            _w_in_cols(COL_Q), _w_in_cols(COL_V), _w_in_cols(COL_ZB),
            pl.BlockSpec((ROW_PAD, HG_WIDTH), lambda b, t: (0, 0)),
            pl.BlockSpec((1, HEADS, HEAD_DIM, HEAD_DIM), lambda b, t: (b, 0, 0, 0)),
        ],
        out_specs=[act_spec, act_spec, act_spec],
        out_shape=[act, act, act],
        scratch_shapes=_SCAN_SCRATCH,
        compiler_params=pltpu.CompilerParams(dimension_semantics=("arbitrary", "arbitrary"),
                                             vmem_limit_bytes=VMEM_LIMIT),
        name="bwd_scan",
    )(x, mod3, nw, wi, wi, wi, lb, sb)


def _fwd_kernel(x_ref, mod_ref, nw_ref, wz_ref, wg_ref, wc0_ref, wc1_ref, wc2_ref, wc3_ref,
                lb_ref, s0_ref, q_ref, v_ref, ob_ref, onw_ref, cw_ref, wo_ref, fnw_ref, out_ref,
                st_s, ops_s, opst_s, bc_s, z_s, g2_s, vec_s, mix_s, cv_s):
    _init_state(s0_ref, st_s)
    g_s = z_s
    o_s = bc_s
    h = _modulated_norm(x_ref[0], nw_ref[0:1, :], mod_ref)

    z_s[...] = jnp.dot(h, wz_ref[...], preferred_element_type=F32)

    def conv_proj(j, wc_ref):
        cv_s[:, j * CONV_WIDTH:(j + 1) * CONV_WIDTH] = jnp.dot(h, wc_ref[...],
                                                               preferred_element_type=F32)

    conv_proj(0, wc0_ref)

    def conv_proj_rest():
        for j, wc_ref in enumerate((wc1_ref, wc2_ref, wc3_ref), start=1):
            conv_proj(j, wc_ref)

    def conv_branch():
        g_s[...] = jnp.dot(h, wg_ref[...], preferred_element_type=F32)
        n_tok = cv_s.shape[0]
        u = cv_s[:, 2 * CONV_WIDTH:3 * CONV_WIDTH] * cv_s[:, 0:CONV_WIDTH]
        col = lax.broadcasted_iota(jnp.int32, (n_tok, 1), 0) % GRID_W
        u_prev = jnp.where(col == 0, 0.0, pltpu.roll(u, 1, axis=0))
        u_next = jnp.where(col == GRID_W - 1, 0.0, pltpu.roll(u, n_tok - 1, axis=0))
        conv = cw_ref[0:1, :] * u_prev + cw_ref[1:2, :] * u + cw_ref[2:3, :] * u_next
        cg = cv_s[:, 3 * CONV_WIDTH:4 * CONV_WIDTH]
        y_cv = cv_s[:, CONV_WIDTH:2 * CONV_WIDTH] * conv * (cg * _sigmoid(cg))
        mix_s[:, HG_WIDTH:] = y_cv.astype(BF16)

    def emit(rows, sl, o):
        o_s[rows, sl] = o + ob_ref[0, rows, sl].astype(F32)

    def gated_readout(rows):
        onw = onw_ref[0:1, :]
        for hd in range(HEADS):
            sl = slice(hd * HEAD_DIM, (hd + 1) * HEAD_DIM)
            o = o_s[rows, sl]
            g = g_s[rows, sl]
            y = (o * _rms_rows(o) * onw[:, sl]) * (g * _sigmoid(g))
            mix_s[rows, sl] = y.astype(BF16)

    _scan_tile(z_s, q_ref.at[0], v_ref.at[0], lb_ref[0:1, :], st_s, ops_s, opst_s, bc_s, g2_s, vec_s,
               emit,
               reverse=False,
               after_gates=conv_proj_rest, before_matmuls=conv_branch, after_chunk=gated_readout)

    half = x_ref.shape[1] // 2
    for rows in (slice(0, half), slice(half, 2 * half)):
        y = jnp.dot(mix_s[rows, :], wo_ref[...], preferred_element_type=F32)
        r = x_ref[0, rows, :] + mod_ref[0, 2:3, :] * y
        out_ref[0, rows, :] = (r * _rms_rows(r)) * fnw_ref[0:1, :]


def _fwd_call(x, mod3, nw, wi, lb, sf, q, v, ob, onw, conv_w, w_o, fnw):
    bsz, seq, _ = x.shape
    nt = seq // TILE_T
    tile = lambda b, t: (b, t, 0)
    const2 = lambda b, t: (0, 0)
    act_spec = pl.BlockSpec((1, TILE_T, HG_WIDTH), tile)
    return pl.pallas_call(
        _fwd_kernel,
        grid=(bsz, nt),
        in_specs=[
            pl.BlockSpec((1, TILE_T, D_MODEL), tile),
            pl.BlockSpec((1, 3, D_MODEL), lambda b, t: (b, 0, 0)),
            pl.BlockSpec((ROW_PAD, D_MODEL), const2),
            _w_in_cols(COL_ZF), _w_in_cols(COL_GATE),
            _w_in_cols(COL_CONV), _w_in_cols(COL_CONV + 1), _w_in_cols(COL_CONV + 2),
            _w_in_cols(COL_CONV + 3),
            pl.BlockSpec((ROW_PAD, HG_WIDTH), const2),
            pl.BlockSpec((1, HEADS, HEAD_DIM, HEAD_DIM), lambda b, t: (b, 0, 0, 0)),
            act_spec, act_spec, act_spec,
            pl.BlockSpec((ROW_PAD, HG_WIDTH), const2),
            pl.BlockSpec(conv_w.shape, const2),
            pl.BlockSpec(w_o.shape, const2, pipeline_mode=pl.Buffered(1)),
            pl.BlockSpec((ROW_PAD, D_MODEL), const2),
        ],
        out_specs=pl.BlockSpec((1, TILE_T, D_MODEL), tile),
        out_shape=jax.ShapeDtypeStruct((bsz, seq, D_MODEL), F32),
        scratch_shapes=_SCAN_SCRATCH + [
            pltpu.VMEM((TILE_T, HG_WIDTH + CONV_WIDTH), BF16),
            pltpu.VMEM((TILE_T, 4 * CONV_WIDTH), F32),
        ],
        compiler_params=pltpu.CompilerParams(dimension_semantics=("arbitrary", "arbitrary"),
                                             vmem_limit_bytes=VMEM_LIMIT),
        name="fwd_scan_out",
    )(x, mod3, nw, wi, wi, wi, wi, wi, wi, lb, sf, q, v, ob, onw, conv_w, w_o, fnw)


def kernel(x, c, ctx, c_ctx, norm_w, w_ada, b_ada, w_in, hg_lb_logits, hg_onorm_w, conv_w,
           w_out, final_norm_w):
    assert w_in.shape[0] == 1, "single-layer block"
    bsz = x.shape[0]
    cc = jnp.concatenate([c, c_ctx[None, :], jnp.zeros((8 - bsz - 1, D_MODEL), F32)], axis=0)
    mod, lb = _mod_call(cc, w_ada[0], b_ada[0][None, :], hg_lb_logits.reshape(-1, HG_WIDTH))
    mod3 = mod.reshape(8, 3, D_MODEL)

    wi = w_in[0].astype(BF16)
    nw = _pad_rows(norm_w[0][None, :])

    sf, sb = _ctx_call(ctx, mod3, nw, wi, lb)
    q, v, ob = _bwd_call(x, mod3, nw, wi, lb, sb)
    onw = _pad_rows(jnp.tile(hg_onorm_w[0], HEADS)[None, :])
    return _fwd_call(x, mod3, nw, wi, lb, sf, q, v, ob, onw, conv_w[0],
                     w_out[0].astype(BF16), _pad_rows(final_norm_w[None, :]))
```

```python
import jax
import jax.numpy as jnp
from jax import lax
from jax.experimental import pallas as pl
from jax.experimental.pallas import tpu as pltpu

D_MODEL = 1024
HEADS = 8
HEAD_DIM = 128
HG_WIDTH = HEADS * HEAD_DIM
CONV_WIDTH = 1024
GRID_W = 64
EPS = 1e-6

CHUNK = 128
BLK = 16
NBLK = CHUNK // BLK
LEVELS = NBLK.bit_length() - 1
N_OPS = 4 + LEVELS
_KEY_OPS = (1,) + tuple(range(4, 4 + LEVELS))
TILE_T = 512
COL_TILE = 256
EXP_RANGE = 115.0
LOG2F_FLOOR = 2.0 * EXP_RANGE / (BLK - 1)
VMEM_LIMIT = 60 * 1024 * 1024
ROW_PAD = 4

COL_Q, COL_V, COL_ZF, COL_ZB, COL_GATE, COL_CONV = 0, 1, 2, 3, 4, 5

F32 = jnp.float32
BF16 = jnp.bfloat16

_TN = (((0,), (0,)), ((), ()))


def _w_in_cols(group):
    return pl.BlockSpec((D_MODEL, HG_WIDTH), lambda *_: (0, group), pipeline_mode=pl.Buffered(1))


def _pad_rows(a):
    return jnp.pad(a, ((0, ROW_PAD - a.shape[0]), (0, 0)))


def _sigmoid(x):
    return 1.0 / (1.0 + jnp.exp(-x))


def _rms_rows(xf):
    return lax.rsqrt(jnp.mean(xf * xf, axis=-1, keepdims=True) + EPS)


def _tri(n, reverse):
    t = lax.broadcasted_iota(jnp.int32, (n, n), 0)
    r = lax.broadcasted_iota(jnp.int32, (n, n), 1)
    return (r >= t) if reverse else (r <= t)


def _split_bf16(a):
    hi = a.astype(BF16)
    return hi, (a - hi.astype(F32)).astype(BF16)


def _cumsum_time(g, tri2):
    return jnp.dot(tri2, jnp.concatenate(_split_bf16(g), axis=0), preferred_element_type=F32)


def _gates(z, lb):
    f = lb + (1.0 - lb) * _sigmoid(z)
    return jnp.maximum(jnp.log2(f), -LOG2F_FLOOR), 1.0 - f


def _mod_kernel(cc_ref, w_ref, b_ref, lg_ref, mod_ref, lb_ref):
    cc = cc_ref[...]
    s = cc * _sigmoid(cc)
    s_hi, s_lo = _split_bf16(s)
    w_hi, w_lo = _split_bf16(w_ref[...])
    dot = lambda a, b: jnp.dot(a, b, preferred_element_type=F32)
    mod_ref[...] = dot(s_hi, w_hi) + (dot(s_hi, w_lo) + dot(s_lo, w_hi)) + b_ref[...]
    n_rows = lg_ref.shape[0] // 2
    for d in range(2):
        lg = lg_ref[d * n_rows:(d + 1) * n_rows, :]
        e = jnp.exp(lg - jnp.max(lg, axis=0, keepdims=True))
        lb_ref[d:d + 1, :] = e[0:1, :] / jnp.sum(e, axis=0, keepdims=True)
    lb_ref[2:, :] = jnp.zeros((ROW_PAD - 2, lb_ref.shape[1]), F32)


def _mod_call(cc, w_ada, b_ada, lb_logits):
    n_out = w_ada.shape[1]
    bn = 768
    return pl.pallas_call(
        _mod_kernel,
        grid=(n_out // bn,),
        in_specs=[
            pl.BlockSpec((8, D_MODEL), lambda j: (0, 0)),
            pl.BlockSpec((D_MODEL, bn), lambda j: (0, j)),
            pl.BlockSpec((1, bn), lambda j: (0, j)),
            pl.BlockSpec(lb_logits.shape, lambda j: (0, 0)),
        ],
        out_specs=[
            pl.BlockSpec((8, bn), lambda j: (0, j)),
            pl.BlockSpec((ROW_PAD, HG_WIDTH), lambda j: (0, 0)),
        ],
        out_shape=[
            jax.ShapeDtypeStruct((8, n_out), F32),
            jax.ShapeDtypeStruct((ROW_PAD, HG_WIDTH), F32),
        ],
        compiler_params=pltpu.CompilerParams(dimension_semantics=("arbitrary",)),
        name="adaln_mod",
    )(cc, w_ada, b_ada, lb_logits)


def _modulated_norm(x, nw, mod_ref):
    shift = mod_ref[0, 0:1, :]
    scale = mod_ref[0, 1:2, :]
    return ((x * _rms_rows(x)) * (nw * (1.0 + scale)) + shift).astype(BF16)


def _ctx_kernel(x_ref, mod_ref, nw_ref, wv_ref, wzf_ref, wzb_ref, lb_ref, sf_ref, sb_ref):
    n = x_ref.shape[1]
    h = _modulated_norm(x_ref[0], nw_ref[0:1, :], mod_ref)
    v = jnp.dot(h, wv_ref[...], preferred_element_type=F32).astype(BF16)
    for d, wz_ref, out_ref in ((0, wzf_ref, sf_ref), (1, wzb_ref, sb_ref)):
        z = jnp.dot(h, wz_ref[...], preferred_element_type=F32)
        g, kk = _gates(z, lb_ref[d:d + 1, :])
        tri = _tri(n, reverse=bool(d)).astype(BF16)
        bc = _cumsum_time(g, jnp.concatenate([tri, tri], axis=1))
        tot = bc[0:1, :] if d else bc[n - 1:n, :]
        ks = (kk * jnp.exp2(tot - bc)).astype(BF16)
        for hd in range(HEADS):
            sl = slice(hd * HEAD_DIM, (hd + 1) * HEAD_DIM)
            out_ref[0, hd] = lax.dot_general(ks[:, sl], v[:, sl], _TN, preferred_element_type=F32)


def _ctx_call(ctx, mod3, nw, wi, lb):
    bsz, n, _ = ctx.shape
    st_shape = jax.ShapeDtypeStruct((bsz, HEADS, HEAD_DIM, HEAD_DIM), F32)
    st_spec = pl.BlockSpec((1, HEADS, HEAD_DIM, HEAD_DIM), lambda b: (b, 0, 0, 0))
    return pl.pallas_call(
        _ctx_kernel,
        grid=(bsz,),
        in_specs=[
            pl.BlockSpec((1, n, D_MODEL), lambda b: (b, 0, 0)),
            pl.BlockSpec((1, 3, D_MODEL), lambda b: (bsz, 0, 0)),
            pl.BlockSpec((ROW_PAD, D_MODEL), lambda b: (0, 0)),
            _w_in_cols(COL_V), _w_in_cols(COL_ZF), _w_in_cols(COL_ZB),
            pl.BlockSpec((ROW_PAD, HG_WIDTH), lambda b: (0, 0)),
        ],
        out_specs=[st_spec, st_spec],
        out_shape=[st_shape, st_shape],
        compiler_params=pltpu.CompilerParams(dimension_semantics=("arbitrary",),
                                             vmem_limit_bytes=VMEM_LIMIT),
        name="ctx_states",
    )(ctx, mod3, nw, wi, wi, wi, lb)


def _pivot_row(i, level, reverse):
    half = NBLK >> (level + 1)
    boundary = ((i // (2 * half)) * 2 + 1) * half * BLK
    return boundary if reverse else boundary - 1


def _q_side(i, level, reverse):
    later = (i // (NBLK >> (level + 1))) % 2 == 1
    return later != reverse


def _tiles(r0, n_rows, row_tile, width, col_tile):
    return [(slice(r0 + r, r0 + r + row_tile), slice(c, c + col_tile))
            for r in range(0, n_rows, row_tile) for c in range(0, width, col_tile)]


def _scan_gates(c, z_ref, lb, bc_s, g2_s, tri2):
    for rows, cols in _tiles(c * CHUNK, CHUNK, BLK, HG_WIDTH, COL_TILE):
        g, kk = _gates(z_ref[rows, cols], lb[:, cols])
        g_hi = g.astype(BF16)
        g2_s[0, rows, cols] = g_hi
        g2_s[1, rows, cols] = (g - g_hi.astype(F32)).astype(BF16)
        z_ref[rows, cols] = kk
    rows = slice(c * CHUNK, (c + 1) * CHUNK)
    g2 = jnp.concatenate([g2_s[0, rows, :], g2_s[1, rows, :]], axis=0)
    bc_s[rows, :] = jnp.dot(tri2, g2, preferred_element_type=F32)


def _scan_prep(c, k_ref, q_ref, ops_s, opst_s, bc_s, reverse):
    r0 = c * CHUNK
    blk = lax.broadcasted_iota(jnp.int32, (NBLK, 1), 0)

    def per_block(row_of_block):
        out = None
        for i in reversed(range(NBLK)):
            if i + 1 < NBLK and row_of_block[i] == row_of_block[i + 1]:
                continue
            row = bc_s[r0 + row_of_block[i]:r0 + row_of_block[i] + 1, :]
            out = row if out is None else jnp.where(blk <= i, row, out)
        return jnp.broadcast_to(out, (NBLK, out.shape[1]))

    mid = 0.5 * (per_block([i * BLK for i in range(NBLK)])
                 + per_block([i * BLK + BLK - 1 for i in range(NBLK)]))
    tot_row = r0 if reverse else r0 + CHUNK - 1
    tot = bc_s[tot_row:tot_row + 1, :]
    scale = [jnp.exp2(mid), jnp.exp2(tot - mid)]
    for level in range(LEVELS):
        piv = per_block([_pivot_row(i, level, reverse) for i in range(NBLK)])
        scale.append(jnp.exp2(-jnp.abs(mid - piv)))
    for rows, cols in _tiles(r0, CHUNK, BLK, HG_WIDTH, COL_TILE):
        i = (rows.start - r0) // BLK
        d = bc_s[rows, cols] - mid[i:i + 1, cols]
        qd = q_ref[rows, cols].astype(F32) * jnp.exp2(d)
        kd = k_ref[rows, cols] * jnp.exp2(-d)
        ops_s[0, rows, cols] = qd.astype(BF16)
        ops_s[1, rows, cols] = kd.astype(BF16)
        ops_s[2, rows, cols] = (qd * scale[0][i:i + 1, cols]).astype(BF16)
        ops_s[3, rows, cols] = (kd * scale[1][i:i + 1, cols]).astype(BF16)
        for level in range(LEVELS):
            side = qd if _q_side(i, level, reverse) else kd
            ops_s[4 + level, rows, cols] = (side * scale[2 + level][i:i + 1, cols]).astype(BF16)
    rows = slice(r0, r0 + CHUNK)
    for hd in range(HEADS):
        sl = slice(hd * HEAD_DIM, (hd + 1) * HEAD_DIM)
        for j, n in enumerate(_KEY_OPS):
            opst_s[j, c * HEADS + hd] = ops_s[n, rows, sl].T
    return jnp.exp2(tot)


def _pair_masks(reverse):
    t = lax.broadcasted_iota(jnp.int32, (CHUNK, CHUNK), 0)
    s = lax.broadcasted_iota(jnp.int32, (CHUNK, CHUNK), 1)
    early, late = (t, s) if reverse else (s, t)
    masks = [((t // BLK) == (s // BLK)) & (early <= late)]
    for level in range(LEVELS):
        half = CHUNK >> (level + 1)
        masks.append(((late // half) == (early // half) + 1) & ((early // half) % 2 == 0))
    return masks


def _scan_mm(c, dec, v_ref, st_ref, ops_s, opst_s, masks, emit, reverse):
    r0 = c * CHUNK
    rows = slice(r0, r0 + CHUNK)
    q_blocks = [[i for i in range(NBLK) if _q_side(i, level, reverse)] for level in range(LEVELS)]
    attn = []
    for hd in range(HEADS):
        sl = slice(hd * HEAD_DIM, (hd + 1) * HEAD_DIM)
        same = jnp.dot(ops_s[0, rows, sl], opst_s[0, c * HEADS + hd], preferred_element_type=F32)
        cross = []
        for level in range(LEVELS):
            lhs = jnp.concatenate([ops_s[4 + level, r0 + i * BLK:r0 + (i + 1) * BLK, sl]
                                   for i in q_blocks[level]], axis=0)
            cross.append(jnp.dot(lhs, opst_s[1 + level, c * HEADS + hd],
                                 preferred_element_type=F32))
        blocks = []
        for i in range(NBLK):
            blk = slice(i * BLK, (i + 1) * BLK)
            a = jnp.where(masks[0][blk], same[blk], 0.0)
            for level in range(LEVELS):
                if i in q_blocks[level]:
                    j = q_blocks[level].index(i)
                    a = jnp.where(masks[1 + level][blk], cross[level][j * BLK:(j + 1) * BLK], a)
            blocks.append(a)
        attn.append(jnp.concatenate(blocks, axis=0).astype(BF16))
    for hd in range(HEADS):
        sl = slice(hd * HEAD_DIM, (hd + 1) * HEAD_DIM)
        st = st_ref[hd]
        v = v_ref[rows, sl]
        lhs = jnp.concatenate([ops_s[2, rows, sl], attn[hd]], axis=1)
        rhs = jnp.concatenate([st.astype(BF16), v], axis=0)
        emit(rows, sl, jnp.dot(lhs, rhs, preferred_element_type=F32))
        kv = lax.dot_general(ops_s[3, rows, sl], v, _TN, preferred_element_type=F32)
        dec_col = jnp.transpose(jnp.broadcast_to(dec[:, sl], (HEAD_DIM, HEAD_DIM)))
        st_ref[hd] = st * dec_col + kv


def _scan_tile(z_ref, q_ref, v_ref, lb, st_ref, ops_s, opst_s, bc_s, g2_s, emit, reverse,
               after_gates=None, before_matmuls=None, after_chunk=None):
    n_chunks = z_ref.shape[0] // CHUNK
    order = range(n_chunks - 1, -1, -1) if reverse else range(n_chunks)
    tri_b = _tri(CHUNK, reverse).astype(BF16)
    tri2 = jnp.concatenate([tri_b, tri_b], axis=1)
    masks = _pair_masks(reverse)
    for c in order:
        _scan_gates(c, z_ref, lb, bc_s, g2_s, tri2)
    if after_gates is not None:
        after_gates()
    dec = {c: _scan_prep(c, z_ref, q_ref, ops_s, opst_s, bc_s, reverse) for c in order}
    if before_matmuls is not None:
        before_matmuls()
    for c in order:
        _scan_mm(c, dec[c], v_ref, st_ref, ops_s, opst_s, masks, emit, reverse)
        if after_chunk is not None:
            after_chunk(slice(c * CHUNK, (c + 1) * CHUNK))


def _init_state(s0_ref, st_s):
    @pl.when(pl.program_id(1) == 0)
    def _():
        st_s[...] = s0_ref[0]


_SCAN_SCRATCH = [
    pltpu.VMEM((HEADS, HEAD_DIM, HEAD_DIM), F32),
    pltpu.VMEM((N_OPS, TILE_T, HG_WIDTH), BF16),
    pltpu.VMEM((len(_KEY_OPS), TILE_T // CHUNK * HEADS, HEAD_DIM, CHUNK), BF16),
    pltpu.VMEM((TILE_T, HG_WIDTH), F32),
    pltpu.VMEM((TILE_T, HG_WIDTH), F32),
    pltpu.VMEM((2, TILE_T, HG_WIDTH), BF16),
]


def _bwd_kernel(x_ref, mod_ref, nw_ref, wq_ref, wv_ref, wz_ref, lb_ref, s0_ref,
                q_ref, v_ref, ob_ref, st_s, ops_s, opst_s, bc_s, z_s, g2_s):
    _init_state(s0_ref, st_s)
    h = _modulated_norm(x_ref[0], nw_ref[0:1, :], mod_ref)
    z_s[...] = jnp.dot(h, wz_ref[...], preferred_element_type=F32)
    q_ref[0] = jnp.dot(h, wq_ref[...], preferred_element_type=F32).astype(BF16)

    def value_proj():
        v_ref[0] = jnp.dot(h, wv_ref[...], preferred_element_type=F32).astype(BF16)

    def emit(rows, sl, o):
        ob_ref[0, rows, sl] = o.astype(BF16)

    _scan_tile(z_s, q_ref.at[0], v_ref.at[0], lb_ref[1:2, :], st_s, ops_s, opst_s, bc_s, g2_s, emit,
               reverse=True,
               after_gates=value_proj)


def _bwd_call(x, mod3, nw, wi, lb, sb):
    bsz, seq, _ = x.shape
    nt = seq // TILE_T
    tile = lambda b, t: (b, nt - 1 - t, 0)
    act = jax.ShapeDtypeStruct((bsz, seq, HG_WIDTH), BF16)
    act_spec = pl.BlockSpec((1, TILE_T, HG_WIDTH), tile)
    return pl.pallas_call(
        _bwd_kernel,
        grid=(bsz, nt),
        in_specs=[
            pl.BlockSpec((1, TILE_T, D_MODEL), tile),
            pl.BlockSpec((1, 3, D_MODEL), lambda b, t: (b, 0, 0)),
            pl.BlockSpec((ROW_PAD, D_MODEL), lambda b, t: (0, 0)),
            _w_in_cols(COL_Q), _w_in_cols(COL_V), _w_in_cols(COL_ZB),
            pl.BlockSpec((ROW_PAD, HG_WIDTH), lambda b, t: (0, 0)),
            pl.BlockSpec((1, HEADS, HEAD_DIM, HEAD_DIM), lambda b, t: (b, 0, 0, 0)),
        ],
        out_specs=[act_spec, act_spec, act_spec],
        out_shape=[act, act, act],
        scratch_shapes=_SCAN_SCRATCH,
        compiler_params=pltpu.CompilerParams(dimension_semantics=("arbitrary", "arbitrary"),
                                             vmem_limit_bytes=VMEM_LIMIT),
        name="bwd_scan",
    )(x, mod3, nw, wi, wi, wi, lb, sb)


def _fwd_kernel(x_ref, mod_ref, nw_ref, wz_ref, wg_ref, wc0_ref, wc1_ref, wc2_ref, wc3_ref,
                lb_ref, s0_ref, q_ref, v_ref, ob_ref, onw_ref, cw_ref, wo_ref, fnw_ref, out_ref,
                st_s, ops_s, opst_s, bc_s, z_s, g2_s, mix_s, cv_s):
    _init_state(s0_ref, st_s)
    g_s = z_s
    o_s = bc_s
    h = _modulated_norm(x_ref[0], nw_ref[0:1, :], mod_ref)

    z_s[...] = jnp.dot(h, wz_ref[...], preferred_element_type=F32)

    def conv_proj(j, wc_ref):
        cv_s[:, j * CONV_WIDTH:(j + 1) * CONV_WIDTH] = jnp.dot(h, wc_ref[...],
                                                               preferred_element_type=F32)

    conv_proj(0, wc0_ref)

    def conv_proj_rest():
        for j, wc_ref in enumerate((wc1_ref, wc2_ref, wc3_ref), start=1):
            conv_proj(j, wc_ref)

    def conv_branch():
        g_s[...] = jnp.dot(h, wg_ref[...], preferred_element_type=F32)
        n_tok = cv_s.shape[0]
        u = cv_s[:, 2 * CONV_WIDTH:3 * CONV_WIDTH] * cv_s[:, 0:CONV_WIDTH]
        col = lax.broadcasted_iota(jnp.int32, (n_tok, 1), 0) % GRID_W
        u_prev = jnp.where(col == 0, 0.0, pltpu.roll(u, 1, axis=0))
        u_next = jnp.where(col == GRID_W - 1, 0.0, pltpu.roll(u, n_tok - 1, axis=0))
        conv = cw_ref[0:1, :] * u_prev + cw_ref[1:2, :] * u + cw_ref[2:3, :] * u_next
        cg = cv_s[:, 3 * CONV_WIDTH:4 * CONV_WIDTH]
        y_cv = cv_s[:, CONV_WIDTH:2 * CONV_WIDTH] * conv * (cg * _sigmoid(cg))
        mix_s[:, HG_WIDTH:] = y_cv.astype(BF16)

    def emit(rows, sl, o):
        o_s[rows, sl] = o + ob_ref[0, rows, sl].astype(F32)

    def gated_readout(rows):
        onw = onw_ref[0:1, :]
        for hd in range(HEADS):
            sl = slice(hd * HEAD_DIM, (hd + 1) * HEAD_DIM)
            o = o_s[rows, sl]
            g = g_s[rows, sl]
            y = (o * _rms_rows(o) * onw[:, sl]) * (g * _sigmoid(g))
            mix_s[rows, sl] = y.astype(BF16)

    _scan_tile(z_s, q_ref.at[0], v_ref.at[0], lb_ref[0:1, :], st_s, ops_s, opst_s, bc_s, g2_s, emit,
               reverse=False,
               after_gates=conv_proj_rest, before_matmuls=conv_branch, after_chunk=gated_readout)

    half = x_ref.shape[1] // 2
    for rows in (slice(0, half), slice(half, 2 * half)):
        y = jnp.dot(mix_s[rows, :], wo_ref[...], preferred_element_type=F32)
        r = x_ref[0, rows, :] + mod_ref[0, 2:3, :] * y
        out_ref[0, rows, :] = (r * _rms_rows(r)) * fnw_ref[0:1, :]


def _fwd_call(x, mod3, nw, wi, lb, sf, q, v, ob, onw, conv_w, w_o, fnw):
    bsz, seq, _ = x.shape
    nt = seq // TILE_T
    tile = lambda b, t: (b, t, 0)
    const2 = lambda b, t: (0, 0)
    act_spec = pl.BlockSpec((1, TILE_T, HG_WIDTH), tile)
    return pl.pallas_call(
        _fwd_kernel,
        grid=(bsz, nt),
        in_specs=[
            pl.BlockSpec((1, TILE_T, D_MODEL), tile),
            pl.BlockSpec((1, 3, D_MODEL), lambda b, t: (b, 0, 0)),
            pl.BlockSpec((ROW_PAD, D_MODEL), const2),
            _w_in_cols(COL_ZF), _w_in_cols(COL_GATE),
            _w_in_cols(COL_CONV), _w_in_cols(COL_CONV + 1), _w_in_cols(COL_CONV + 2),
            _w_in_cols(COL_CONV + 3),
            pl.BlockSpec((ROW_PAD, HG_WIDTH), const2),
            pl.BlockSpec((1, HEADS, HEAD_DIM, HEAD_DIM), lambda b, t: (b, 0, 0, 0)),
            act_spec, act_spec, act_spec,
            pl.BlockSpec((ROW_PAD, HG_WIDTH), const2),
            pl.BlockSpec(conv_w.shape, const2),
            pl.BlockSpec(w_o.shape, const2, pipeline_mode=pl.Buffered(1)),
            pl.BlockSpec((ROW_PAD, D_MODEL), const2),
        ],
        out_specs=pl.BlockSpec((1, TILE_T, D_MODEL), tile),
        out_shape=jax.ShapeDtypeStruct((bsz, seq, D_MODEL), F32),
        scratch_shapes=_SCAN_SCRATCH + [
            pltpu.VMEM((TILE_T, HG_WIDTH + CONV_WIDTH), BF16),
            pltpu.VMEM((TILE_T, 4 * CONV_WIDTH), F32),
        ],
        compiler_params=pltpu.CompilerParams(dimension_semantics=("arbitrary", "arbitrary"),
                                             vmem_limit_bytes=VMEM_LIMIT),
        name="fwd_scan_out",
    )(x, mod3, nw, wi, wi, wi, wi, wi, wi, lb, sf, q, v, ob, onw, conv_w, w_o, fnw)


def kernel(x, c, ctx, c_ctx, norm_w, w_ada, b_ada, w_in, hg_lb_logits, hg_onorm_w, conv_w,
           w_out, final_norm_w):
    assert w_in.shape[0] == 1, "single-layer block"
    bsz = x.shape[0]
    cc = jnp.concatenate([c, c_ctx[None, :], jnp.zeros((8 - bsz - 1, D_MODEL), F32)], axis=0)
    mod, lb = _mod_call(cc, w_ada[0], b_ada[0][None, :], hg_lb_logits.reshape(-1, HG_WIDTH))
    mod3 = mod.reshape(8, 3, D_MODEL)

    wi = w_in[0].astype(BF16)
    nw = _pad_rows(norm_w[0][None, :])

    sf, sb = _ctx_call(ctx, mod3, nw, wi, lb)
    q, v, ob = _bwd_call(x, mod3, nw, wi, lb, sb)
    onw = _pad_rows(jnp.tile(hg_onorm_w[0], HEADS)[None, :])
    return _fwd_call(x, mod3, nw, wi, lb, sf, q, v, ob, onw, conv_w[0],
                     w_out[0].astype(BF16), _pad_rows(final_norm_w[None, :]))
```

```python
import jax
import jax.numpy as jnp
from jax import lax
from jax.experimental import pallas as pl
from jax.experimental.pallas import tpu as pltpu

D_MODEL = 1024
HEADS = 8
HEAD_DIM = 128
HG_WIDTH = HEADS * HEAD_DIM
CONV_WIDTH = 1024
GRID_W = 64
EPS = 1e-6

CHUNK = 128
BLK = 16
NBLK = CHUNK // BLK
LEVELS = NBLK.bit_length() - 1
N_OPS = 4 + LEVELS
_KEY_OPS = (1,) + tuple(range(4, 4 + LEVELS))
TILE_T = 512
COL_TILE = 256
EXP_RANGE = 115.0
LOG2F_FLOOR = 2.0 * EXP_RANGE / (BLK - 1)
VMEM_CAPACITY = 64 * 1024 * 1024
VMEM_LIMIT = VMEM_CAPACITY - 4 * 1024 * 1024
ROW_PAD = 4

COL_Q, COL_V, COL_ZF, COL_ZB, COL_GATE, COL_CONV = 0, 1, 2, 3, 4, 5

F32 = jnp.float32
BF16 = jnp.bfloat16

_TN = (((0,), (0,)), ((), ()))


def _w_in_cols(group):
    return pl.BlockSpec((D_MODEL, HG_WIDTH), lambda *_: (0, group), pipeline_mode=pl.Buffered(1))


def _pad_rows(a):
    return jnp.pad(a, ((0, ROW_PAD - a.shape[0]), (0, 0)))


def _sigmoid(x):
    return 1.0 / (1.0 + jnp.exp(-x))


def _rms_rows(xf):
    return lax.rsqrt(jnp.mean(xf * xf, axis=-1, keepdims=True) + EPS)


def _tri(n, reverse):
    t = lax.broadcasted_iota(jnp.int32, (n, n), 0)
    r = lax.broadcasted_iota(jnp.int32, (n, n), 1)
    return (r >= t) if reverse else (r <= t)


def _split_bf16(a):
    hi = a.astype(BF16)
    return hi, (a - hi.astype(F32)).astype(BF16)


def _cumsum_time(g, tri2):
    return jnp.dot(tri2, jnp.concatenate(_split_bf16(g), axis=0), preferred_element_type=F32)


def _gates(z, lb):
    f = lb + (1.0 - lb) * _sigmoid(z)
    return jnp.maximum(jnp.log2(f), -LOG2F_FLOOR), 1.0 - f


def _mod_kernel(cc_ref, w_ref, b_ref, lg_ref, mod_ref, lb_ref):
    cc = cc_ref[...]
    s = cc * _sigmoid(cc)
    s_hi, s_lo = _split_bf16(s)
    w_hi, w_lo = _split_bf16(w_ref[...])
    dot = lambda a, b: jnp.dot(a, b, preferred_element_type=F32)
    mod_ref[...] = dot(s_hi, w_hi) + (dot(s_hi, w_lo) + dot(s_lo, w_hi)) + b_ref[...]
    n_rows = lg_ref.shape[0] // 2
    for d in range(2):
        lg = lg_ref[d * n_rows:(d + 1) * n_rows, :]
        e = jnp.exp(lg - jnp.max(lg, axis=0, keepdims=True))
        lb_ref[d:d + 1, :] = e[0:1, :] / jnp.sum(e, axis=0, keepdims=True)
    lb_ref[2:, :] = jnp.zeros((ROW_PAD - 2, lb_ref.shape[1]), F32)


def _mod_call(cc, w_ada, b_ada, lb_logits):
    n_out = w_ada.shape[1]
    bn = n_out // 2
    return pl.pallas_call(
        _mod_kernel,
        grid=(n_out // bn,),
        in_specs=[
            pl.BlockSpec((8, D_MODEL), lambda j: (0, 0)),
            pl.BlockSpec((D_MODEL, bn), lambda j: (0, j)),
            pl.BlockSpec((1, bn), lambda j: (0, j)),
            pl.BlockSpec(lb_logits.shape, lambda j: (0, 0)),
        ],
        out_specs=[
            pl.BlockSpec((8, bn), lambda j: (0, j)),
            pl.BlockSpec((ROW_PAD, HG_WIDTH), lambda j: (0, 0)),
        ],
        out_shape=[
            jax.ShapeDtypeStruct((8, n_out), F32),
            jax.ShapeDtypeStruct((ROW_PAD, HG_WIDTH), F32),
        ],
        compiler_params=pltpu.CompilerParams(dimension_semantics=("arbitrary",)),
        name="adaln_mod",
    )(cc, w_ada, b_ada, lb_logits)


def _modulated_norm(x, nw, mod_ref):
    shift = mod_ref[0, 0:1, :]
    scale = mod_ref[0, 1:2, :]
    return ((x * _rms_rows(x)) * (nw * (1.0 + scale)) + shift).astype(BF16)


def _ctx_kernel(x_ref, mod_ref, nw_ref, wv_ref, wzf_ref, wzb_ref, lb_ref, sf_ref, sb_ref):
    n = x_ref.shape[1]
    h = _modulated_norm(x_ref[0], nw_ref[0:1, :], mod_ref)
    v = jnp.dot(h, wv_ref[...], preferred_element_type=F32).astype(BF16)
    for d, wz_ref, out_ref in ((0, wzf_ref, sf_ref), (1, wzb_ref, sb_ref)):
        z = jnp.dot(h, wz_ref[...], preferred_element_type=F32)
        g, kk = _gates(z, lb_ref[d:d + 1, :])
        tri = _tri(n, reverse=bool(d)).astype(BF16)
        bc = _cumsum_time(g, jnp.concatenate([tri, tri], axis=1))
        tot = bc[0:1, :] if d else bc[n - 1:n, :]
        ks = (kk * jnp.exp2(tot - bc)).astype(BF16)
        for hd in range(HEADS):
            sl = slice(hd * HEAD_DIM, (hd + 1) * HEAD_DIM)
            out_ref[0, hd] = lax.dot_general(ks[:, sl], v[:, sl], _TN, preferred_element_type=F32)


def _ctx_call(ctx, mod3, nw, wi, lb):
    bsz, n, _ = ctx.shape
    st_shape = jax.ShapeDtypeStruct((bsz, HEADS, HEAD_DIM, HEAD_DIM), F32)
    st_spec = pl.BlockSpec((1, HEADS, HEAD_DIM, HEAD_DIM), lambda b: (b, 0, 0, 0))
    return pl.pallas_call(
        _ctx_kernel,
        grid=(bsz,),
        in_specs=[
            pl.BlockSpec((1, n, D_MODEL), lambda b: (b, 0, 0)),
            pl.BlockSpec((1, 3, D_MODEL), lambda b: (bsz, 0, 0)),
            pl.BlockSpec((ROW_PAD, D_MODEL), lambda b: (0, 0)),
            _w_in_cols(COL_V), _w_in_cols(COL_ZF), _w_in_cols(COL_ZB),
            pl.BlockSpec((ROW_PAD, HG_WIDTH), lambda b: (0, 0)),
        ],
        out_specs=[st_spec, st_spec],
        out_shape=[st_shape, st_shape],
        compiler_params=pltpu.CompilerParams(dimension_semantics=("arbitrary",),
                                             vmem_limit_bytes=VMEM_LIMIT),
        name="ctx_states",
    )(ctx, mod3, nw, wi, wi, wi, lb)


def _pivot_row(i, level, reverse):
    half = NBLK >> (level + 1)
    boundary = ((i // (2 * half)) * 2 + 1) * half * BLK
    return boundary if reverse else boundary - 1


def _q_side(i, level, reverse):
    later = (i // (NBLK >> (level + 1))) % 2 == 1
    return later != reverse


def _tiles(r0, n_rows, row_tile, width, col_tile):
    return [(slice(r0 + r, r0 + r + row_tile), slice(c, c + col_tile))
            for r in range(0, n_rows, row_tile) for c in range(0, width, col_tile)]


def _scan_gates(c, z_ref, lb, bc_s, g2_s, tri2):
    for rows, cols in _tiles(c * CHUNK, CHUNK, BLK, HG_WIDTH, COL_TILE):
        g, kk = _gates(z_ref[rows, cols], lb[:, cols])
        g_hi = g.astype(BF16)
        g2_s[0, rows, cols] = g_hi
        g2_s[1, rows, cols] = (g - g_hi.astype(F32)).astype(BF16)
        z_ref[rows, cols] = kk
    rows = slice(c * CHUNK, (c + 1) * CHUNK)
    g2 = jnp.concatenate([g2_s[0, rows, :], g2_s[1, rows, :]], axis=0)
    bc_s[rows, :] = jnp.dot(tri2, g2, preferred_element_type=F32)


def _scan_prep(c, k_ref, q_ref, ops_s, opst_s, bc_s, reverse):
    r0 = c * CHUNK
    blk = lax.broadcasted_iota(jnp.int32, (NBLK, 1), 0)

    def per_block(row_of_block):
        out = None
        for i in reversed(range(NBLK)):
            if i + 1 < NBLK and row_of_block[i] == row_of_block[i + 1]:
                continue
            row = bc_s[r0 + row_of_block[i]:r0 + row_of_block[i] + 1, :]
            out = row if out is None else jnp.where(blk <= i, row, out)
        return jnp.broadcast_to(out, (NBLK, out.shape[1]))

    mid = 0.5 * (per_block([i * BLK for i in range(NBLK)])
                 + per_block([i * BLK + BLK - 1 for i in range(NBLK)]))
    tot_row = r0 if reverse else r0 + CHUNK - 1
    tot = bc_s[tot_row:tot_row + 1, :]
    scale = [jnp.exp2(mid), jnp.exp2(tot - mid)]
    for level in range(LEVELS):
        piv = per_block([_pivot_row(i, level, reverse) for i in range(NBLK)])
        scale.append(jnp.exp2(-jnp.abs(mid - piv)))
    for rows, cols in _tiles(r0, CHUNK, BLK, HG_WIDTH, COL_TILE):
        i = (rows.start - r0) // BLK
        d = bc_s[rows, cols] - mid[i:i + 1, cols]
        qd = q_ref[rows, cols].astype(F32) * jnp.exp2(d)
        kd = k_ref[rows, cols] * jnp.exp2(-d)
        ops_s[0, rows, cols] = qd.astype(BF16)
        ops_s[1, rows, cols] = kd.astype(BF16)
        ops_s[2, rows, cols] = (qd * scale[0][i:i + 1, cols]).astype(BF16)
        ops_s[3, rows, cols] = (kd * scale[1][i:i + 1, cols]).astype(BF16)
        for level in range(LEVELS):
            side = qd if _q_side(i, level, reverse) else kd
            ops_s[4 + level, rows, cols] = (side * scale[2 + level][i:i + 1, cols]).astype(BF16)
    rows = slice(r0, r0 + CHUNK)
    for hd in range(HEADS):
        sl = slice(hd * HEAD_DIM, (hd + 1) * HEAD_DIM)
        for j, n in enumerate(_KEY_OPS):
            opst_s[j, c * HEADS + hd] = ops_s[n, rows, sl].T
    return jnp.exp2(tot)


def _pair_masks(reverse):
    t = lax.broadcasted_iota(jnp.int32, (CHUNK, CHUNK), 0)
    s = lax.broadcasted_iota(jnp.int32, (CHUNK, CHUNK), 1)
    early, late = (t, s) if reverse else (s, t)
    masks = [((t // BLK) == (s // BLK)) & (early <= late)]
    for level in range(LEVELS):
        half = CHUNK >> (level + 1)
        masks.append(((late // half) == (early // half) + 1) & ((early // half) % 2 == 0))
    return masks


def _scan_mm(c, dec, v_ref, st_ref, ops_s, opst_s, masks, emit, reverse):
    r0 = c * CHUNK
    rows = slice(r0, r0 + CHUNK)
    q_blocks = [[i for i in range(NBLK) if _q_side(i, level, reverse)] for level in range(LEVELS)]
    attn = []
    for hd in range(HEADS):
        sl = slice(hd * HEAD_DIM, (hd + 1) * HEAD_DIM)
        same = jnp.dot(ops_s[0, rows, sl], opst_s[0, c * HEADS + hd], preferred_element_type=F32)
        cross = []
        for level in range(LEVELS):
            lhs = jnp.concatenate([ops_s[4 + level, r0 + i * BLK:r0 + (i + 1) * BLK, sl]
                                   for i in q_blocks[level]], axis=0)
            cross.append(jnp.dot(lhs, opst_s[1 + level, c * HEADS + hd],
                                 preferred_element_type=F32))
        blocks = []
        for i in range(NBLK):
            blk = slice(i * BLK, (i + 1) * BLK)
            a = jnp.where(masks[0][blk], same[blk], 0.0)
            for level in range(LEVELS):
                if i in q_blocks[level]:
                    j = q_blocks[level].index(i)
                    a = jnp.where(masks[1 + level][blk], cross[level][j * BLK:(j + 1) * BLK], a)
            blocks.append(a)
        attn.append(jnp.concatenate(blocks, axis=0).astype(BF16))
    for hd in range(HEADS):
        sl = slice(hd * HEAD_DIM, (hd + 1) * HEAD_DIM)
        st = st_ref[hd]
        v = v_ref[rows, sl]
        lhs = jnp.concatenate([ops_s[2, rows, sl], attn[hd]], axis=1)
        rhs = jnp.concatenate([st.astype(BF16), v], axis=0)
        emit(rows, sl, jnp.dot(lhs, rhs, preferred_element_type=F32))
        kv = lax.dot_general(ops_s[3, rows, sl], v, _TN, preferred_element_type=F32)
        dec_col = jnp.transpose(jnp.broadcast_to(dec[:, sl], (HEAD_DIM, HEAD_DIM)))
        st_ref[hd] = st * dec_col + kv


def _scan_tile(z_ref, q_ref, v_ref, lb, st_ref, ops_s, opst_s, bc_s, g2_s, emit, reverse,
               after_gates=None, before_matmuls=None, after_chunk=None):
    n_chunks = z_ref.shape[0] // CHUNK
    order = range(n_chunks - 1, -1, -1) if reverse else range(n_chunks)
    tri_b = _tri(CHUNK, reverse).astype(BF16)
    tri2 = jnp.concatenate([tri_b, tri_b], axis=1)
    masks = _pair_masks(reverse)
    for c in order:
        _scan_gates(c, z_ref, lb, bc_s, g2_s, tri2)
    if after_gates is not None:
        after_gates()
    dec = {c: _scan_prep(c, z_ref, q_ref, ops_s, opst_s, bc_s, reverse) for c in order}
    if before_matmuls is not None:
        before_matmuls()
    for c in order:
        _scan_mm(c, dec[c], v_ref, st_ref, ops_s, opst_s, masks, emit, reverse)
        if after_chunk is not None:
            after_chunk(slice(c * CHUNK, (c + 1) * CHUNK))


def _init_state(s0_ref, st_s):
    @pl.when(pl.program_id(1) == 0)
    def _():
        st_s[...] = s0_ref[0]


_SCAN_SCRATCH = [
    pltpu.VMEM((HEADS, HEAD_DIM, HEAD_DIM), F32),
    pltpu.VMEM((N_OPS, TILE_T, HG_WIDTH), BF16),
    pltpu.VMEM((len(_KEY_OPS), TILE_T // CHUNK * HEADS, HEAD_DIM, CHUNK), BF16),
    pltpu.VMEM((TILE_T, HG_WIDTH), F32),
    pltpu.VMEM((TILE_T, HG_WIDTH), F32),
    pltpu.VMEM((2, TILE_T, HG_WIDTH), BF16),
]


def _bwd_kernel(x_ref, mod_ref, nw_ref, wq_ref, wv_ref, wz_ref, lb_ref, s0_ref,
                q_ref, v_ref, ob_ref, st_s, ops_s, opst_s, bc_s, z_s, g2_s):
    _init_state(s0_ref, st_s)
    h = _modulated_norm(x_ref[0], nw_ref[0:1, :], mod_ref)
    z_s[...] = jnp.dot(h, wz_ref[...], preferred_element_type=F32)
    q_ref[0] = jnp.dot(h, wq_ref[...], preferred_element_type=F32).astype(BF16)

    def value_proj():
        v_ref[0] = jnp.dot(h, wv_ref[...], preferred_element_type=F32).astype(BF16)

    def emit(rows, sl, o):
        ob_ref[0, rows, sl] = o.astype(BF16)

    _scan_tile(z_s, q_ref.at[0], v_ref.at[0], lb_ref[1:2, :], st_s, ops_s, opst_s, bc_s, g2_s, emit,
               reverse=True,
               after_gates=value_proj)


def _bwd_call(x, mod3, nw, wi, lb, sb):
    bsz, seq, _ = x.shape
    nt = seq // TILE_T
    tile = lambda b, t: (b, nt - 1 - t, 0)
    act = jax.ShapeDtypeStruct((bsz, seq, HG_WIDTH), BF16)
    act_spec = pl.BlockSpec((1, TILE_T, HG_WIDTH), tile)
    return pl.pallas_call(
        _bwd_kernel,
        grid=(bsz, nt),
        in_specs=[
            pl.BlockSpec((1, TILE_T, D_MODEL), tile),
            pl.BlockSpec((1, 3, D_MODEL), lambda b, t: (b, 0, 0)),
            pl.BlockSpec((ROW_PAD, D_MODEL), lambda b, t: (0, 0)),
            _w_in_cols(COL_Q), _w_in_cols(COL_V), _w_in_cols(COL_ZB),
            pl.BlockSpec((ROW_PAD, HG_WIDTH), lambda b, t: (0, 0)),
            pl.BlockSpec((1, HEADS, HEAD_DIM, HEAD_DIM), lambda b, t: (b, 0, 0, 0)),
        ],
        out_specs=[act_spec, act_spec, act_spec],
        out_shape=[act, act, act],
        scratch_shapes=_SCAN_SCRATCH,
        compiler_params=pltpu.CompilerParams(dimension_semantics=("arbitrary", "arbitrary"),
                                             vmem_limit_bytes=VMEM_LIMIT),
        name="bwd_scan",
    )(x, mod3, nw, wi, wi, wi, lb, sb)


def _fwd_kernel(x_ref, mod_ref, nw_ref, wz_ref, wg_ref, wc0_ref, wc1_ref, wc2_ref, wc3_ref,
                lb_ref, s0_ref, q_ref, v_ref, ob_ref, onw_ref, cw_ref, wo_ref, fnw_ref, out_ref,
                st_s, ops_s, opst_s, bc_s, z_s, g2_s, mix_s, cv_s):
    _init_state(s0_ref, st_s)
    g_s = z_s
    o_s = bc_s
    h = _modulated_norm(x_ref[0], nw_ref[0:1, :], mod_ref)

    z_s[...] = jnp.dot(h, wz_ref[...], preferred_element_type=F32)

    def conv_proj(j, wc_ref):
        cv_s[:, j * CONV_WIDTH:(j + 1) * CONV_WIDTH] = jnp.dot(h, wc_ref[...],
                                                               preferred_element_type=F32)

    conv_proj(0, wc0_ref)

    def conv_proj_rest():
        for j, wc_ref in enumerate((wc1_ref, wc2_ref, wc3_ref), start=1):
            conv_proj(j, wc_ref)

    def conv_branch():
        g_s[...] = jnp.dot(h, wg_ref[...], preferred_element_type=F32)
        n_tok = cv_s.shape[0]
        u = cv_s[:, 2 * CONV_WIDTH:3 * CONV_WIDTH] * cv_s[:, 0:CONV_WIDTH]
        col = lax.broadcasted_iota(jnp.int32, (n_tok, 1), 0) % GRID_W
        u_prev = jnp.where(col == 0, 0.0, pltpu.roll(u, 1, axis=0))
        u_next = jnp.where(col == GRID_W - 1, 0.0, pltpu.roll(u, n_tok - 1, axis=0))
        conv = cw_ref[0:1, :] * u_prev + cw_ref[1:2, :] * u + cw_ref[2:3, :] * u_next
        cg = cv_s[:, 3 * CONV_WIDTH:4 * CONV_WIDTH]
        y_cv = cv_s[:, CONV_WIDTH:2 * CONV_WIDTH] * conv * (cg * _sigmoid(cg))
        mix_s[:, HG_WIDTH:] = y_cv.astype(BF16)

    def emit(rows, sl, o):
        o_s[rows, sl] = o + ob_ref[0, rows, sl].astype(F32)

    def gated_readout(rows):
        onw = onw_ref[0:1, :]
        for hd in range(HEADS):
            sl = slice(hd * HEAD_DIM, (hd + 1) * HEAD_DIM)
            o = o_s[rows, sl]
            g = g_s[rows, sl]
            y = (o * _rms_rows(o) * onw[:, sl]) * (g * _sigmoid(g))
            mix_s[rows, sl] = y.astype(BF16)

    _scan_tile(z_s, q_ref.at[0], v_ref.at[0], lb_ref[0:1, :], st_s, ops_s, opst_s, bc_s, g2_s, emit,
               reverse=False,
               after_gates=conv_proj_rest, before_matmuls=conv_branch, after_chunk=gated_readout)

    half = x_ref.shape[1] // 2
    for rows in (slice(0, half), slice(half, 2 * half)):
        y = jnp.dot(mix_s[rows, :], wo_ref[...], preferred_element_type=F32)
        r = x_ref[0, rows, :] + mod_ref[0, 2:3, :] * y
        out_ref[0, rows, :] = (r * _rms_rows(r)) * fnw_ref[0:1, :]


def _fwd_call(x, mod3, nw, wi, lb, sf, q, v, ob, onw, conv_w, w_o, fnw):
    bsz, seq, _ = x.shape
    nt = seq // TILE_T
    tile = lambda b, t: (b, t, 0)
    const2 = lambda b, t: (0, 0)
    act_spec = pl.BlockSpec((1, TILE_T, HG_WIDTH), tile)
    return pl.pallas_call(
        _fwd_kernel,
        grid=(bsz, nt),
        in_specs=[
            pl.BlockSpec((1, TILE_T, D_MODEL), tile),
            pl.BlockSpec((1, 3, D_MODEL), lambda b, t: (b, 0, 0)),
            pl.BlockSpec((ROW_PAD, D_MODEL), const2),
            _w_in_cols(COL_ZF), _w_in_cols(COL_GATE),
            _w_in_cols(COL_CONV), _w_in_cols(COL_CONV + 1), _w_in_cols(COL_CONV + 2),
            _w_in_cols(COL_CONV + 3),
            pl.BlockSpec((ROW_PAD, HG_WIDTH), const2),
            pl.BlockSpec((1, HEADS, HEAD_DIM, HEAD_DIM), lambda b, t: (b, 0, 0, 0)),
            act_spec, act_spec, act_spec,
            pl.BlockSpec((ROW_PAD, HG_WIDTH), const2),
            pl.BlockSpec(conv_w.shape, const2),
            pl.BlockSpec(w_o.shape, const2, pipeline_mode=pl.Buffered(1)),
            pl.BlockSpec((ROW_PAD, D_MODEL), const2),
        ],
        out_specs=pl.BlockSpec((1, TILE_T, D_MODEL), tile),
        out_shape=jax.ShapeDtypeStruct((bsz, seq, D_MODEL), F32),
        scratch_shapes=_SCAN_SCRATCH + [
            pltpu.VMEM((TILE_T, HG_WIDTH + CONV_WIDTH), BF16),
            pltpu.VMEM((TILE_T, 4 * CONV_WIDTH), F32),
        ],
        compiler_params=pltpu.CompilerParams(dimension_semantics=("arbitrary", "arbitrary"),
                                             vmem_limit_bytes=VMEM_LIMIT),
        name="fwd_scan_out",
    )(x, mod3, nw, wi, wi, wi, wi, wi, wi, lb, sf, q, v, ob, onw, conv_w, w_o, fnw)


def kernel(x, c, ctx, c_ctx, norm_w, w_ada, b_ada, w_in, hg_lb_logits, hg_onorm_w, conv_w,
           w_out, final_norm_w):
    assert w_in.shape[0] == 1, "single-layer block"
    bsz = x.shape[0]
    cc = jnp.concatenate([c, c_ctx[None, :], jnp.zeros((8 - bsz - 1, D_MODEL), F32)], axis=0)
    mod, lb = _mod_call(cc, w_ada[0], b_ada[0][None, :], hg_lb_logits.reshape(-1, HG_WIDTH))
    mod3 = mod.reshape(8, 3, D_MODEL)

    wi = w_in[0].astype(BF16)
    nw = _pad_rows(norm_w[0][None, :])

    sf, sb = _ctx_call(ctx, mod3, nw, wi, lb)
    q, v, ob = _bwd_call(x, mod3, nw, wi, lb, sb)
    onw = _pad_rows(jnp.tile(hg_onorm_w[0], HEADS)[None, :])
    return _fwd_call(x, mod3, nw, wi, lb, sf, q, v, ob, onw, conv_w[0],
                     w_out[0].astype(BF16), _pad_rows(final_norm_w[None, :]))
```

```python
import jax
import jax.numpy as jnp
from jax import lax
from jax.experimental import pallas as pl
from jax.experimental.pallas import tpu as pltpu

D_MODEL = 1024
HEADS = 8
HEAD_DIM = 128
HG_WIDTH = HEADS * HEAD_DIM
CONV_WIDTH = 1024
GRID_W = 64
EPS = 1e-6

CHUNK = 128
BLK = 16
NBLK = CHUNK // BLK
LEVELS = NBLK.bit_length() - 1
N_OPS = 4 + LEVELS
_KEY_OPS = (1,) + tuple(range(4, 4 + LEVELS))
TILE_T = 512
COL_TILE = 256
EXP_RANGE = 115.0
LOG2F_FLOOR = 2.0 * EXP_RANGE / (BLK - 1)
VMEM_CAPACITY = 64 * 1024 * 1024
VMEM_LIMIT = VMEM_CAPACITY - 4 * 1024 * 1024
ROW_PAD = 4

COL_Q, COL_V, COL_ZF, COL_ZB, COL_GATE, COL_CONV = 0, 1, 2, 3, 4, 5

F32 = jnp.float32
BF16 = jnp.bfloat16

_TN = (((0,), (0,)), ((), ()))


def _w_in_cols(group):
    return pl.BlockSpec((D_MODEL, HG_WIDTH), lambda *_: (0, group), pipeline_mode=pl.Buffered(1))


def _pad_rows(a):
    return jnp.pad(a, ((0, ROW_PAD - a.shape[0]), (0, 0)))


def _sigmoid(x):
    return 1.0 / (1.0 + jnp.exp(-x))


def _rms_rows(xf):
    return lax.rsqrt(jnp.mean(xf * xf, axis=-1, keepdims=True) + EPS)


def _tri(n, reverse):
    t = lax.broadcasted_iota(jnp.int32, (n, n), 0)
    r = lax.broadcasted_iota(jnp.int32, (n, n), 1)
    return (r >= t) if reverse else (r <= t)


def _split_bf16(a):
    hi = a.astype(BF16)
    return hi, (a - hi.astype(F32)).astype(BF16)


def _cumsum_time(g, tri2):
    return jnp.dot(tri2, jnp.concatenate(_split_bf16(g), axis=0), preferred_element_type=F32)


def _gates(z, lb):
    f = lb + (1.0 - lb) * _sigmoid(z)
    return jnp.maximum(jnp.log2(f), -LOG2F_FLOOR), 1.0 - f


def _mod_kernel(cc_ref, w_ref, b_ref, lg_ref, mod_ref, lb_ref):
    cc = cc_ref[...]
    s = cc * _sigmoid(cc)
    s_hi, s_lo = _split_bf16(s)
    w_hi, w_lo = _split_bf16(w_ref[...])
    dot = lambda a, b: jnp.dot(a, b, preferred_element_type=F32)
    mod_ref[...] = dot(s_hi, w_hi) + (dot(s_hi, w_lo) + dot(s_lo, w_hi)) + b_ref[...]
    n_rows = lg_ref.shape[0] // 2
    for d in range(2):
        lg = lg_ref[d * n_rows:(d + 1) * n_rows, :]
        e = jnp.exp(lg - jnp.max(lg, axis=0, keepdims=True))
        lb_ref[d:d + 1, :] = e[0:1, :] / jnp.sum(e, axis=0, keepdims=True)
    lb_ref[2:, :] = jnp.zeros((ROW_PAD - 2, lb_ref.shape[1]), F32)


def _mod_call(cc, w_ada, b_ada, lb_logits):
    n_out = w_ada.shape[1]
    bn = 768
    return pl.pallas_call(
        _mod_kernel,
        grid=(n_out // bn,),
        in_specs=[
            pl.BlockSpec((8, D_MODEL), lambda j: (0, 0)),
            pl.BlockSpec((D_MODEL, bn), lambda j: (0, j)),
            pl.BlockSpec((1, bn), lambda j: (0, j)),
            pl.BlockSpec(lb_logits.shape, lambda j: (0, 0)),
        ],
        out_specs=[
            pl.BlockSpec((8, bn), lambda j: (0, j)),
            pl.BlockSpec((ROW_PAD, HG_WIDTH), lambda j: (0, 0)),
        ],
        out_shape=[
            jax.ShapeDtypeStruct((8, n_out), F32),
            jax.ShapeDtypeStruct((ROW_PAD, HG_WIDTH), F32),
        ],
        compiler_params=pltpu.CompilerParams(dimension_semantics=("arbitrary",)),
        name="adaln_mod",
    )(cc, w_ada, b_ada, lb_logits)


def _modulated_norm(x, nw, mod_ref):
    shift = mod_ref[0, 0:1, :]
    scale = mod_ref[0, 1:2, :]
    return ((x * _rms_rows(x)) * (nw * (1.0 + scale)) + shift).astype(BF16)


def _ctx_kernel(x_ref, mod_ref, nw_ref, wv_ref, wzf_ref, wzb_ref, lb_ref, sf_ref, sb_ref):
    n = x_ref.shape[1]
    h = _modulated_norm(x_ref[0], nw_ref[0:1, :], mod_ref)
    v = jnp.dot(h, wv_ref[...], preferred_element_type=F32).astype(BF16)
    for d, wz_ref, out_ref in ((0, wzf_ref, sf_ref), (1, wzb_ref, sb_ref)):
        z = jnp.dot(h, wz_ref[...], preferred_element_type=F32)
        g, kk = _gates(z, lb_ref[d:d + 1, :])
        tri = _tri(n, reverse=bool(d)).astype(BF16)
        bc = _cumsum_time(g, jnp.concatenate([tri, tri], axis=1))
        tot = bc[0:1, :] if d else bc[n - 1:n, :]
        ks = (kk * jnp.exp2(tot - bc)).astype(BF16)
        for hd in range(HEADS):
            sl = slice(hd * HEAD_DIM, (hd + 1) * HEAD_DIM)
            out_ref[0, hd] = lax.dot_general(ks[:, sl], v[:, sl], _TN, preferred_element_type=F32)


def _ctx_call(ctx, mod3, nw, wi, lb):
    bsz, n, _ = ctx.shape
    st_shape = jax.ShapeDtypeStruct((bsz, HEADS, HEAD_DIM, HEAD_DIM), F32)
    st_spec = pl.BlockSpec((1, HEADS, HEAD_DIM, HEAD_DIM), lambda b: (b, 0, 0, 0))
    return pl.pallas_call(
        _ctx_kernel,
        grid=(bsz,),
        in_specs=[
            pl.BlockSpec((1, n, D_MODEL), lambda b: (b, 0, 0)),
            pl.BlockSpec((1, 3, D_MODEL), lambda b: (bsz, 0, 0)),
            pl.BlockSpec((ROW_PAD, D_MODEL), lambda b: (0, 0)),
            _w_in_cols(COL_V), _w_in_cols(COL_ZF), _w_in_cols(COL_ZB),
            pl.BlockSpec((ROW_PAD, HG_WIDTH), lambda b: (0, 0)),
        ],
        out_specs=[st_spec, st_spec],
        out_shape=[st_shape, st_shape],
        compiler_params=pltpu.CompilerParams(dimension_semantics=("arbitrary",),
                                             vmem_limit_bytes=VMEM_LIMIT),
        name="ctx_states",
    )(ctx, mod3, nw, wi, wi, wi, lb)


def _pivot_row(i, level, reverse):
    half = NBLK >> (level + 1)
    boundary = ((i // (2 * half)) * 2 + 1) * half * BLK
    return boundary if reverse else boundary - 1


def _q_side(i, level, reverse):
    later = (i // (NBLK >> (level + 1))) % 2 == 1
    return later != reverse


def _tiles(r0, n_rows, row_tile, width, col_tile):
    return [(slice(r0 + r, r0 + r + row_tile), slice(c, c + col_tile))
            for r in range(0, n_rows, row_tile) for c in range(0, width, col_tile)]


def _scan_gates(c, z_ref, lb, bc_s, g2_s, tri2):
    for rows, cols in _tiles(c * CHUNK, CHUNK, BLK, HG_WIDTH, COL_TILE):
        g, kk = _gates(z_ref[rows, cols], lb[:, cols])
        g_hi = g.astype(BF16)
        g2_s[0, rows, cols] = g_hi
        g2_s[1, rows, cols] = (g - g_hi.astype(F32)).astype(BF16)
        z_ref[rows, cols] = kk
    rows = slice(c * CHUNK, (c + 1) * CHUNK)
    g2 = jnp.concatenate([g2_s[0, rows, :], g2_s[1, rows, :]], axis=0)
    bc_s[rows, :] = jnp.dot(tri2, g2, preferred_element_type=F32)


def _scan_prep(c, k_ref, q_ref, ops_s, opst_s, bc_s, reverse):
    r0 = c * CHUNK
    blk = lax.broadcasted_iota(jnp.int32, (NBLK, 1), 0)

    def per_block(row_of_block):
        out = None
        for i in reversed(range(NBLK)):
            if i + 1 < NBLK and row_of_block[i] == row_of_block[i + 1]:
                continue
            row = bc_s[r0 + row_of_block[i]:r0 + row_of_block[i] + 1, :]
            out = row if out is None else jnp.where(blk <= i, row, out)
        return jnp.broadcast_to(out, (NBLK, out.shape[1]))

    mid = 0.5 * (per_block([i * BLK for i in range(NBLK)])
                 + per_block([i * BLK + BLK - 1 for i in range(NBLK)]))
    tot_row = r0 if reverse else r0 + CHUNK - 1
    tot = bc_s[tot_row:tot_row + 1, :]
    scale = [jnp.exp2(mid), jnp.exp2(tot - mid)]
    for level in range(LEVELS):
        piv = per_block([_pivot_row(i, level, reverse) for i in range(NBLK)])
        scale.append(jnp.exp2(-jnp.abs(mid - piv)))
    for rows, cols in _tiles(r0, CHUNK, BLK, HG_WIDTH, COL_TILE):
        i = (rows.start - r0) // BLK
        d = bc_s[rows, cols] - mid[i:i + 1, cols]
        qd = q_ref[rows, cols].astype(F32) * jnp.exp2(d)
        kd = k_ref[rows, cols] * jnp.exp2(-d)
        ops_s[0, rows, cols] = qd.astype(BF16)
        ops_s[1, rows, cols] = kd.astype(BF16)
        ops_s[2, rows, cols] = (qd * scale[0][i:i + 1, cols]).astype(BF16)
        ops_s[3, rows, cols] = (kd * scale[1][i:i + 1, cols]).astype(BF16)
        for level in range(LEVELS):
            side = qd if _q_side(i, level, reverse) else kd
            ops_s[4 + level, rows, cols] = (side * scale[2 + level][i:i + 1, cols]).astype(BF16)
    rows = slice(r0, r0 + CHUNK)
    for hd in range(HEADS):
        sl = slice(hd * HEAD_DIM, (hd + 1) * HEAD_DIM)
        for j, n in enumerate(_KEY_OPS):
            opst_s[j, c * HEADS + hd] = ops_s[n, rows, sl].T
    return jnp.exp2(tot)


def _pair_masks(reverse):
    t = lax.broadcasted_iota(jnp.int32, (CHUNK, CHUNK), 0)
    s = lax.broadcasted_iota(jnp.int32, (CHUNK, CHUNK), 1)
    early, late = (t, s) if reverse else (s, t)
    masks = [((t // BLK) == (s // BLK)) & (early <= late)]
    for level in range(LEVELS):
        half = CHUNK >> (level + 1)
        masks.append(((late // half) == (early // half) + 1) & ((early // half) % 2 == 0))
    return masks


def _scan_mm(c, dec, v_ref, st_ref, ops_s, opst_s, masks, emit, reverse):
    r0 = c * CHUNK
    rows = slice(r0, r0 + CHUNK)
    q_blocks = [[i for i in range(NBLK) if _q_side(i, level, reverse)] for level in range(LEVELS)]
    attn = []
    for hd in range(HEADS):
        sl = slice(hd * HEAD_DIM, (hd + 1) * HEAD_DIM)
        same = jnp.dot(ops_s[0, rows, sl], opst_s[0, c * HEADS + hd], preferred_element_type=F32)
        cross = []
        for level in range(LEVELS):
            lhs = jnp.concatenate([ops_s[4 + level, r0 + i * BLK:r0 + (i + 1) * BLK, sl]
                                   for i in q_blocks[level]], axis=0)
            cross.append(jnp.dot(lhs, opst_s[1 + level, c * HEADS + hd],
                                 preferred_element_type=F32))
        blocks = []
        for i in range(NBLK):
            blk = slice(i * BLK, (i + 1) * BLK)
            a = jnp.where(masks[0][blk], same[blk], 0.0)
            for level in range(LEVELS):
                if i in q_blocks[level]:
                    j = q_blocks[level].index(i)
                    a = jnp.where(masks[1 + level][blk], cross[level][j * BLK:(j + 1) * BLK], a)
            blocks.append(a)
        attn.append(jnp.concatenate(blocks, axis=0).astype(BF16))
    for hd in range(HEADS):
        sl = slice(hd * HEAD_DIM, (hd + 1) * HEAD_DIM)
        st = st_ref[hd]
        v = v_ref[rows, sl]
        lhs = jnp.concatenate([ops_s[2, rows, sl], attn[hd]], axis=1)
        rhs = jnp.concatenate([st.astype(BF16), v], axis=0)
        emit(rows, sl, jnp.dot(lhs, rhs, preferred_element_type=F32))
        kv = lax.dot_general(ops_s[3, rows, sl], v, _TN, preferred_element_type=F32)
        dec_col = jnp.transpose(jnp.broadcast_to(dec[:, sl], (HEAD_DIM, HEAD_DIM)))
        st_ref[hd] = st * dec_col + kv


def _scan_tile(z_ref, q_ref, v_ref, lb, st_ref, ops_s, opst_s, bc_s, g2_s, emit, reverse,
               after_gates=None, before_matmuls=None, after_chunk=None):
    n_chunks = z_ref.shape[0] // CHUNK
    order = range(n_chunks - 1, -1, -1) if reverse else range(n_chunks)
    tri_b = _tri(CHUNK, reverse).astype(BF16)
    tri2 = jnp.concatenate([tri_b, tri_b], axis=1)
    masks = _pair_masks(reverse)
    for c in order:
        _scan_gates(c, z_ref, lb, bc_s, g2_s, tri2)
    if after_gates is not None:
        after_gates()
    dec = {c: _scan_prep(c, z_ref, q_ref, ops_s, opst_s, bc_s, reverse) for c in order}
    if before_matmuls is not None:
        before_matmuls()
    for c in order:
        _scan_mm(c, dec[c], v_ref, st_ref, ops_s, opst_s, masks, emit, reverse)
        if after_chunk is not None:
            after_chunk(slice(c * CHUNK, (c + 1) * CHUNK))


def _init_state(s0_ref, st_s):
    @pl.when(pl.program_id(1) == 0)
    def _():
        st_s[...] = s0_ref[0]


_SCAN_SCRATCH = [
    pltpu.VMEM((HEADS, HEAD_DIM, HEAD_DIM), F32),
    pltpu.VMEM((N_OPS, TILE_T, HG_WIDTH), BF16),
    pltpu.VMEM((len(_KEY_OPS), TILE_T // CHUNK * HEADS, HEAD_DIM, CHUNK), BF16),
    pltpu.VMEM((TILE_T, HG_WIDTH), F32),
    pltpu.VMEM((TILE_T, HG_WIDTH), F32),
    pltpu.VMEM((2, TILE_T, HG_WIDTH), BF16),
]


def _bwd_kernel(x_ref, mod_ref, nw_ref, wq_ref, wv_ref, wz_ref, lb_ref, s0_ref,
                q_ref, v_ref, ob_ref, st_s, ops_s, opst_s, bc_s, z_s, g2_s):
    _init_state(s0_ref, st_s)
    h = _modulated_norm(x_ref[0], nw_ref[0:1, :], mod_ref)
    z_s[...] = jnp.dot(h, wz_ref[...], preferred_element_type=F32)
    q_ref[0] = jnp.dot(h, wq_ref[...], preferred_element_type=F32).astype(BF16)

    def value_proj():
        v_ref[0] = jnp.dot(h, wv_ref[...], preferred_element_type=F32).astype(BF16)

    def emit(rows, sl, o):
        ob_ref[0, rows, sl] = o.astype(BF16)

    _scan_tile(z_s, q_ref.at[0], v_ref.at[0], lb_ref[1:2, :], st_s, ops_s, opst_s, bc_s, g2_s, emit,
               reverse=True,
               after_gates=value_proj)


def _bwd_call(x, mod3, nw, wi, lb, sb):
    bsz, seq, _ = x.shape
    nt = seq // TILE_T
    tile = lambda b, t: (b, nt - 1 - t, 0)
    act = jax.ShapeDtypeStruct((bsz, seq, HG_WIDTH), BF16)
    act_spec = pl.BlockSpec((1, TILE_T, HG_WIDTH), tile)
    return pl.pallas_call(
        _bwd_kernel,
        grid=(bsz, nt),
        in_specs=[
            pl.BlockSpec((1, TILE_T, D_MODEL), tile),
            pl.BlockSpec((1, 3, D_MODEL), lambda b, t: (b, 0, 0)),
            pl.BlockSpec((ROW_PAD, D_MODEL), lambda b, t: (0, 0)),
            _w_in_cols(COL_Q), _w_in_cols(COL_V), _w_in_cols(COL_ZB),
            pl.BlockSpec((ROW_PAD, HG_WIDTH), lambda b, t: (0, 0)),
            pl.BlockSpec((1, HEADS, HEAD_DIM, HEAD_DIM), lambda b, t: (b, 0, 0, 0)),
        ],
        out_specs=[act_spec, act_spec, act_spec],
        out_shape=[act, act, act],
        scratch_shapes=_SCAN_SCRATCH,
        compiler_params=pltpu.CompilerParams(dimension_semantics=("arbitrary", "arbitrary"),
                                             vmem_limit_bytes=VMEM_LIMIT),
        name="bwd_scan",
    )(x, mod3, nw, wi, wi, wi, lb, sb)


def _fwd_kernel(x_ref, mod_ref, nw_ref, wz_ref, wg_ref, wc0_ref, wc1_ref, wc2_ref, wc3_ref,
                lb_ref, s0_ref, q_ref, v_ref, ob_ref, onw_ref, cw_ref, wo_ref, fnw_ref, out_ref,
                st_s, ops_s, opst_s, bc_s, z_s, g2_s, mix_s, cv_s):
    _init_state(s0_ref, st_s)
    g_s = z_s
    o_s = bc_s
    h = _modulated_norm(x_ref[0], nw_ref[0:1, :], mod_ref)

    z_s[...] = jnp.dot(h, wz_ref[...], preferred_element_type=F32)

    def conv_proj(j, wc_ref):
        cv_s[:, j * CONV_WIDTH:(j + 1) * CONV_WIDTH] = jnp.dot(h, wc_ref[...],
                                                               preferred_element_type=F32)

    conv_proj(0, wc0_ref)

    def conv_proj_rest():
        for j, wc_ref in enumerate((wc1_ref, wc2_ref, wc3_ref), start=1):
            conv_proj(j, wc_ref)

    def conv_branch():
        g_s[...] = jnp.dot(h, wg_ref[...], preferred_element_type=F32)
        n_tok = cv_s.shape[0]
        u = cv_s[:, 2 * CONV_WIDTH:3 * CONV_WIDTH] * cv_s[:, 0:CONV_WIDTH]
        col = lax.broadcasted_iota(jnp.int32, (n_tok, 1), 0) % GRID_W
        u_prev = jnp.where(col == 0, 0.0, pltpu.roll(u, 1, axis=0))
        u_next = jnp.where(col == GRID_W - 1, 0.0, pltpu.roll(u, n_tok - 1, axis=0))
        conv = cw_ref[0:1, :] * u_prev + cw_ref[1:2, :] * u + cw_ref[2:3, :] * u_next
        cg = cv_s[:, 3 * CONV_WIDTH:4 * CONV_WIDTH]
        y_cv = cv_s[:, CONV_WIDTH:2 * CONV_WIDTH] * conv * (cg * _sigmoid(cg))
        mix_s[:, HG_WIDTH:] = y_cv.astype(BF16)

    def emit(rows, sl, o):
        o_s[rows, sl] = o + ob_ref[0, rows, sl].astype(F32)

    def gated_readout(rows):
        onw = onw_ref[0:1, :]
        for hd in range(HEADS):
            sl = slice(hd * HEAD_DIM, (hd + 1) * HEAD_DIM)
            o = o_s[rows, sl]
            g = g_s[rows, sl]
            y = (o * _rms_rows(o) * onw[:, sl]) * (g * _sigmoid(g))
            mix_s[rows, sl] = y.astype(BF16)

    _scan_tile(z_s, q_ref.at[0], v_ref.at[0], lb_ref[0:1, :], st_s, ops_s, opst_s, bc_s, g2_s, emit,
               reverse=False,
               after_gates=conv_proj_rest, before_matmuls=conv_branch, after_chunk=gated_readout)

    half = x_ref.shape[1] // 2
    for rows in (slice(0, half), slice(half, 2 * half)):
        y = jnp.dot(mix_s[rows, :], wo_ref[...], preferred_element_type=F32)
        r = x_ref[0, rows, :] + mod_ref[0, 2:3, :] * y
        out_ref[0, rows, :] = (r * _rms_rows(r)) * fnw_ref[0:1, :]


def _fwd_call(x, mod3, nw, wi, lb, sf, q, v, ob, onw, conv_w, w_o, fnw):
    bsz, seq, _ = x.shape
    nt = seq // TILE_T
    tile = lambda b, t: (b, t, 0)
    const2 = lambda b, t: (0, 0)
    act_spec = pl.BlockSpec((1, TILE_T, HG_WIDTH), tile)
    return pl.pallas_call(
        _fwd_kernel,
        grid=(bsz, nt),
        in_specs=[
            pl.BlockSpec((1, TILE_T, D_MODEL), tile),
            pl.BlockSpec((1, 3, D_MODEL), lambda b, t: (b, 0, 0)),
            pl.BlockSpec((ROW_PAD, D_MODEL), const2),
            _w_in_cols(COL_ZF), _w_in_cols(COL_GATE),
            _w_in_cols(COL_CONV), _w_in_cols(COL_CONV + 1), _w_in_cols(COL_CONV + 2),
            _w_in_cols(COL_CONV + 3),
            pl.BlockSpec((ROW_PAD, HG_WIDTH), const2),
            pl.BlockSpec((1, HEADS, HEAD_DIM, HEAD_DIM), lambda b, t: (b, 0, 0, 0)),
            act_spec, act_spec, act_spec,
            pl.BlockSpec((ROW_PAD, HG_WIDTH), const2),
            pl.BlockSpec(conv_w.shape, const2),
            pl.BlockSpec(w_o.shape, const2, pipeline_mode=pl.Buffered(1)),
            pl.BlockSpec((ROW_PAD, D_MODEL), const2),
        ],
        out_specs=pl.BlockSpec((1, TILE_T, D_MODEL), tile),
        out_shape=jax.ShapeDtypeStruct((bsz, seq, D_MODEL), F32),
        scratch_shapes=_SCAN_SCRATCH + [
            pltpu.VMEM((TILE_T, HG_WIDTH + CONV_WIDTH), BF16),
            pltpu.VMEM((TILE_T, 4 * CONV_WIDTH), F32),
        ],
        compiler_params=pltpu.CompilerParams(dimension_semantics=("arbitrary", "arbitrary"),
                                             vmem_limit_bytes=VMEM_LIMIT),
        name="fwd_scan_out",
    )(x, mod3, nw, wi, wi, wi, wi, wi, wi, lb, sf, q, v, ob, onw, conv_w, w_o, fnw)


def kernel(x, c, ctx, c_ctx, norm_w, w_ada, b_ada, w_in, hg_lb_logits, hg_onorm_w, conv_w,
           w_out, final_norm_w):
    assert w_in.shape[0] == 1, "single-layer block"
    bsz = x.shape[0]
    cc = jnp.concatenate([c, c_ctx[None, :], jnp.zeros((8 - bsz - 1, D_MODEL), F32)], axis=0)
    mod, lb = _mod_call(cc, w_ada[0], b_ada[0][None, :], hg_lb_logits.reshape(-1, HG_WIDTH))
    mod3 = mod.reshape(8, 3, D_MODEL)

    wi = w_in[0].astype(BF16)
    nw = _pad_rows(norm_w[0][None, :])

    sf, sb = _ctx_call(ctx, mod3, nw, wi, lb)
    q, v, ob = _bwd_call(x, mod3, nw, wi, lb, sb)
    onw = _pad_rows(jnp.tile(hg_onorm_w[0], HEADS)[None, :])
    return _fwd_call(x, mod3, nw, wi, lb, sf, q, v, ob, onw, conv_w[0],
                     w_out[0].astype(BF16), _pad_rows(final_norm_w[None, :]))
```

```python
import functools

import jax
import jax.numpy as jnp
from jax import lax
from jax.experimental import pallas as pl
from jax.experimental.pallas import tpu as pltpu

D_MODEL = 1024
HEADS = 8
HEAD_DIM = 128
HG_WIDTH = HEADS * HEAD_DIM
CONV_WIDTH = 1024
GRID_W = 64
EPS = 1e-6

CHUNK = 128
BLK = 16
NBLK = CHUNK // BLK
LEVELS = NBLK.bit_length() - 1
N_OPS = 4 + LEVELS
_KEY_OPS = (1,) + tuple(range(4, 4 + LEVELS))
TILE_T = 512
COL_TILE = 256
EXP_RANGE = 115.0
LOG2F_FLOOR = 2.0 * EXP_RANGE / (BLK - 1)
VMEM_CAPACITY = 64 * 1024 * 1024
VMEM_LIMIT = VMEM_CAPACITY - 4 * 1024 * 1024
ROW_PAD = 4

COL_Q, COL_V, COL_ZF, COL_ZB, COL_GATE, COL_CONV = 0, 1, 2, 3, 4, 5
HALVED_GROUPS = (COL_ZF, COL_ZB, COL_GATE, COL_CONV + 3)

F32 = jnp.float32
BF16 = jnp.bfloat16

_TN = (((0,), (0,)), ((), ()))


def _w_in_cols(group):
    return pl.BlockSpec((D_MODEL, HG_WIDTH), lambda *_: (0, group), pipeline_mode=pl.Buffered(1))


def _pad_rows(a):
    return jnp.pad(a, ((0, ROW_PAD - a.shape[0]), (0, 0)))


def _silu_of_half(hx):
    return hx + hx * jnp.tanh(hx)


def _rms_rows(xf):
    return lax.rsqrt(jnp.mean(xf * xf, axis=-1, keepdims=True) + EPS)


def _tri(n, reverse):
    t = lax.broadcasted_iota(jnp.int32, (n, n), 0)
    r = lax.broadcasted_iota(jnp.int32, (n, n), 1)
    return (r >= t) if reverse else (r <= t)


def _split_bf16(a):
    hi = a.astype(BF16)
    return hi, (a - hi.astype(F32)).astype(BF16)


def _cumsum_time(g, tri2):
    return jnp.dot(tri2, jnp.concatenate(_split_bf16(g), axis=0), preferred_element_type=F32)


def _gates(zh, lb):
    b = 0.5 * (1.0 - lb)
    bt = b * jnp.tanh(zh)
    return jnp.maximum(jnp.log2((1.0 - b) + bt), -LOG2F_FLOOR), b - bt


def _mod_kernel(cc_ref, w_ref, b_ref, lg_ref, mod_ref, lb_ref):
    cc = cc_ref[...]
    s = _silu_of_half(0.5 * cc)
    s_hi, s_lo = _split_bf16(s)
    w_hi, w_lo = _split_bf16(w_ref[...])
    dot = lambda a, b: jnp.dot(a, b, preferred_element_type=F32)
    mod_ref[...] = dot(s_hi, w_hi) + (dot(s_hi, w_lo) + dot(s_lo, w_hi)) + b_ref[...]
    n_rows = lg_ref.shape[0] // 2
    for d in range(2):
        lg = lg_ref[d * n_rows:(d + 1) * n_rows, :]
        e = jnp.exp(lg - jnp.max(lg, axis=0, keepdims=True))
        lb_ref[d:d + 1, :] = e[0:1, :] / jnp.sum(e, axis=0, keepdims=True)
    lb_ref[2:, :] = jnp.zeros((ROW_PAD - 2, lb_ref.shape[1]), F32)


def _mod_call(cc, w_ada, b_ada, lb_logits):
    n_out = w_ada.shape[1]
    bn = 768
    return pl.pallas_call(
        _mod_kernel,
        grid=(n_out // bn,),
        in_specs=[
            pl.BlockSpec((8, D_MODEL), lambda j: (0, 0)),
            pl.BlockSpec((D_MODEL, bn), lambda j: (0, j)),
            pl.BlockSpec((1, bn), lambda j: (0, j)),
            pl.BlockSpec(lb_logits.shape, lambda j: (0, 0)),
        ],
        out_specs=[
            pl.BlockSpec((8, bn), lambda j: (0, j)),
            pl.BlockSpec((ROW_PAD, HG_WIDTH), lambda j: (0, 0)),
        ],
        out_shape=[
            jax.ShapeDtypeStruct((8, n_out), F32),
            jax.ShapeDtypeStruct((ROW_PAD, HG_WIDTH), F32),
        ],
        compiler_params=pltpu.CompilerParams(dimension_semantics=("arbitrary",)),
        name="adaln_mod",
    )(cc, w_ada, b_ada, lb_logits)


def _modulated_norm(x, nw, mod_ref):
    shift = mod_ref[0, 0:1, :]
    scale = mod_ref[0, 1:2, :]
    return ((x * _rms_rows(x)) * (nw * (1.0 + scale)) + shift).astype(BF16)


def _ctx_kernel(x_ref, mod_ref, nw_ref, wv_ref, wzf_ref, wzb_ref, lb_ref, sf_ref, sb_ref):
    n = x_ref.shape[1]
    h = _modulated_norm(x_ref[0], nw_ref[0:1, :], mod_ref)
    v = jnp.dot(h, wv_ref[...], preferred_element_type=F32).astype(BF16)
    for d, wz_ref, out_ref in ((0, wzf_ref, sf_ref), (1, wzb_ref, sb_ref)):
        z = jnp.dot(h, wz_ref[...], preferred_element_type=F32)
        g, kk = _gates(z, lb_ref[d:d + 1, :])
        tri = _tri(n, reverse=bool(d)).astype(BF16)
        bc = _cumsum_time(g, jnp.concatenate([tri, tri], axis=1))
        tot = bc[0:1, :] if d else bc[n - 1:n, :]
        ks = (kk * jnp.exp2(tot - bc)).astype(BF16)
        for hd in range(HEADS):
            sl = slice(hd * HEAD_DIM, (hd + 1) * HEAD_DIM)
            out_ref[0, hd] = lax.dot_general(ks[:, sl], v[:, sl], _TN, preferred_element_type=F32)


def _ctx_call(ctx, mod3, nw, wi, lb):
    bsz, n, _ = ctx.shape
    st_shape = jax.ShapeDtypeStruct((bsz, HEADS, HEAD_DIM, HEAD_DIM), F32)
    st_spec = pl.BlockSpec((1, HEADS, HEAD_DIM, HEAD_DIM), lambda b: (b, 0, 0, 0))
    return pl.pallas_call(
        _ctx_kernel,
        grid=(bsz,),
        in_specs=[
            pl.BlockSpec((1, n, D_MODEL), lambda b: (b, 0, 0)),
            pl.BlockSpec((1, 3, D_MODEL), lambda b: (bsz, 0, 0)),
            pl.BlockSpec((ROW_PAD, D_MODEL), lambda b: (0, 0)),
            _w_in_cols(COL_V), _w_in_cols(COL_ZF), _w_in_cols(COL_ZB),
            pl.BlockSpec((ROW_PAD, HG_WIDTH), lambda b: (0, 0)),
        ],
        out_specs=[st_spec, st_spec],
        out_shape=[st_shape, st_shape],
        compiler_params=pltpu.CompilerParams(dimension_semantics=("arbitrary",),
                                             vmem_limit_bytes=VMEM_LIMIT),
        name="ctx_states",
    )(ctx, mod3, nw, wi, wi, wi, lb)


def _pivot_row(i, level, reverse):
    half = NBLK >> (level + 1)
    boundary = ((i // (2 * half)) * 2 + 1) * half * BLK
    return boundary if reverse else boundary - 1


def _q_side(i, level, reverse):
    later = (i // (NBLK >> (level + 1))) % 2 == 1
    return later != reverse


def _tiles(r0, n_rows, row_tile, width, col_tile):
    return [(slice(r0 + r, r0 + r + row_tile), slice(c, c + col_tile))
            for r in range(0, n_rows, row_tile) for c in range(0, width, col_tile)]


def _scan_gates(c, z_ref, lb, bc_s, g2_s, tri2):
    for rows, cols in _tiles(c * CHUNK, CHUNK, BLK, HG_WIDTH, COL_TILE):
        g, kk = _gates(z_ref[rows, cols], lb[:, cols])
        g_hi = g.astype(BF16)
        g2_s[0, rows, cols] = g_hi
        g2_s[1, rows, cols] = (g - g_hi.astype(F32)).astype(BF16)
        z_ref[rows, cols] = kk
    rows = slice(c * CHUNK, (c + 1) * CHUNK)
    g2 = jnp.concatenate([g2_s[0, rows, :], g2_s[1, rows, :]], axis=0)
    bc_s[rows, :] = jnp.dot(tri2, g2, preferred_element_type=F32)


def _scan_prep(c, k_ref, q_ref, ops_s, opst_s, bc_s, reverse):
    r0 = c * CHUNK
    blk = lax.broadcasted_iota(jnp.int32, (NBLK, 1), 0)

    def per_block(row_of_block):
        out = None
        for i in reversed(range(NBLK)):
            if i + 1 < NBLK and row_of_block[i] == row_of_block[i + 1]:
                continue
            row = bc_s[r0 + row_of_block[i]:r0 + row_of_block[i] + 1, :]
            out = row if out is None else jnp.where(blk <= i, row, out)
        return jnp.broadcast_to(out, (NBLK, out.shape[1]))

    mid = 0.5 * (per_block([i * BLK for i in range(NBLK)])
                 + per_block([i * BLK + BLK - 1 for i in range(NBLK)]))
    tot_row = r0 if reverse else r0 + CHUNK - 1
    tot = bc_s[tot_row:tot_row + 1, :]
    scale = [jnp.exp2(mid), jnp.exp2(tot - mid)]
    for level in range(LEVELS):
        piv = per_block([_pivot_row(i, level, reverse) for i in range(NBLK)])
        scale.append(jnp.exp2(-jnp.abs(mid - piv)))
    for rows, cols in _tiles(r0, CHUNK, BLK, HG_WIDTH, COL_TILE):
        i = (rows.start - r0) // BLK
        d = bc_s[rows, cols] - mid[i:i + 1, cols]
        qd = q_ref[rows, cols].astype(F32) * jnp.exp2(d)
        kd = k_ref[rows, cols] * jnp.exp2(-d)
        ops_s[0, rows, cols] = qd.astype(BF16)
        ops_s[1, rows, cols] = kd.astype(BF16)
        ops_s[2, rows, cols] = (qd * scale[0][i:i + 1, cols]).astype(BF16)
        ops_s[3, rows, cols] = (kd * scale[1][i:i + 1, cols]).astype(BF16)
        for level in range(LEVELS):
            side = qd if _q_side(i, level, reverse) else kd
            ops_s[4 + level, rows, cols] = (side * scale[2 + level][i:i + 1, cols]).astype(BF16)
    rows = slice(r0, r0 + CHUNK)
    for hd in range(HEADS):
        sl = slice(hd * HEAD_DIM, (hd + 1) * HEAD_DIM)
        for j, n in enumerate(_KEY_OPS):
            opst_s[j, c * HEADS + hd] = ops_s[n, rows, sl].T
    return jnp.exp2(tot)


def _pair_masks(reverse):
    t = lax.broadcasted_iota(jnp.int32, (CHUNK, CHUNK), 0)
    s = lax.broadcasted_iota(jnp.int32, (CHUNK, CHUNK), 1)
    early, late = (t, s) if reverse else (s, t)
    masks = [((t // BLK) == (s // BLK)) & (early <= late)]
    for level in range(LEVELS):
        half = CHUNK >> (level + 1)
        masks.append(((late // half) == (early // half) + 1) & ((early // half) % 2 == 0))
    return masks


def _scan_mm(c, dec, v_ref, st_ref, ops_s, opst_s, masks, emit, reverse):
    r0 = c * CHUNK
    rows = slice(r0, r0 + CHUNK)
    q_blocks = [[i for i in range(NBLK) if _q_side(i, level, reverse)] for level in range(LEVELS)]
    attn = []
    for hd in range(HEADS):
        sl = slice(hd * HEAD_DIM, (hd + 1) * HEAD_DIM)
        same = jnp.dot(ops_s[0, rows, sl], opst_s[0, c * HEADS + hd], preferred_element_type=F32)
        cross = []
        for level in range(LEVELS):
            lhs = jnp.concatenate([ops_s[4 + level, r0 + i * BLK:r0 + (i + 1) * BLK, sl]
                                   for i in q_blocks[level]], axis=0)
            cross.append(jnp.dot(lhs, opst_s[1 + level, c * HEADS + hd],
                                 preferred_element_type=F32))
        blocks = []
        for i in range(NBLK):
            blk = slice(i * BLK, (i + 1) * BLK)
            a = jnp.where(masks[0][blk], same[blk], 0.0)
            for level in range(LEVELS):
                if i in q_blocks[level]:
                    j = q_blocks[level].index(i)
                    a = jnp.where(masks[1 + level][blk], cross[level][j * BLK:(j + 1) * BLK], a)
            blocks.append(a)
        attn.append(jnp.concatenate(blocks, axis=0).astype(BF16))
    for hd in range(HEADS):
        sl = slice(hd * HEAD_DIM, (hd + 1) * HEAD_DIM)
        st = st_ref[hd]
        v = v_ref[rows, sl]
        lhs = jnp.concatenate([ops_s[2, rows, sl], attn[hd]], axis=1)
        rhs = jnp.concatenate([st.astype(BF16), v], axis=0)
        emit(rows, sl, jnp.dot(lhs, rhs, preferred_element_type=F32))
        kv = lax.dot_general(ops_s[3, rows, sl], v, _TN, preferred_element_type=F32)
        dec_col = jnp.transpose(jnp.broadcast_to(dec[:, sl], (HEAD_DIM, HEAD_DIM)))
        st_ref[hd] = st * dec_col + kv


def _scan_tile(z_ref, q_ref, v_ref, lb, st_ref, ops_s, opst_s, bc_s, g2_s, emit, reverse,
               after_gates=None, before_matmuls=None, after_chunk=None):
    n_chunks = z_ref.shape[0] // CHUNK
    order = range(n_chunks - 1, -1, -1) if reverse else range(n_chunks)
    tri_b = _tri(CHUNK, reverse).astype(BF16)
    tri2 = jnp.concatenate([tri_b, tri_b], axis=1)
    masks = _pair_masks(reverse)
    for c in order:
        _scan_gates(c, z_ref, lb, bc_s, g2_s, tri2)
    if after_gates is not None:
        after_gates()
    dec = {c: _scan_prep(c, z_ref, q_ref, ops_s, opst_s, bc_s, reverse) for c in order}
    if before_matmuls is not None:
        before_matmuls()
    for c in order:
        _scan_mm(c, dec[c], v_ref, st_ref, ops_s, opst_s, masks, emit, reverse)
        if after_chunk is not None:
            after_chunk(slice(c * CHUNK, (c + 1) * CHUNK))


def _init_state(s0_ref, st_s):
    @pl.when(pl.program_id(1) == 0)
    def _():
        st_s[...] = s0_ref[0]


_SCAN_SCRATCH = [
    pltpu.VMEM((HEADS, HEAD_DIM, HEAD_DIM), F32),
    pltpu.VMEM((N_OPS, TILE_T, HG_WIDTH), BF16),
    pltpu.VMEM((len(_KEY_OPS), TILE_T // CHUNK * HEADS, HEAD_DIM, CHUNK), BF16),
    pltpu.VMEM((TILE_T, HG_WIDTH), F32),
    pltpu.VMEM((TILE_T, HG_WIDTH), F32),
    pltpu.VMEM((2, TILE_T, HG_WIDTH), BF16),
]


def _bwd_kernel(x_ref, mod_ref, nw_ref, wq_ref, wv_ref, wz_ref, lb_ref, s0_ref,
                q_ref, v_ref, ob_ref, st_s, ops_s, opst_s, bc_s, z_s, g2_s):
    _init_state(s0_ref, st_s)
    h = _modulated_norm(x_ref[0], nw_ref[0:1, :], mod_ref)
    z_s[...] = jnp.dot(h, wz_ref[...], preferred_element_type=F32)
    q_ref[0] = jnp.dot(h, wq_ref[...], preferred_element_type=F32).astype(BF16)

    def value_proj():
        v_ref[0] = jnp.dot(h, wv_ref[...], preferred_element_type=F32).astype(BF16)

    def emit(rows, sl, o):
        ob_ref[0, rows, sl] = o.astype(BF16)

    _scan_tile(z_s, q_ref.at[0], v_ref.at[0], lb_ref[1:2, :], st_s, ops_s, opst_s, bc_s, g2_s, emit,
               reverse=True,
               after_gates=value_proj)


def _bwd_call(x, mod3, nw, wi, lb, sb):
    bsz, seq, _ = x.shape
    nt = seq // TILE_T
    tile = lambda b, t: (b, nt - 1 - t, 0)
    act = jax.ShapeDtypeStruct((bsz, seq, HG_WIDTH), BF16)
    act_spec = pl.BlockSpec((1, TILE_T, HG_WIDTH), tile)
    return pl.pallas_call(
        _bwd_kernel,
        grid=(bsz, nt),
        in_specs=[
            pl.BlockSpec((1, TILE_T, D_MODEL), tile),
            pl.BlockSpec((1, 3, D_MODEL), lambda b, t: (b, 0, 0)),
            pl.BlockSpec((ROW_PAD, D_MODEL), lambda b, t: (0, 0)),
            _w_in_cols(COL_Q), _w_in_cols(COL_V), _w_in_cols(COL_ZB),
            pl.BlockSpec((ROW_PAD, HG_WIDTH), lambda b, t: (0, 0)),
            pl.BlockSpec((1, HEADS, HEAD_DIM, HEAD_DIM), lambda b, t: (b, 0, 0, 0)),
        ],
        out_specs=[act_spec, act_spec, act_spec],
        out_shape=[act, act, act],
        scratch_shapes=_SCAN_SCRATCH,
        compiler_params=pltpu.CompilerParams(dimension_semantics=("arbitrary", "arbitrary"),
                                             vmem_limit_bytes=VMEM_LIMIT),
        name="bwd_scan",
    )(x, mod3, nw, wi, wi, wi, lb, sb)


def _fwd_kernel(x_ref, mod_ref, nw_ref, wz_ref, wg_ref, wc0_ref, wc1_ref, wc2_ref, wc3_ref,
                lb_ref, s0_ref, q_ref, v_ref, ob_ref, onw_ref, cw_ref, wo_ref, fnw_ref, out_ref,
                st_s, ops_s, opst_s, bc_s, z_s, g2_s, mix_s, cv_s):
    _init_state(s0_ref, st_s)
    g_s = z_s
    o_s = bc_s
    h = _modulated_norm(x_ref[0], nw_ref[0:1, :], mod_ref)

    z_s[...] = jnp.dot(h, wz_ref[...], preferred_element_type=F32)

    def conv_proj(j, wc_ref):
        cv_s[:, j * CONV_WIDTH:(j + 1) * CONV_WIDTH] = jnp.dot(h, wc_ref[...],
                                                               preferred_element_type=F32)

    conv_proj(0, wc0_ref)

    def conv_proj_rest():
        for j, wc_ref in enumerate((wc1_ref, wc2_ref, wc3_ref), start=1):
            conv_proj(j, wc_ref)

    def conv_branch():
        g_s[...] = jnp.dot(h, wg_ref[...], preferred_element_type=F32)
        n_tok = cv_s.shape[0]
        u = cv_s[:, 2 * CONV_WIDTH:3 * CONV_WIDTH] * cv_s[:, 0:CONV_WIDTH]
        col = lax.broadcasted_iota(jnp.int32, (n_tok, 1), 0) % GRID_W
        u_prev = jnp.where(col == 0, 0.0, pltpu.roll(u, 1, axis=0))
        u_next = jnp.where(col == GRID_W - 1, 0.0, pltpu.roll(u, n_tok - 1, axis=0))
        conv = cw_ref[0:1, :] * u_prev + cw_ref[1:2, :] * u + cw_ref[2:3, :] * u_next
        cg = cv_s[:, 3 * CONV_WIDTH:4 * CONV_WIDTH]
        y_cv = cv_s[:, CONV_WIDTH:2 * CONV_WIDTH] * conv * _silu_of_half(cg)
        mix_s[:, HG_WIDTH:] = y_cv.astype(BF16)

    def emit(rows, sl, o):
        o_s[rows, sl] = o + ob_ref[0, rows, sl].astype(F32)

    def gated_readout(rows):
        onw = onw_ref[0:1, :]
        for hd in range(HEADS):
            sl = slice(hd * HEAD_DIM, (hd + 1) * HEAD_DIM)
            o = o_s[rows, sl]
            g = g_s[rows, sl]
            y = (o * _rms_rows(o) * onw[:, sl]) * _silu_of_half(g)
            mix_s[rows, sl] = y.astype(BF16)

    _scan_tile(z_s, q_ref.at[0], v_ref.at[0], lb_ref[0:1, :], st_s, ops_s, opst_s, bc_s, g2_s, emit,
               reverse=False,
               after_gates=conv_proj_rest, before_matmuls=conv_branch, after_chunk=gated_readout)

    half = x_ref.shape[1] // 2
    for rows in (slice(0, half), slice(half, 2 * half)):
        y = jnp.dot(mix_s[rows, :], wo_ref[...], preferred_element_type=F32)
        r = x_ref[0, rows, :] + mod_ref[0, 2:3, :] * y
        out_ref[0, rows, :] = (r * _rms_rows(r)) * fnw_ref[0:1, :]


def _fwd_call(x, mod3, nw, wi, lb, sf, q, v, ob, onw, conv_w, w_o, fnw):
    bsz, seq, _ = x.shape
    nt = seq // TILE_T
    tile = lambda b, t: (b, t, 0)
    const2 = lambda b, t: (0, 0)
    act_spec = pl.BlockSpec((1, TILE_T, HG_WIDTH), tile)
    return pl.pallas_call(
        _fwd_kernel,
        grid=(bsz, nt),
        in_specs=[
            pl.BlockSpec((1, TILE_T, D_MODEL), tile),
            pl.BlockSpec((1, 3, D_MODEL), lambda b, t: (b, 0, 0)),
            pl.BlockSpec((ROW_PAD, D_MODEL), const2),
            _w_in_cols(COL_ZF), _w_in_cols(COL_GATE),
            _w_in_cols(COL_CONV), _w_in_cols(COL_CONV + 1), _w_in_cols(COL_CONV + 2),
            _w_in_cols(COL_CONV + 3),
            pl.BlockSpec((ROW_PAD, HG_WIDTH), const2),
            pl.BlockSpec((1, HEADS, HEAD_DIM, HEAD_DIM), lambda b, t: (b, 0, 0, 0)),
            act_spec, act_spec, act_spec,
            pl.BlockSpec((ROW_PAD, HG_WIDTH), const2),
            pl.BlockSpec(conv_w.shape, const2),
            pl.BlockSpec(w_o.shape, const2, pipeline_mode=pl.Buffered(1)),
            pl.BlockSpec((ROW_PAD, D_MODEL), const2),
        ],
        out_specs=pl.BlockSpec((1, TILE_T, D_MODEL), tile),
        out_shape=jax.ShapeDtypeStruct((bsz, seq, D_MODEL), F32),
        scratch_shapes=_SCAN_SCRATCH + [
            pltpu.VMEM((TILE_T, HG_WIDTH + CONV_WIDTH), BF16),
            pltpu.VMEM((TILE_T, 4 * CONV_WIDTH), F32),
        ],
        compiler_params=pltpu.CompilerParams(dimension_semantics=("arbitrary", "arbitrary"),
                                             vmem_limit_bytes=VMEM_LIMIT),
        name="fwd_scan_out",
    )(x, mod3, nw, wi, wi, wi, wi, wi, wi, lb, sf, q, v, ob, onw, conv_w, w_o, fnw)


def kernel(x, c, ctx, c_ctx, norm_w, w_ada, b_ada, w_in, hg_lb_logits, hg_onorm_w, conv_w,
           w_out, final_norm_w):
    assert w_in.shape[0] == 1, "single-layer block"
    bsz = x.shape[0]
    cc = jnp.concatenate([c, c_ctx[None, :], jnp.zeros((8 - bsz - 1, D_MODEL), F32)], axis=0)
    mod, lb = _mod_call(cc, w_ada[0], b_ada[0][None, :], hg_lb_logits.reshape(-1, HG_WIDTH))
    mod3 = mod.reshape(8, 3, D_MODEL)

    group = jnp.arange(w_in.shape[2]) // HG_WIDTH
    halved = functools.reduce(jnp.logical_or, [group == g for g in HALVED_GROUPS])
    wi = (w_in[0] * jnp.where(halved, 0.5, 1.0)).astype(BF16)
    nw = _pad_rows(norm_w[0][None, :])

    sf, sb = _ctx_call(ctx, mod3, nw, wi, lb)
    q, v, ob = _bwd_call(x, mod3, nw, wi, lb, sb)
    onw = _pad_rows(jnp.tile(hg_onorm_w[0], HEADS)[None, :])
    return _fwd_call(x, mod3, nw, wi, lb, sf, q, v, ob, onw, conv_w[0],
                     w_out[0].astype(BF16), _pad_rows(final_norm_w[None, :]))
```

```python
import jax
import jax.numpy as jnp
from jax import lax
from jax.experimental import pallas as pl
from jax.experimental.pallas import tpu as pltpu

D_MODEL = 1024
HEADS = 8
HEAD_DIM = 128
HG_WIDTH = HEADS * HEAD_DIM
CONV_WIDTH = 1024
GRID_W = 64
EPS = 1e-6

CHUNK = 128
BLK = 16
NBLK = CHUNK // BLK
LEVELS = NBLK.bit_length() - 1
N_OPS = 4 + LEVELS
_KEY_OPS = (1,) + tuple(range(4, 4 + LEVELS))
TILE_T = 512
COL_TILE = 256
EXP_RANGE = 115.0
LOG2F_FLOOR = 2.0 * EXP_RANGE / (BLK - 1)
VMEM_CAPACITY = 64 * 1024 * 1024
VMEM_LIMIT = VMEM_CAPACITY - 4 * 1024 * 1024
ROW_PAD = 4

COL_Q, COL_V, COL_ZF, COL_ZB, COL_GATE, COL_CONV = 0, 1, 2, 3, 4, 5
N_REST_COLS = 5 * HG_WIDTH
STAGE_ROWS = 32

F32 = jnp.float32
BF16 = jnp.bfloat16

_TN = (((0,), (0,)), ((), ()))


def _w_in_cols(group):
    return pl.BlockSpec((D_MODEL, HG_WIDTH), lambda *_: (0, group), pipeline_mode=pl.Buffered(1))


def _pad_rows(a):
    return jnp.pad(a, ((0, ROW_PAD - a.shape[0]), (0, 0)))


def _silu(x):
    hx = 0.5 * x
    return hx + hx * jnp.tanh(hx)


def _rms_rows(xf):
    return lax.rsqrt(jnp.mean(xf * xf, axis=-1, keepdims=True) + EPS)


def _tri(n, reverse):
    t = lax.broadcasted_iota(jnp.int32, (n, n), 0)
    r = lax.broadcasted_iota(jnp.int32, (n, n), 1)
    return (r >= t) if reverse else (r <= t)


def _split_bf16(a):
    hi = a.astype(BF16)
    return hi, (a - hi.astype(F32)).astype(BF16)


def _cumsum_time(g, tri2):
    return jnp.dot(tri2, jnp.concatenate(_split_bf16(g), axis=0), preferred_element_type=F32)


def _gates(z, lb):
    b = 0.5 * (1.0 - lb)
    bt = b * jnp.tanh(0.5 * z)
    return jnp.maximum(jnp.log2((1.0 - b) + bt), -LOG2F_FLOOR), b - bt


def _mod_kernel(cc_ref, w_ref, b_ref, lg_ref, mod_ref, lb_ref):
    cc = cc_ref[...]
    s = _silu(cc)
    s_hi, s_lo = _split_bf16(s)
    w_hi, w_lo = _split_bf16(w_ref[...])
    dot = lambda a, b: jnp.dot(a, b, preferred_element_type=F32)
    mod_ref[...] = dot(s_hi, w_hi) + (dot(s_hi, w_lo) + dot(s_lo, w_hi)) + b_ref[...]
    n_rows = lg_ref.shape[0] // 2
    for d in range(2):
        lg = lg_ref[d * n_rows:(d + 1) * n_rows, :]
        e = jnp.exp(lg - jnp.max(lg, axis=0, keepdims=True))
        lb_ref[d:d + 1, :] = e[0:1, :] / jnp.sum(e, axis=0, keepdims=True)
    lb_ref[2:, :] = jnp.zeros((ROW_PAD - 2, lb_ref.shape[1]), F32)


def _mod_call(cc, w_ada, b_ada, lb_logits):
    n_out = w_ada.shape[1]
    bn = 768
    return pl.pallas_call(
        _mod_kernel,
        grid=(n_out // bn,),
        in_specs=[
            pl.BlockSpec((8, D_MODEL), lambda j: (0, 0)),
            pl.BlockSpec((D_MODEL, bn), lambda j: (0, j)),
            pl.BlockSpec((1, bn), lambda j: (0, j)),
            pl.BlockSpec(lb_logits.shape, lambda j: (0, 0)),
        ],
        out_specs=[
            pl.BlockSpec((8, bn), lambda j: (0, j)),
            pl.BlockSpec((ROW_PAD, HG_WIDTH), lambda j: (0, 0)),
        ],
        out_shape=[
            jax.ShapeDtypeStruct((8, n_out), F32),
            jax.ShapeDtypeStruct((ROW_PAD, HG_WIDTH), F32),
        ],
        compiler_params=pltpu.CompilerParams(dimension_semantics=("arbitrary",)),
        name="adaln_mod",
    )(cc, w_ada, b_ada, lb_logits)


def _modulated_norm(x, nw, mod_ref):
    shift = mod_ref[0, 0:1, :]
    scale = mod_ref[0, 1:2, :]
    return ((x * _rms_rows(x)) * (nw * (1.0 + scale)) + shift).astype(BF16)


def _ctx_kernel(x_ref, mod_ref, nw_ref, wv_ref, wzf_ref, wzb_ref, lb_ref, sf_ref, sb_ref):
    n = x_ref.shape[1]
    h = _modulated_norm(x_ref[0], nw_ref[0:1, :], mod_ref)
    v = jnp.dot(h, wv_ref[...], preferred_element_type=F32).astype(BF16)
    for d, wz_ref, out_ref in ((0, wzf_ref, sf_ref), (1, wzb_ref, sb_ref)):
        z = jnp.dot(h, wz_ref[...], preferred_element_type=F32)
        g, kk = _gates(z, lb_ref[d:d + 1, :])
        tri = _tri(n, reverse=bool(d)).astype(BF16)
        bc = _cumsum_time(g, jnp.concatenate([tri, tri], axis=1))
        tot = bc[0:1, :] if d else bc[n - 1:n, :]
        ks = (kk * jnp.exp2(tot - bc)).astype(BF16)
        for hd in range(HEADS):
            sl = slice(hd * HEAD_DIM, (hd + 1) * HEAD_DIM)
            out_ref[0, hd] = lax.dot_general(ks[:, sl], v[:, sl], _TN, preferred_element_type=F32)


def _ctx_call(ctx, mod3, nw, wi, lb):
    bsz, n, _ = ctx.shape
    st_shape = jax.ShapeDtypeStruct((bsz, HEADS, HEAD_DIM, HEAD_DIM), F32)
    st_spec = pl.BlockSpec((1, HEADS, HEAD_DIM, HEAD_DIM), lambda b: (b, 0, 0, 0))
    return pl.pallas_call(
        _ctx_kernel,
        grid=(bsz,),
        in_specs=[
            pl.BlockSpec((1, n, D_MODEL), lambda b: (b, 0, 0)),
            pl.BlockSpec((1, 3, D_MODEL), lambda b: (bsz, 0, 0)),
            pl.BlockSpec((ROW_PAD, D_MODEL), lambda b: (0, 0)),
            _w_in_cols(COL_V), _w_in_cols(COL_ZF), _w_in_cols(COL_ZB),
            pl.BlockSpec((ROW_PAD, HG_WIDTH), lambda b: (0, 0)),
        ],
        out_specs=[st_spec, st_spec],
        out_shape=[st_shape, st_shape],
        compiler_params=pltpu.CompilerParams(dimension_semantics=("arbitrary",),
                                             vmem_limit_bytes=VMEM_LIMIT),
        name="ctx_states",
    )(ctx, mod3, nw, wi, wi, wi, lb)


def _pivot_row(i, level, reverse):
    half = NBLK >> (level + 1)
    boundary = ((i // (2 * half)) * 2 + 1) * half * BLK
    return boundary if reverse else boundary - 1


def _q_side(i, level, reverse):
    later = (i // (NBLK >> (level + 1))) % 2 == 1
    return later != reverse


def _tiles(r0, n_rows, row_tile, width, col_tile):
    return [(slice(r0 + r, r0 + r + row_tile), slice(c, c + col_tile))
            for r in range(0, n_rows, row_tile) for c in range(0, width, col_tile)]


def _scan_gates(c, z_ref, lb, bc_s, g2_s, tri2):
    for rows, cols in _tiles(c * CHUNK, CHUNK, BLK, HG_WIDTH, COL_TILE):
        g, kk = _gates(z_ref[rows, cols], lb[:, cols])
        g_hi = g.astype(BF16)
        g2_s[0, rows, cols] = g_hi
        g2_s[1, rows, cols] = (g - g_hi.astype(F32)).astype(BF16)
        z_ref[rows, cols] = kk
    rows = slice(c * CHUNK, (c + 1) * CHUNK)
    g2 = jnp.concatenate([g2_s[0, rows, :], g2_s[1, rows, :]], axis=0)
    bc_s[rows, :] = jnp.dot(tri2, g2, preferred_element_type=F32)


def _scan_prep(c, k_ref, q_ref, ops_s, opst_s, bc_s, reverse):
    r0 = c * CHUNK
    blk = lax.broadcasted_iota(jnp.int32, (NBLK, 1), 0)

    def per_block(row_of_block):
        out = None
        for i in reversed(range(NBLK)):
            if i + 1 < NBLK and row_of_block[i] == row_of_block[i + 1]:
                continue
            row = bc_s[r0 + row_of_block[i]:r0 + row_of_block[i] + 1, :]
            out = row if out is None else jnp.where(blk <= i, row, out)
        return jnp.broadcast_to(out, (NBLK, out.shape[1]))

    mid = 0.5 * (per_block([i * BLK for i in range(NBLK)])
                 + per_block([i * BLK + BLK - 1 for i in range(NBLK)]))
    tot_row = r0 if reverse else r0 + CHUNK - 1
    tot = bc_s[tot_row:tot_row + 1, :]
    scale = [jnp.exp2(mid), jnp.exp2(tot - mid)]
    for level in range(LEVELS):
        piv = per_block([_pivot_row(i, level, reverse) for i in range(NBLK)])
        scale.append(jnp.exp2(-jnp.abs(mid - piv)))
    for rows, cols in _tiles(r0, CHUNK, BLK, HG_WIDTH, COL_TILE):
        i = (rows.start - r0) // BLK
        d = bc_s[rows, cols] - mid[i:i + 1, cols]
        qd = q_ref[rows, cols].astype(F32) * jnp.exp2(d)
        kd = k_ref[rows, cols] * jnp.exp2(-d)
        ops_s[0, rows, cols] = qd.astype(BF16)
        ops_s[1, rows, cols] = kd.astype(BF16)
        ops_s[2, rows, cols] = (qd * scale[0][i:i + 1, cols]).astype(BF16)
        ops_s[3, rows, cols] = (kd * scale[1][i:i + 1, cols]).astype(BF16)
        for level in range(LEVELS):
            side = qd if _q_side(i, level, reverse) else kd
            ops_s[4 + level, rows, cols] = (side * scale[2 + level][i:i + 1, cols]).astype(BF16)
    rows = slice(r0, r0 + CHUNK)
    for hd in range(HEADS):
        sl = slice(hd * HEAD_DIM, (hd + 1) * HEAD_DIM)
        for j, n in enumerate(_KEY_OPS):
            opst_s[j, c * HEADS + hd] = ops_s[n, rows, sl].T
    return jnp.exp2(tot)


def _pair_masks(reverse):
    t = lax.broadcasted_iota(jnp.int32, (CHUNK, CHUNK), 0)
    s = lax.broadcasted_iota(jnp.int32, (CHUNK, CHUNK), 1)
    early, late = (t, s) if reverse else (s, t)
    masks = [((t // BLK) == (s // BLK)) & (early <= late)]
    for level in range(LEVELS):
        half = CHUNK >> (level + 1)
        masks.append(((late // half) == (early // half) + 1) & ((early // half) % 2 == 0))
    return masks


def _scan_mm(c, dec, v_ref, st_ref, ops_s, opst_s, masks, emit, reverse):
    r0 = c * CHUNK
    rows = slice(r0, r0 + CHUNK)
    q_blocks = [[i for i in range(NBLK) if _q_side(i, level, reverse)] for level in range(LEVELS)]
    attn = []
    for hd in range(HEADS):
        sl = slice(hd * HEAD_DIM, (hd + 1) * HEAD_DIM)
        same = jnp.dot(ops_s[0, rows, sl], opst_s[0, c * HEADS + hd], preferred_element_type=F32)
        cross = []
        for level in range(LEVELS):
            lhs = jnp.concatenate([ops_s[4 + level, r0 + i * BLK:r0 + (i + 1) * BLK, sl]
                                   for i in q_blocks[level]], axis=0)
            cross.append(jnp.dot(lhs, opst_s[1 + level, c * HEADS + hd],
                                 preferred_element_type=F32))
        blocks = []
        for i in range(NBLK):
            blk = slice(i * BLK, (i + 1) * BLK)
            a = jnp.where(masks[0][blk], same[blk], 0.0)
            for level in range(LEVELS):
                if i in q_blocks[level]:
                    j = q_blocks[level].index(i)
                    a = jnp.where(masks[1 + level][blk], cross[level][j * BLK:(j + 1) * BLK], a)
            blocks.append(a)
        attn.append(jnp.concatenate(blocks, axis=0).astype(BF16))
    for hd in range(HEADS):
        sl = slice(hd * HEAD_DIM, (hd + 1) * HEAD_DIM)
        st = st_ref[hd]
        v = v_ref[rows, sl]
        lhs = jnp.concatenate([ops_s[2, rows, sl], attn[hd]], axis=1)
        rhs = jnp.concatenate([st.astype(BF16), v], axis=0)
        emit(rows, sl, jnp.dot(lhs, rhs, preferred_element_type=F32))
        kv = lax.dot_general(ops_s[3, rows, sl], v, _TN, preferred_element_type=F32)
        dec_col = jnp.transpose(jnp.broadcast_to(dec[:, sl], (HEAD_DIM, HEAD_DIM)))
        st_ref[hd] = st * dec_col + kv


def _scan_tile(z_ref, q_ref, v_ref, lb, st_ref, ops_s, opst_s, bc_s, g2_s, emit, reverse,
               after_gates=None, before_matmuls=None, after_chunk=None):
    n_chunks = z_ref.shape[0] // CHUNK
    order = range(n_chunks - 1, -1, -1) if reverse else range(n_chunks)
    tri_b = _tri(CHUNK, reverse).astype(BF16)
    tri2 = jnp.concatenate([tri_b, tri_b], axis=1)
    masks = _pair_masks(reverse)
    for c in order:
        _scan_gates(c, z_ref, lb, bc_s, g2_s, tri2)
    if after_gates is not None:
        after_gates()
    dec = {c: _scan_prep(c, z_ref, q_ref, ops_s, opst_s, bc_s, reverse) for c in order}
    if before_matmuls is not None:
        before_matmuls()
    for c in order:
        _scan_mm(c, dec[c], v_ref, st_ref, ops_s, opst_s, masks, emit, reverse)
        if after_chunk is not None:
            after_chunk(slice(c * CHUNK, (c + 1) * CHUNK))


def _init_state(s0_ref, st_s):
    @pl.when(pl.program_id(1) == 0)
    def _():
        st_s[...] = s0_ref[0]


_SCAN_SCRATCH = [
    pltpu.VMEM((HEADS, HEAD_DIM, HEAD_DIM), F32),
    pltpu.VMEM((N_OPS, TILE_T, HG_WIDTH), BF16),
    pltpu.VMEM((len(_KEY_OPS), TILE_T // CHUNK * HEADS, HEAD_DIM, CHUNK), BF16),
    pltpu.VMEM((TILE_T, HG_WIDTH), F32),
    pltpu.VMEM((TILE_T, HG_WIDTH), F32),
    pltpu.VMEM((2, TILE_T, HG_WIDTH), BF16),
]


def _bwd_kernel(x_ref, mod_ref, nw_ref, wq_ref, wv_ref, wz_ref, lb_ref, s0_ref,
                q_ref, v_ref, ob_ref, st_s, ops_s, opst_s, bc_s, z_s, g2_s):
    _init_state(s0_ref, st_s)
    h = _modulated_norm(x_ref[0], nw_ref[0:1, :], mod_ref)
    z_s[...] = jnp.dot(h, wz_ref[...], preferred_element_type=F32)
    q_ref[0] = jnp.dot(h, wq_ref[...], preferred_element_type=F32).astype(BF16)

    def value_proj():
        v_ref[0] = jnp.dot(h, wv_ref[...], preferred_element_type=F32).astype(BF16)

    def emit(rows, sl, o):
        ob_ref[0, rows, sl] = o.astype(BF16)

    _scan_tile(z_s, q_ref.at[0], v_ref.at[0], lb_ref[1:2, :], st_s, ops_s, opst_s, bc_s, g2_s, emit,
               reverse=True,
               after_gates=value_proj)


def _bwd_call(x, mod3, nw, wi, lb, sb):
    bsz, seq, _ = x.shape
    nt = seq // TILE_T
    tile = lambda b, t: (b, nt - 1 - t, 0)
    act = jax.ShapeDtypeStruct((bsz, seq, HG_WIDTH), BF16)
    act_spec = pl.BlockSpec((1, TILE_T, HG_WIDTH), tile)
    return pl.pallas_call(
        _bwd_kernel,
        grid=(bsz, nt),
        in_specs=[
            pl.BlockSpec((1, TILE_T, D_MODEL), tile),
            pl.BlockSpec((1, 3, D_MODEL), lambda b, t: (b, 0, 0)),
            pl.BlockSpec((ROW_PAD, D_MODEL), lambda b, t: (0, 0)),
            _w_in_cols(COL_Q), _w_in_cols(COL_V), _w_in_cols(COL_ZB),
            pl.BlockSpec((ROW_PAD, HG_WIDTH), lambda b, t: (0, 0)),
            pl.BlockSpec((1, HEADS, HEAD_DIM, HEAD_DIM), lambda b, t: (b, 0, 0, 0)),
        ],
        out_specs=[act_spec, act_spec, act_spec],
        out_shape=[act, act, act],
        scratch_shapes=_SCAN_SCRATCH,
        compiler_params=pltpu.CompilerParams(dimension_semantics=("arbitrary", "arbitrary"),
                                             vmem_limit_bytes=VMEM_LIMIT),
        name="bwd_scan",
    )(x, mod3, nw, wi, wi, wi, lb, sb)


def _stage_weights(src_hbm, col0, dst_s, stage, sem):
    chunk = stage.shape[1]
    n_chunks = dst_s.shape[0] // chunk
    n_cols = dst_s.shape[1]

    def copy(i, slot):
        return pltpu.make_async_copy(
            src_hbm.at[pl.ds(i * chunk, chunk), pl.ds(col0, n_cols)], stage.at[slot], sem.at[slot])

    copy(0, 0).start()

    def body(i, carry):
        slot = i % 2

        @pl.when(i + 1 < n_chunks)
        def _():
            copy(i + 1, 1 - slot).start()

        copy(i, slot).wait()
        dst_s[pl.ds(i * chunk, chunk), :] = stage[slot].astype(BF16)
        return carry

    lax.fori_loop(0, n_chunks, body, 0)


def _fwd_kernel(x_ref, mod_ref, nw_ref, wz_ref, win_hbm, lb_ref, s0_ref, q_ref, v_ref, ob_ref,
                onw_ref, cw_ref, wout_hbm, fnw_ref, out_ref,
                st_s, ops_s, opst_s, bc_s, z_s, g2_s, mix_s, cv_s,
                wr_s, wo_s, stage_r, stage_o, sem_r, sem_o):
    _init_state(s0_ref, st_s)

    @pl.when((pl.program_id(0) == 0) & (pl.program_id(1) == 0))
    def _():
        _stage_weights(win_hbm, COL_GATE * HG_WIDTH, wr_s, stage_r, sem_r)
        _stage_weights(wout_hbm, 0, wo_s, stage_o, sem_o)

    wg_ref = wr_s.at[:, 0:HG_WIDTH]
    wc_refs = [wr_s.at[:, (1 + j) * CONV_WIDTH:(2 + j) * CONV_WIDTH] for j in range(4)]
    g_s = z_s
    o_s = bc_s
    h = _modulated_norm(x_ref[0], nw_ref[0:1, :], mod_ref)

    z_s[...] = jnp.dot(h, wz_ref[...], preferred_element_type=F32)

    def conv_proj(j, wc_ref):
        cv_s[:, j * CONV_WIDTH:(j + 1) * CONV_WIDTH] = jnp.dot(h, wc_ref[...],
                                                               preferred_element_type=F32)

    conv_proj(0, wc_refs[0])

    def conv_proj_rest():
        for j, wc_ref in enumerate(wc_refs[1:], start=1):
            conv_proj(j, wc_ref)

    def conv_branch():
        g_s[...] = jnp.dot(h, wg_ref[...], preferred_element_type=F32)
        n_tok = cv_s.shape[0]
        u = cv_s[:, 2 * CONV_WIDTH:3 * CONV_WIDTH] * cv_s[:, 0:CONV_WIDTH]
        col = lax.broadcasted_iota(jnp.int32, (n_tok, 1), 0) % GRID_W
        u_prev = jnp.where(col == 0, 0.0, pltpu.roll(u, 1, axis=0))
        u_next = jnp.where(col == GRID_W - 1, 0.0, pltpu.roll(u, n_tok - 1, axis=0))
        conv = cw_ref[0:1, :] * u_prev + cw_ref[1:2, :] * u + cw_ref[2:3, :] * u_next
        cg = cv_s[:, 3 * CONV_WIDTH:4 * CONV_WIDTH]
        y_cv = cv_s[:, CONV_WIDTH:2 * CONV_WIDTH] * conv * _silu(cg)
        mix_s[:, HG_WIDTH:] = y_cv.astype(BF16)

    def emit(rows, sl, o):
        o_s[rows, sl] = o + ob_ref[0, rows, sl].astype(F32)

    def gated_readout(rows):
        onw = onw_ref[0:1, :]
        for hd in range(HEADS):
            sl = slice(hd * HEAD_DIM, (hd + 1) * HEAD_DIM)
            o = o_s[rows, sl]
            g = g_s[rows, sl]
            y = (o * _rms_rows(o) * onw[:, sl]) * _silu(g)
            mix_s[rows, sl] = y.astype(BF16)

    _scan_tile(z_s, q_ref.at[0], v_ref.at[0], lb_ref[0:1, :], st_s, ops_s, opst_s, bc_s, g2_s, emit,
               reverse=False,
               after_gates=conv_proj_rest, before_matmuls=conv_branch, after_chunk=gated_readout)

    half = x_ref.shape[1] // 2
    for rows in (slice(0, half), slice(half, 2 * half)):
        y = jnp.dot(mix_s[rows, :], wo_s[...], preferred_element_type=F32)
        r = x_ref[0, rows, :] + mod_ref[0, 2:3, :] * y
        out_ref[0, rows, :] = (r * _rms_rows(r)) * fnw_ref[0:1, :]


def _fwd_call(x, mod3, nw, wi, w_in, lb, sf, q, v, ob, onw, conv_w, w_out, fnw):
    bsz, seq, _ = x.shape
    nt = seq // TILE_T
    tile = lambda b, t: (b, t, 0)
    const2 = lambda b, t: (0, 0)
    act_spec = pl.BlockSpec((1, TILE_T, HG_WIDTH), tile)
    return pl.pallas_call(
        _fwd_kernel,
        grid=(bsz, nt),
        in_specs=[
            pl.BlockSpec((1, TILE_T, D_MODEL), tile),
            pl.BlockSpec((1, 3, D_MODEL), lambda b, t: (b, 0, 0)),
            pl.BlockSpec((ROW_PAD, D_MODEL), const2),
            _w_in_cols(COL_ZF),
            pl.BlockSpec(memory_space=pl.ANY),
            pl.BlockSpec((ROW_PAD, HG_WIDTH), const2),
            pl.BlockSpec((1, HEADS, HEAD_DIM, HEAD_DIM), lambda b, t: (b, 0, 0, 0)),
            act_spec, act_spec, act_spec,
            pl.BlockSpec((ROW_PAD, HG_WIDTH), const2),
            pl.BlockSpec(conv_w.shape, const2),
            pl.BlockSpec(memory_space=pl.ANY),
            pl.BlockSpec((ROW_PAD, D_MODEL), const2),
        ],
        out_specs=pl.BlockSpec((1, TILE_T, D_MODEL), tile),
        out_shape=jax.ShapeDtypeStruct((bsz, seq, D_MODEL), F32),
        scratch_shapes=_SCAN_SCRATCH + [
            pltpu.VMEM((TILE_T, HG_WIDTH + CONV_WIDTH), BF16),
            pltpu.VMEM((TILE_T, 4 * CONV_WIDTH), F32),
            pltpu.VMEM((D_MODEL, N_REST_COLS), BF16),
            pltpu.VMEM(w_out.shape, BF16),
            pltpu.VMEM((2, STAGE_ROWS, N_REST_COLS), F32),
            pltpu.VMEM((2, 4 * STAGE_ROWS, D_MODEL), F32),
            pltpu.SemaphoreType.DMA((2,)),
            pltpu.SemaphoreType.DMA((2,)),
        ],
        compiler_params=pltpu.CompilerParams(dimension_semantics=("arbitrary", "arbitrary"),
                                             vmem_limit_bytes=VMEM_LIMIT),
        name="fwd_scan_out",
    )(x, mod3, nw, wi, w_in, lb, sf, q, v, ob, onw, conv_w, w_out, fnw)


def kernel(x, c, ctx, c_ctx, norm_w, w_ada, b_ada, w_in, hg_lb_logits, hg_onorm_w, conv_w,
           w_out, final_norm_w):
    assert w_in.shape[0] == 1, "single-layer block"
    bsz = x.shape[0]
    cc = jnp.concatenate([c, c_ctx[None, :], jnp.zeros((8 - bsz - 1, D_MODEL), F32)], axis=0)
    mod, lb = _mod_call(cc, w_ada[0], b_ada[0][None, :], hg_lb_logits.reshape(-1, HG_WIDTH))
    mod3 = mod.reshape(8, 3, D_MODEL)

    wi = w_in[0, :, :COL_GATE * HG_WIDTH].astype(BF16)
    nw = _pad_rows(norm_w[0][None, :])

    sf, sb = _ctx_call(ctx, mod3, nw, wi, lb)
    q, v, ob = _bwd_call(x, mod3, nw, wi, lb, sb)
    onw = _pad_rows(jnp.tile(hg_onorm_w[0], HEADS)[None, :])
    return _fwd_call(x, mod3, nw, wi, w_in[0], lb, sf, q, v, ob, onw, conv_w[0],
                     w_out[0], _pad_rows(final_norm_w[None, :]))
```

```python
import jax
import jax.numpy as jnp
from jax import lax
from jax.experimental import pallas as pl
from jax.experimental.pallas import tpu as pltpu

D_MODEL = 1024
HEADS = 8
HEAD_DIM = 128
HG_WIDTH = HEADS * HEAD_DIM
CONV_WIDTH = 1024
GRID_W = 64
EPS = 1e-6

CHUNK = 128
BLK = 16
NBLK = CHUNK // BLK
LEVELS = NBLK.bit_length() - 1
N_OPS = 4 + LEVELS
_KEY_OPS = (1,) + tuple(range(4, 4 + LEVELS))
TILE_T = 512
COL_TILE = 256
EXP_RANGE = 115.0
LOG2F_FLOOR = 2.0 * EXP_RANGE / (BLK - 1)
VMEM_CAPACITY = 64 * 1024 * 1024
VMEM_LIMIT = VMEM_CAPACITY - 4 * 1024 * 1024
ROW_PAD = 4

COL_Q, COL_V, COL_ZF, COL_ZB, COL_GATE, COL_CONV = 0, 1, 2, 3, 4, 5
N_REST_COLS = 5 * HG_WIDTH
STAGE_ROWS = 32
STAGE_DEPTH = 8

F32 = jnp.float32
BF16 = jnp.bfloat16

_TN = (((0,), (0,)), ((), ()))


def _w_in_cols(group):
    return pl.BlockSpec((D_MODEL, HG_WIDTH), lambda *_: (0, group), pipeline_mode=pl.Buffered(1))


def _pad_rows(a):
    return jnp.pad(a, ((0, ROW_PAD - a.shape[0]), (0, 0)))


def _silu(x):
    hx = 0.5 * x
    return hx + hx * jnp.tanh(hx)


def _rms_rows(xf):
    return lax.rsqrt(jnp.mean(xf * xf, axis=-1, keepdims=True) + EPS)


def _tri(n, reverse):
    t = lax.broadcasted_iota(jnp.int32, (n, n), 0)
    r = lax.broadcasted_iota(jnp.int32, (n, n), 1)
    return (r >= t) if reverse else (r <= t)


def _split_bf16(a):
    hi = a.astype(BF16)
    return hi, (a - hi.astype(F32)).astype(BF16)


def _cumsum_time(g, tri2):
    return jnp.dot(tri2, jnp.concatenate(_split_bf16(g), axis=0), preferred_element_type=F32)


def _gates(z, lb):
    b = 0.5 * (1.0 - lb)
    bt = b * jnp.tanh(0.5 * z)
    return jnp.maximum(jnp.log2((1.0 - b) + bt), -LOG2F_FLOOR), b - bt


def _mod_kernel(cc_ref, w_ref, b_ref, lg_ref, mod_ref, lb_ref):
    cc = cc_ref[...]
    s = _silu(cc)
    s_hi, s_lo = _split_bf16(s)
    w_hi, w_lo = _split_bf16(w_ref[...])
    dot = lambda a, b: jnp.dot(a, b, preferred_element_type=F32)
    mod_ref[...] = dot(s_hi, w_hi) + (dot(s_hi, w_lo) + dot(s_lo, w_hi)) + b_ref[...]
    n_rows = lg_ref.shape[0] // 2
    for d in range(2):
        lg = lg_ref[d * n_rows:(d + 1) * n_rows, :]
        e = jnp.exp(lg - jnp.max(lg, axis=0, keepdims=True))
        lb_ref[d:d + 1, :] = e[0:1, :] / jnp.sum(e, axis=0, keepdims=True)
    lb_ref[2:, :] = jnp.zeros((ROW_PAD - 2, lb_ref.shape[1]), F32)


def _mod_call(cc, w_ada, b_ada, lb_logits):
    n_out = w_ada.shape[1]
    bn = 768
    return pl.pallas_call(
        _mod_kernel,
        grid=(n_out // bn,),
        in_specs=[
            pl.BlockSpec((8, D_MODEL), lambda j: (0, 0)),
            pl.BlockSpec((D_MODEL, bn), lambda j: (0, j)),
            pl.BlockSpec((1, bn), lambda j: (0, j)),
            pl.BlockSpec(lb_logits.shape, lambda j: (0, 0)),
        ],
        out_specs=[
            pl.BlockSpec((8, bn), lambda j: (0, j)),
            pl.BlockSpec((ROW_PAD, HG_WIDTH), lambda j: (0, 0)),
        ],
        out_shape=[
            jax.ShapeDtypeStruct((8, n_out), F32),
            jax.ShapeDtypeStruct((ROW_PAD, HG_WIDTH), F32),
        ],
        compiler_params=pltpu.CompilerParams(dimension_semantics=("arbitrary",)),
        name="adaln_mod",
    )(cc, w_ada, b_ada, lb_logits)


def _modulated_norm(x, nw, mod_ref):
    shift = mod_ref[0, 0:1, :]
    scale = mod_ref[0, 1:2, :]
    return ((x * _rms_rows(x)) * (nw * (1.0 + scale)) + shift).astype(BF16)


def _ctx_kernel(x_ref, mod_ref, nw_ref, wv_ref, wzf_ref, wzb_ref, lb_ref, sf_ref, sb_ref):
    n = x_ref.shape[1]
    h = _modulated_norm(x_ref[0], nw_ref[0:1, :], mod_ref)
    v = jnp.dot(h, wv_ref[...], preferred_element_type=F32).astype(BF16)
    for d, wz_ref, out_ref in ((0, wzf_ref, sf_ref), (1, wzb_ref, sb_ref)):
        z = jnp.dot(h, wz_ref[...], preferred_element_type=F32)
        g, kk = _gates(z, lb_ref[d:d + 1, :])
        tri = _tri(n, reverse=bool(d)).astype(BF16)
        bc = _cumsum_time(g, jnp.concatenate([tri, tri], axis=1))
        tot = bc[0:1, :] if d else bc[n - 1:n, :]
        ks = (kk * jnp.exp2(tot - bc)).astype(BF16)
        for hd in range(HEADS):
            sl = slice(hd * HEAD_DIM, (hd + 1) * HEAD_DIM)
            out_ref[0, hd] = lax.dot_general(ks[:, sl], v[:, sl], _TN, preferred_element_type=F32)


def _ctx_call(ctx, mod3, nw, wi, lb):
    bsz, n, _ = ctx.shape
    st_shape = jax.ShapeDtypeStruct((bsz, HEADS, HEAD_DIM, HEAD_DIM), F32)
    st_spec = pl.BlockSpec((1, HEADS, HEAD_DIM, HEAD_DIM), lambda b: (b, 0, 0, 0))
    return pl.pallas_call(
        _ctx_kernel,
        grid=(bsz,),
        in_specs=[
            pl.BlockSpec((1, n, D_MODEL), lambda b: (b, 0, 0)),
            pl.BlockSpec((1, 3, D_MODEL), lambda b: (bsz, 0, 0)),
            pl.BlockSpec((ROW_PAD, D_MODEL), lambda b: (0, 0)),
            _w_in_cols(COL_V), _w_in_cols(COL_ZF), _w_in_cols(COL_ZB),
            pl.BlockSpec((ROW_PAD, HG_WIDTH), lambda b: (0, 0)),
        ],
        out_specs=[st_spec, st_spec],
        out_shape=[st_shape, st_shape],
        compiler_params=pltpu.CompilerParams(dimension_semantics=("arbitrary",),
                                             vmem_limit_bytes=VMEM_LIMIT),
        name="ctx_states",
    )(ctx, mod3, nw, wi, wi, wi, lb)


def _pivot_row(i, level, reverse):
    half = NBLK >> (level + 1)
    boundary = ((i // (2 * half)) * 2 + 1) * half * BLK
    return boundary if reverse else boundary - 1


def _q_side(i, level, reverse):
    later = (i // (NBLK >> (level + 1))) % 2 == 1
    return later != reverse


def _tiles(r0, n_rows, row_tile, width, col_tile):
    return [(slice(r0 + r, r0 + r + row_tile), slice(c, c + col_tile))
            for r in range(0, n_rows, row_tile) for c in range(0, width, col_tile)]


def _scan_gates(c, z_ref, lb, bc_s, g2_s, tri2):
    for rows, cols in _tiles(c * CHUNK, CHUNK, BLK, HG_WIDTH, COL_TILE):
        g, kk = _gates(z_ref[rows, cols], lb[:, cols])
        g_hi = g.astype(BF16)
        g2_s[0, rows, cols] = g_hi
        g2_s[1, rows, cols] = (g - g_hi.astype(F32)).astype(BF16)
        z_ref[rows, cols] = kk
    rows = slice(c * CHUNK, (c + 1) * CHUNK)
    g2 = jnp.concatenate([g2_s[0, rows, :], g2_s[1, rows, :]], axis=0)
    bc_s[rows, :] = jnp.dot(tri2, g2, preferred_element_type=F32)


def _scan_prep(c, k_ref, q_ref, ops_s, opst_s, bc_s, reverse):
    r0 = c * CHUNK
    blk = lax.broadcasted_iota(jnp.int32, (NBLK, 1), 0)

    def per_block(row_of_block):
        out = None
        for i in reversed(range(NBLK)):
            if i + 1 < NBLK and row_of_block[i] == row_of_block[i + 1]:
                continue
            row = bc_s[r0 + row_of_block[i]:r0 + row_of_block[i] + 1, :]
            out = row if out is None else jnp.where(blk <= i, row, out)
        return jnp.broadcast_to(out, (NBLK, out.shape[1]))

    mid = 0.5 * (per_block([i * BLK for i in range(NBLK)])
                 + per_block([i * BLK + BLK - 1 for i in range(NBLK)]))
    tot_row = r0 if reverse else r0 + CHUNK - 1
    tot = bc_s[tot_row:tot_row + 1, :]
    scale = [jnp.exp2(mid), jnp.exp2(tot - mid)]
    for level in range(LEVELS):
        piv = per_block([_pivot_row(i, level, reverse) for i in range(NBLK)])
        scale.append(jnp.exp2(-jnp.abs(mid - piv)))
    for rows, cols in _tiles(r0, CHUNK, BLK, HG_WIDTH, COL_TILE):
        i = (rows.start - r0) // BLK
        d = bc_s[rows, cols] - mid[i:i + 1, cols]
        qd = q_ref[rows, cols].astype(F32) * jnp.exp2(d)
        kd = k_ref[rows, cols] * jnp.exp2(-d)
        ops_s[0, rows, cols] = qd.astype(BF16)
        ops_s[1, rows, cols] = kd.astype(BF16)
        ops_s[2, rows, cols] = (qd * scale[0][i:i + 1, cols]).astype(BF16)
        ops_s[3, rows, cols] = (kd * scale[1][i:i + 1, cols]).astype(BF16)
        for level in range(LEVELS):
            side = qd if _q_side(i, level, reverse) else kd
            ops_s[4 + level, rows, cols] = (side * scale[2 + level][i:i + 1, cols]).astype(BF16)
    rows = slice(r0, r0 + CHUNK)
    for hd in range(HEADS):
        sl = slice(hd * HEAD_DIM, (hd + 1) * HEAD_DIM)
        for j, n in enumerate(_KEY_OPS):
            opst_s[j, c * HEADS + hd] = ops_s[n, rows, sl].T
    return jnp.exp2(tot)


def _pair_masks(reverse):
    t = lax.broadcasted_iota(jnp.int32, (CHUNK, CHUNK), 0)
    s = lax.broadcasted_iota(jnp.int32, (CHUNK, CHUNK), 1)
    early, late = (t, s) if reverse else (s, t)
    masks = [((t // BLK) == (s // BLK)) & (early <= late)]
    for level in range(LEVELS):
        half = CHUNK >> (level + 1)
        masks.append(((late // half) == (early // half) + 1) & ((early // half) % 2 == 0))
    return masks


def _scan_mm(c, dec, v_ref, st_ref, ops_s, opst_s, masks, emit, reverse):
    r0 = c * CHUNK
    rows = slice(r0, r0 + CHUNK)
    q_blocks = [[i for i in range(NBLK) if _q_side(i, level, reverse)] for level in range(LEVELS)]
    attn = []
    for hd in range(HEADS):
        sl = slice(hd * HEAD_DIM, (hd + 1) * HEAD_DIM)
        same = jnp.dot(ops_s[0, rows, sl], opst_s[0, c * HEADS + hd], preferred_element_type=F32)
        cross = []
        for level in range(LEVELS):
            lhs = jnp.concatenate([ops_s[4 + level, r0 + i * BLK:r0 + (i + 1) * BLK, sl]
                                   for i in q_blocks[level]], axis=0)
            cross.append(jnp.dot(lhs, opst_s[1 + level, c * HEADS + hd],
                                 preferred_element_type=F32))
        blocks = []
        for i in range(NBLK):
            blk = slice(i * BLK, (i + 1) * BLK)
            a = jnp.where(masks[0][blk], same[blk], 0.0)
            for level in range(LEVELS):
                if i in q_blocks[level]:
                    j = q_blocks[level].index(i)
                    a = jnp.where(masks[1 + level][blk], cross[level][j * BLK:(j + 1) * BLK], a)
            blocks.append(a)
        attn.append(jnp.concatenate(blocks, axis=0).astype(BF16))
    for hd in range(HEADS):
        sl = slice(hd * HEAD_DIM, (hd + 1) * HEAD_DIM)
        st = st_ref[hd]
        v = v_ref[rows, sl]
        lhs = jnp.concatenate([ops_s[2, rows, sl], attn[hd]], axis=1)
        rhs = jnp.concatenate([st.astype(BF16), v], axis=0)
        emit(rows, sl, jnp.dot(lhs, rhs, preferred_element_type=F32))
        kv = lax.dot_general(ops_s[3, rows, sl], v, _TN, preferred_element_type=F32)
        dec_col = jnp.transpose(jnp.broadcast_to(dec[:, sl], (HEAD_DIM, HEAD_DIM)))
        st_ref[hd] = st * dec_col + kv


def _scan_tile(z_ref, q_ref, v_ref, lb, st_ref, ops_s, opst_s, bc_s, g2_s, emit, reverse,
               after_gates=None, before_matmuls=None, after_chunk=None):
    n_chunks = z_ref.shape[0] // CHUNK
    order = range(n_chunks - 1, -1, -1) if reverse else range(n_chunks)
    tri_b = _tri(CHUNK, reverse).astype(BF16)
    tri2 = jnp.concatenate([tri_b, tri_b], axis=1)
    masks = _pair_masks(reverse)
    for c in order:
        _scan_gates(c, z_ref, lb, bc_s, g2_s, tri2)
    if after_gates is not None:
        after_gates()
    dec = {c: _scan_prep(c, z_ref, q_ref, ops_s, opst_s, bc_s, reverse) for c in order}
    if before_matmuls is not None:
        before_matmuls()
    for c in order:
        _scan_mm(c, dec[c], v_ref, st_ref, ops_s, opst_s, masks, emit, reverse)
        if after_chunk is not None:
            after_chunk(slice(c * CHUNK, (c + 1) * CHUNK))


def _init_state(s0_ref, st_s):
    @pl.when(pl.program_id(1) == 0)
    def _():
        st_s[...] = s0_ref[0]


_SCAN_SCRATCH = [
    pltpu.VMEM((HEADS, HEAD_DIM, HEAD_DIM), F32),
    pltpu.VMEM((N_OPS, TILE_T, HG_WIDTH), BF16),
    pltpu.VMEM((len(_KEY_OPS), TILE_T // CHUNK * HEADS, HEAD_DIM, CHUNK), BF16),
    pltpu.VMEM((TILE_T, HG_WIDTH), F32),
    pltpu.VMEM((TILE_T, HG_WIDTH), F32),
    pltpu.VMEM((2, TILE_T, HG_WIDTH), BF16),
]


def _bwd_kernel(x_ref, mod_ref, nw_ref, wq_ref, wv_ref, wz_ref, lb_ref, s0_ref,
                q_ref, v_ref, ob_ref, st_s, ops_s, opst_s, bc_s, z_s, g2_s):
    _init_state(s0_ref, st_s)
    h = _modulated_norm(x_ref[0], nw_ref[0:1, :], mod_ref)
    z_s[...] = jnp.dot(h, wz_ref[...], preferred_element_type=F32)
    q_ref[0] = jnp.dot(h, wq_ref[...], preferred_element_type=F32).astype(BF16)

    def value_proj():
        v_ref[0] = jnp.dot(h, wv_ref[...], preferred_element_type=F32).astype(BF16)

    def emit(rows, sl, o):
        ob_ref[0, rows, sl] = o.astype(BF16)

    _scan_tile(z_s, q_ref.at[0], v_ref.at[0], lb_ref[1:2, :], st_s, ops_s, opst_s, bc_s, g2_s, emit,
               reverse=True,
               after_gates=value_proj)


def _bwd_call(x, mod3, nw, wi, lb, sb):
    bsz, seq, _ = x.shape
    nt = seq // TILE_T
    tile = lambda b, t: (b, nt - 1 - t, 0)
    act = jax.ShapeDtypeStruct((bsz, seq, HG_WIDTH), BF16)
    act_spec = pl.BlockSpec((1, TILE_T, HG_WIDTH), tile)
    return pl.pallas_call(
        _bwd_kernel,
        grid=(bsz, nt),
        in_specs=[
            pl.BlockSpec((1, TILE_T, D_MODEL), tile),
            pl.BlockSpec((1, 3, D_MODEL), lambda b, t: (b, 0, 0)),
            pl.BlockSpec((ROW_PAD, D_MODEL), lambda b, t: (0, 0)),
            _w_in_cols(COL_Q), _w_in_cols(COL_V), _w_in_cols(COL_ZB),
            pl.BlockSpec((ROW_PAD, HG_WIDTH), lambda b, t: (0, 0)),
            pl.BlockSpec((1, HEADS, HEAD_DIM, HEAD_DIM), lambda b, t: (b, 0, 0, 0)),
        ],
        out_specs=[act_spec, act_spec, act_spec],
        out_shape=[act, act, act],
        scratch_shapes=_SCAN_SCRATCH,
        compiler_params=pltpu.CompilerParams(dimension_semantics=("arbitrary", "arbitrary"),
                                             vmem_limit_bytes=VMEM_LIMIT),
        name="bwd_scan",
    )(x, mod3, nw, wi, wi, wi, lb, sb)


def _stage_weights(src_hbm, col0, dst_s, stage_s, sem, depth):
    per_group = dst_s.shape[0] // STAGE_ROWS
    n_tiles = per_group * (dst_s.shape[1] // HG_WIDTH)
    per_row = stage_s.shape[1] // HG_WIDTH
    aligned = lambda v: v if isinstance(v, int) else pl.multiple_of(v, HG_WIDTH)

    def tile(k):
        g, r = k // per_group, (k % per_group) * STAGE_ROWS
        slot = k % depth
        sr = (slot // per_row) * STAGE_ROWS
        sc = aligned((slot % per_row) * HG_WIDTH)
        dc = aligned(g * HG_WIDTH)
        src = src_hbm.at[pl.ds(r, STAGE_ROWS), pl.ds(col0 + dc, HG_WIDTH)]
        stg = stage_s.at[pl.ds(sr, STAGE_ROWS), pl.ds(sc, HG_WIDTH)]
        dst = dst_s.at[pl.ds(r, STAGE_ROWS), pl.ds(dc, HG_WIDTH)]
        return pltpu.make_async_copy(src, stg, sem.at[slot]), stg, dst

    for k in range(min(depth, n_tiles)):
        tile(k)[0].start()

    def body(k, carry):
        copy, stg, dst = tile(k)
        copy.wait()
        dst[...] = stg[...].astype(BF16)

        @pl.when(k + depth < n_tiles)
        def _():
            tile(k + depth)[0].start()

        return carry

    lax.fori_loop(0, n_tiles, body, 0)


def _fwd_kernel(x_ref, mod_ref, nw_ref, wz_ref, win_hbm, lb_ref, s0_ref, q_ref, v_ref, ob_ref,
                onw_ref, cw_ref, wout_hbm, fnw_ref, out_ref,
                st_s, ops_s, opst_s, bc_s, z_s, g2_s, mix_s, cv_s,
                wr_s, wo_s, stage_s, sem):
    _init_state(s0_ref, st_s)

    @pl.when((pl.program_id(0) == 0) & (pl.program_id(1) == 0))
    def _():
        _stage_weights(win_hbm, COL_GATE * HG_WIDTH, wr_s, stage_s, sem, STAGE_DEPTH)
        _stage_weights(wout_hbm, 0, wo_s, stage_s, sem, STAGE_DEPTH)

    wg_ref = wr_s.at[:, 0:HG_WIDTH]
    wc_refs = [wr_s.at[:, (1 + j) * CONV_WIDTH:(2 + j) * CONV_WIDTH] for j in range(4)]
    g_s = z_s
    o_s = bc_s
    h = _modulated_norm(x_ref[0], nw_ref[0:1, :], mod_ref)

    z_s[...] = jnp.dot(h, wz_ref[...], preferred_element_type=F32)

    def conv_proj(j, wc_ref):
        cv_s[:, j * CONV_WIDTH:(j + 1) * CONV_WIDTH] = jnp.dot(h, wc_ref[...],
                                                               preferred_element_type=F32)

    conv_proj(0, wc_refs[0])

    def conv_proj_rest():
        for j, wc_ref in enumerate(wc_refs[1:], start=1):
            conv_proj(j, wc_ref)

    def conv_branch():
        g_s[...] = jnp.dot(h, wg_ref[...], preferred_element_type=F32)
        n_tok = cv_s.shape[0]
        u = cv_s[:, 2 * CONV_WIDTH:3 * CONV_WIDTH] * cv_s[:, 0:CONV_WIDTH]
        col = lax.broadcasted_iota(jnp.int32, (n_tok, 1), 0) % GRID_W
        u_prev = jnp.where(col == 0, 0.0, pltpu.roll(u, 1, axis=0))
        u_next = jnp.where(col == GRID_W - 1, 0.0, pltpu.roll(u, n_tok - 1, axis=0))
        conv = cw_ref[0:1, :] * u_prev + cw_ref[1:2, :] * u + cw_ref[2:3, :] * u_next
        cg = cv_s[:, 3 * CONV_WIDTH:4 * CONV_WIDTH]
        y_cv = cv_s[:, CONV_WIDTH:2 * CONV_WIDTH] * conv * _silu(cg)
        mix_s[:, HG_WIDTH:] = y_cv.astype(BF16)

    def emit(rows, sl, o):
        o_s[rows, sl] = o + ob_ref[0, rows, sl].astype(F32)

    def gated_readout(rows):
        onw = onw_ref[0:1, :]
        for hd in range(HEADS):
            sl = slice(hd * HEAD_DIM, (hd + 1) * HEAD_DIM)
            o = o_s[rows, sl]
            g = g_s[rows, sl]
            y = (o * _rms_rows(o) * onw[:, sl]) * _silu(g)
            mix_s[rows, sl] = y.astype(BF16)

    _scan_tile(z_s, q_ref.at[0], v_ref.at[0], lb_ref[0:1, :], st_s, ops_s, opst_s, bc_s, g2_s, emit,
               reverse=False,
               after_gates=conv_proj_rest, before_matmuls=conv_branch, after_chunk=gated_readout)

    half = x_ref.shape[1] // 2
    for rows in (slice(0, half), slice(half, 2 * half)):
        y = jnp.dot(mix_s[rows, :], wo_s[...], preferred_element_type=F32)
        r = x_ref[0, rows, :] + mod_ref[0, 2:3, :] * y
        out_ref[0, rows, :] = (r * _rms_rows(r)) * fnw_ref[0:1, :]


def _fwd_call(x, mod3, nw, wi, w_in, lb, sf, q, v, ob, onw, conv_w, w_out, fnw):
    bsz, seq, _ = x.shape
    nt = seq // TILE_T
    tile = lambda b, t: (b, t, 0)
    const2 = lambda b, t: (0, 0)
    act_spec = pl.BlockSpec((1, TILE_T, HG_WIDTH), tile)
    return pl.pallas_call(
        _fwd_kernel,
        grid=(bsz, nt),
        in_specs=[
            pl.BlockSpec((1, TILE_T, D_MODEL), tile),
            pl.BlockSpec((1, 3, D_MODEL), lambda b, t: (b, 0, 0)),
            pl.BlockSpec((ROW_PAD, D_MODEL), const2),
            _w_in_cols(COL_ZF),
            pl.BlockSpec(memory_space=pl.ANY),
            pl.BlockSpec((ROW_PAD, HG_WIDTH), const2),
            pl.BlockSpec((1, HEADS, HEAD_DIM, HEAD_DIM), lambda b, t: (b, 0, 0, 0)),
            act_spec, act_spec, act_spec,
            pl.BlockSpec((ROW_PAD, HG_WIDTH), const2),
            pl.BlockSpec(conv_w.shape, const2),
            pl.BlockSpec(memory_space=pl.ANY),
            pl.BlockSpec((ROW_PAD, D_MODEL), const2),
        ],
        out_specs=pl.BlockSpec((1, TILE_T, D_MODEL), tile),
        out_shape=jax.ShapeDtypeStruct((bsz, seq, D_MODEL), F32),
        scratch_shapes=_SCAN_SCRATCH + [
            pltpu.VMEM((TILE_T, HG_WIDTH + CONV_WIDTH), BF16),
            pltpu.VMEM((TILE_T, 4 * CONV_WIDTH), F32),
            pltpu.VMEM((D_MODEL, N_REST_COLS), BF16),
            pltpu.VMEM(w_out.shape, BF16),
            pltpu.VMEM((STAGE_ROWS * STAGE_DEPTH // 4, 4 * HG_WIDTH), F32),
            pltpu.SemaphoreType.DMA((STAGE_DEPTH,)),
        ],
        compiler_params=pltpu.CompilerParams(dimension_semantics=("arbitrary", "arbitrary"),
                                             vmem_limit_bytes=VMEM_LIMIT),
        name="fwd_scan_out",
    )(x, mod3, nw, wi, w_in, lb, sf, q, v, ob, onw, conv_w, w_out, fnw)


def kernel(x, c, ctx, c_ctx, norm_w, w_ada, b_ada, w_in, hg_lb_logits, hg_onorm_w, conv_w,
           w_out, final_norm_w):
    assert w_in.shape[0] == 1, "single-layer block"
    bsz = x.shape[0]
    cc = jnp.concatenate([c, c_ctx[None, :], jnp.zeros((8 - bsz - 1, D_MODEL), F32)], axis=0)
    mod, lb = _mod_call(cc, w_ada[0], b_ada[0][None, :], hg_lb_logits.reshape(-1, HG_WIDTH))
    mod3 = mod.reshape(8, 3, D_MODEL)

    wi = w_in[0, :, :COL_GATE * HG_WIDTH].astype(BF16)
    nw = _pad_rows(norm_w[0][None, :])

    sf, sb = _ctx_call(ctx, mod3, nw, wi, lb)
    q, v, ob = _bwd_call(x, mod3, nw, wi, lb, sb)
    onw = _pad_rows(jnp.tile(hg_onorm_w[0], HEADS)[None, :])
    return _fwd_call(x, mod3, nw, wi, w_in[0], lb, sf, q, v, ob, onw, conv_w[0],
                     w_out[0], _pad_rows(final_norm_w[None, :]))
```

```python
import jax
import jax.numpy as jnp
from jax import lax
from jax.experimental import pallas as pl
from jax.experimental.pallas import tpu as pltpu

D_MODEL = 1024
HEADS = 8
HEAD_DIM = 128
HG_WIDTH = HEADS * HEAD_DIM
CONV_WIDTH = 1024
GRID_W = 64
EPS = 1e-6

CHUNK = 128
BLK = 16
NBLK = CHUNK // BLK
LEVELS = NBLK.bit_length() - 1
N_OPS = 4 + LEVELS
_KEY_OPS = (1,) + tuple(range(4, 4 + LEVELS))
TILE_T = 512
COL_TILE = 256
EXP_RANGE = 115.0
LOG2F_FLOOR = 2.0 * EXP_RANGE / (BLK - 1)
VMEM_CAPACITY = 64 * 1024 * 1024
VMEM_LIMIT = VMEM_CAPACITY - 4 * 1024 * 1024
VMEM_LIMIT_CTX = 32 * 1024 * 1024
VMEM_LIMIT_BWD = 48 * 1024 * 1024
ROW_PAD = 4

COL_Q, COL_V, COL_ZF, COL_ZB, COL_GATE, COL_CONV = 0, 1, 2, 3, 4, 5

F32 = jnp.float32
BF16 = jnp.bfloat16

_TN = (((0,), (0,)), ((), ()))


def _w_in_cols(group):
    return pl.BlockSpec((D_MODEL, HG_WIDTH), lambda *_: (0, group), pipeline_mode=pl.Buffered(1))


def _pad_rows(a):
    return jnp.pad(a, ((0, ROW_PAD - a.shape[0]), (0, 0)))


def _silu(x):
    hx = 0.5 * x
    return hx + hx * jnp.tanh(hx)


def _rms_rows(xf):
    return lax.rsqrt(jnp.mean(xf * xf, axis=-1, keepdims=True) + EPS)


def _tri(n, reverse):
    t = lax.broadcasted_iota(jnp.int32, (n, n), 0)
    r = lax.broadcasted_iota(jnp.int32, (n, n), 1)
    return (r >= t) if reverse else (r <= t)


def _split_bf16(a):
    hi = a.astype(BF16)
    return hi, (a - hi.astype(F32)).astype(BF16)


def _cumsum_time(g, tri2):
    return jnp.dot(tri2, jnp.concatenate(_split_bf16(g), axis=0), preferred_element_type=F32)


def _gates(z, lb):
    b = 0.5 * (1.0 - lb)
    bt = b * jnp.tanh(0.5 * z)
    return jnp.maximum(jnp.log2((1.0 - b) + bt), -LOG2F_FLOOR), b - bt


def _mod_kernel(cc_ref, w_ref, b_ref, lg_ref, mod_ref, lb_ref):
    cc = cc_ref[...]
    s = _silu(cc)
    s_hi, s_lo = _split_bf16(s)
    w_hi, w_lo = _split_bf16(w_ref[...])
    dot = lambda a, b: jnp.dot(a, b, preferred_element_type=F32)
    mod_ref[...] = dot(s_hi, w_hi) + (dot(s_hi, w_lo) + dot(s_lo, w_hi)) + b_ref[...]
    n_rows = lg_ref.shape[0] // 2
    for d in range(2):
        lg = lg_ref[d * n_rows:(d + 1) * n_rows, :]
        e = jnp.exp(lg - jnp.max(lg, axis=0, keepdims=True))
        lb_ref[d:d + 1, :] = e[0:1, :] / jnp.sum(e, axis=0, keepdims=True)
    lb_ref[2:, :] = jnp.zeros((ROW_PAD - 2, lb_ref.shape[1]), F32)


def _mod_call(cc, w_ada, b_ada, lb_logits):
    n_out = w_ada.shape[1]
    bn = 768
    return pl.pallas_call(
        _mod_kernel,
        grid=(n_out // bn,),
        in_specs=[
            pl.BlockSpec((8, D_MODEL), lambda j: (0, 0)),
            pl.BlockSpec((D_MODEL, bn), lambda j: (0, j)),
            pl.BlockSpec((1, bn), lambda j: (0, j)),
            pl.BlockSpec(lb_logits.shape, lambda j: (0, 0)),
        ],
        out_specs=[
            pl.BlockSpec((8, bn), lambda j: (0, j)),
            pl.BlockSpec((ROW_PAD, HG_WIDTH), lambda j: (0, 0)),
        ],
        out_shape=[
            jax.ShapeDtypeStruct((8, n_out), F32),
            jax.ShapeDtypeStruct((ROW_PAD, HG_WIDTH), F32),
        ],
        compiler_params=pltpu.CompilerParams(dimension_semantics=("arbitrary",)),
        name="adaln_mod",
    )(cc, w_ada, b_ada, lb_logits)


def _modulated_norm(x, nw, mod_ref):
    shift = mod_ref[0, 0:1, :]
    scale = mod_ref[0, 1:2, :]
    return ((x * _rms_rows(x)) * (nw * (1.0 + scale)) + shift).astype(BF16)


def _ctx_kernel(x_ref, mod_ref, nw_ref, wv_ref, wzf_ref, wzb_ref, lb_ref, sf_ref, sb_ref):
    n = x_ref.shape[1]
    h = _modulated_norm(x_ref[0], nw_ref[0:1, :], mod_ref)
    v = jnp.dot(h, wv_ref[...], preferred_element_type=F32).astype(BF16)
    for d, wz_ref, out_ref in ((0, wzf_ref, sf_ref), (1, wzb_ref, sb_ref)):
        z = jnp.dot(h, wz_ref[...], preferred_element_type=F32)
        g, kk = _gates(z, lb_ref[d:d + 1, :])
        tri = _tri(n, reverse=bool(d)).astype(BF16)
        bc = _cumsum_time(g, jnp.concatenate([tri, tri], axis=1))
        tot = bc[0:1, :] if d else bc[n - 1:n, :]
        ks = (kk * jnp.exp2(tot - bc)).astype(BF16)
        for hd in range(HEADS):
            sl = slice(hd * HEAD_DIM, (hd + 1) * HEAD_DIM)
            out_ref[0, hd] = lax.dot_general(ks[:, sl], v[:, sl], _TN, preferred_element_type=F32)


def _ctx_call(ctx, mod3, nw, wi, lb):
    bsz, n, _ = ctx.shape
    st_shape = jax.ShapeDtypeStruct((bsz, HEADS, HEAD_DIM, HEAD_DIM), F32)
    st_spec = pl.BlockSpec((1, HEADS, HEAD_DIM, HEAD_DIM), lambda b: (b, 0, 0, 0))
    return pl.pallas_call(
        _ctx_kernel,
        grid=(bsz,),
        in_specs=[
            pl.BlockSpec((1, n, D_MODEL), lambda b: (b, 0, 0)),
            pl.BlockSpec((1, 3, D_MODEL), lambda b: (bsz, 0, 0)),
            pl.BlockSpec((ROW_PAD, D_MODEL), lambda b: (0, 0)),
            _w_in_cols(COL_V), _w_in_cols(COL_ZF), _w_in_cols(COL_ZB),
            pl.BlockSpec((ROW_PAD, HG_WIDTH), lambda b: (0, 0)),
        ],
        out_specs=[st_spec, st_spec],
        out_shape=[st_shape, st_shape],
        compiler_params=pltpu.CompilerParams(dimension_semantics=("arbitrary",),
                                             vmem_limit_bytes=VMEM_LIMIT_CTX),
        name="ctx_states",
    )(ctx, mod3, nw, wi, wi, wi, lb)


def _pivot_row(i, level, reverse):
    half = NBLK >> (level + 1)
    boundary = ((i // (2 * half)) * 2 + 1) * half * BLK
    return boundary if reverse else boundary - 1


def _q_side(i, level, reverse):
    later = (i // (NBLK >> (level + 1))) % 2 == 1
    return later != reverse


def _tiles(r0, n_rows, row_tile, width, col_tile):
    return [(slice(r0 + r, r0 + r + row_tile), slice(c, c + col_tile))
            for r in range(0, n_rows, row_tile) for c in range(0, width, col_tile)]


def _scan_gates(c, z_ref, lb, bc_s, g2_s, tri2):
    for rows, cols in _tiles(c * CHUNK, CHUNK, BLK, HG_WIDTH, COL_TILE):
        g, kk = _gates(z_ref[rows, cols], lb[:, cols])
        g_hi = g.astype(BF16)
        g2_s[0, rows, cols] = g_hi
        g2_s[1, rows, cols] = (g - g_hi.astype(F32)).astype(BF16)
        z_ref[rows, cols] = kk
    rows = slice(c * CHUNK, (c + 1) * CHUNK)
    g2 = jnp.concatenate([g2_s[0, rows, :], g2_s[1, rows, :]], axis=0)
    bc_s[rows, :] = jnp.dot(tri2, g2, preferred_element_type=F32)


def _scan_prep(c, k_ref, q_ref, ops_s, opst_s, bc_s, reverse):
    r0 = c * CHUNK
    blk = lax.broadcasted_iota(jnp.int32, (NBLK, 1), 0)

    def per_block(row_of_block):
        out = None
        for i in reversed(range(NBLK)):
            if i + 1 < NBLK and row_of_block[i] == row_of_block[i + 1]:
                continue
            row = bc_s[r0 + row_of_block[i]:r0 + row_of_block[i] + 1, :]
            out = row if out is None else jnp.where(blk <= i, row, out)
        return jnp.broadcast_to(out, (NBLK, out.shape[1]))

    mid = 0.5 * (per_block([i * BLK for i in range(NBLK)])
                 + per_block([i * BLK + BLK - 1 for i in range(NBLK)]))
    tot_row = r0 if reverse else r0 + CHUNK - 1
    tot = bc_s[tot_row:tot_row + 1, :]
    scale = [jnp.exp2(mid), jnp.exp2(tot - mid)]
    for level in range(LEVELS):
        piv = per_block([_pivot_row(i, level, reverse) for i in range(NBLK)])
        scale.append(jnp.exp2(-jnp.abs(mid - piv)))
    for rows, cols in _tiles(r0, CHUNK, BLK, HG_WIDTH, COL_TILE):
        i = (rows.start - r0) // BLK
        d = bc_s[rows, cols] - mid[i:i + 1, cols]
        qd = q_ref[rows, cols].astype(F32) * jnp.exp2(d)
        kd = k_ref[rows, cols] * jnp.exp2(-d)
        ops_s[0, rows, cols] = qd.astype(BF16)
        ops_s[1, rows, cols] = kd.astype(BF16)
        ops_s[2, rows, cols] = (qd * scale[0][i:i + 1, cols]).astype(BF16)
        ops_s[3, rows, cols] = (kd * scale[1][i:i + 1, cols]).astype(BF16)
        for level in range(LEVELS):
            side = qd if _q_side(i, level, reverse) else kd
            ops_s[4 + level, rows, cols] = (side * scale[2 + level][i:i + 1, cols]).astype(BF16)
    rows = slice(r0, r0 + CHUNK)
    for hd in range(HEADS):
        sl = slice(hd * HEAD_DIM, (hd + 1) * HEAD_DIM)
        for j, n in enumerate(_KEY_OPS):
            opst_s[j, c * HEADS + hd] = ops_s[n, rows, sl].T
    return jnp.exp2(tot)


def _pair_masks(reverse):
    t = lax.broadcasted_iota(jnp.int32, (CHUNK, CHUNK), 0)
    s = lax.broadcasted_iota(jnp.int32, (CHUNK, CHUNK), 1)
    early, late = (t, s) if reverse else (s, t)
    masks = [((t // BLK) == (s // BLK)) & (early <= late)]
    for level in range(LEVELS):
        half = CHUNK >> (level + 1)
        masks.append(((late // half) == (early // half) + 1) & ((early // half) % 2 == 0))
    return masks


def _scan_mm(c, dec, v_ref, st_ref, ops_s, opst_s, masks, emit, reverse):
    r0 = c * CHUNK
    rows = slice(r0, r0 + CHUNK)
    q_blocks = [[i for i in range(NBLK) if _q_side(i, level, reverse)] for level in range(LEVELS)]
    attn = []
    for hd in range(HEADS):
        sl = slice(hd * HEAD_DIM, (hd + 1) * HEAD_DIM)
        same = jnp.dot(ops_s[0, rows, sl], opst_s[0, c * HEADS + hd], preferred_element_type=F32)
        cross = []
        for level in range(LEVELS):
            lhs = jnp.concatenate([ops_s[4 + level, r0 + i * BLK:r0 + (i + 1) * BLK, sl]
                                   for i in q_blocks[level]], axis=0)
            cross.append(jnp.dot(lhs, opst_s[1 + level, c * HEADS + hd],
                                 preferred_element_type=F32))
        blocks = []
        for i in range(NBLK):
            blk = slice(i * BLK, (i + 1) * BLK)
            a = jnp.where(masks[0][blk], same[blk], 0.0)
            for level in range(LEVELS):
                if i in q_blocks[level]:
                    j = q_blocks[level].index(i)
                    a = jnp.where(masks[1 + level][blk], cross[level][j * BLK:(j + 1) * BLK], a)
            blocks.append(a)
        attn.append(jnp.concatenate(blocks, axis=0).astype(BF16))
    for hd in range(HEADS):
        sl = slice(hd * HEAD_DIM, (hd + 1) * HEAD_DIM)
        st = st_ref[hd]
        v = v_ref[rows, sl]
        lhs = jnp.concatenate([ops_s[2, rows, sl], attn[hd]], axis=1)
        rhs = jnp.concatenate([st.astype(BF16), v], axis=0)
        emit(rows, sl, jnp.dot(lhs, rhs, preferred_element_type=F32))
        kv = lax.dot_general(ops_s[3, rows, sl], v, _TN, preferred_element_type=F32)
        dec_col = jnp.transpose(jnp.broadcast_to(dec[:, sl], (HEAD_DIM, HEAD_DIM)))
        st_ref[hd] = st * dec_col + kv


def _scan_tile(z_ref, q_ref, v_ref, lb, st_ref, ops_s, opst_s, bc_s, g2_s, emit, reverse,
               after_gates=None, before_matmuls=None, after_chunk=None):
    n_chunks = z_ref.shape[0] // CHUNK
    order = range(n_chunks - 1, -1, -1) if reverse else range(n_chunks)
    tri_b = _tri(CHUNK, reverse).astype(BF16)
    tri2 = jnp.concatenate([tri_b, tri_b], axis=1)
    masks = _pair_masks(reverse)
    for c in order:
        _scan_gates(c, z_ref, lb, bc_s, g2_s, tri2)
    if after_gates is not None:
        after_gates()
    dec = {c: _scan_prep(c, z_ref, q_ref, ops_s, opst_s, bc_s, reverse) for c in order}
    if before_matmuls is not None:
        before_matmuls()
    for c in order:
        _scan_mm(c, dec[c], v_ref, st_ref, ops_s, opst_s, masks, emit, reverse)
        if after_chunk is not None:
            after_chunk(slice(c * CHUNK, (c + 1) * CHUNK))


def _init_state(s0_ref, st_s):
    @pl.when(pl.program_id(1) == 0)
    def _():
        st_s[...] = s0_ref[0]


_SCAN_SCRATCH = [
    pltpu.VMEM((HEADS, HEAD_DIM, HEAD_DIM), F32),
    pltpu.VMEM((N_OPS, TILE_T, HG_WIDTH), BF16),
    pltpu.VMEM((len(_KEY_OPS), TILE_T // CHUNK * HEADS, HEAD_DIM, CHUNK), BF16),
    pltpu.VMEM((TILE_T, HG_WIDTH), F32),
    pltpu.VMEM((TILE_T, HG_WIDTH), F32),
    pltpu.VMEM((2, TILE_T, HG_WIDTH), BF16),
]


def _bwd_kernel(x_ref, mod_ref, nw_ref, wq_ref, wv_ref, wz_ref, lb_ref, s0_ref,
                q_ref, v_ref, ob_ref, st_s, ops_s, opst_s, bc_s, z_s, g2_s):
    _init_state(s0_ref, st_s)
    h = _modulated_norm(x_ref[0], nw_ref[0:1, :], mod_ref)
    z_s[...] = jnp.dot(h, wz_ref[...], preferred_element_type=F32)
    q_ref[0] = jnp.dot(h, wq_ref[...], preferred_element_type=F32).astype(BF16)

    def value_proj():
        v_ref[0] = jnp.dot(h, wv_ref[...], preferred_element_type=F32).astype(BF16)

    def emit(rows, sl, o):
        ob_ref[0, rows, sl] = o.astype(BF16)

    _scan_tile(z_s, q_ref.at[0], v_ref.at[0], lb_ref[1:2, :], st_s, ops_s, opst_s, bc_s, g2_s, emit,
               reverse=True,
               after_gates=value_proj)


def _bwd_call(x, mod3, nw, wi, lb, sb):
    bsz, seq, _ = x.shape
    nt = seq // TILE_T
    tile = lambda b, t: (b, nt - 1 - t, 0)
    act = jax.ShapeDtypeStruct((bsz, seq, HG_WIDTH), BF16)
    act_spec = pl.BlockSpec((1, TILE_T, HG_WIDTH), tile)
    return pl.pallas_call(
        _bwd_kernel,
        grid=(bsz, nt),
        in_specs=[
            pl.BlockSpec((1, TILE_T, D_MODEL), tile),
            pl.BlockSpec((1, 3, D_MODEL), lambda b, t: (b, 0, 0)),
            pl.BlockSpec((ROW_PAD, D_MODEL), lambda b, t: (0, 0)),
            _w_in_cols(COL_Q), _w_in_cols(COL_V), _w_in_cols(COL_ZB),
            pl.BlockSpec((ROW_PAD, HG_WIDTH), lambda b, t: (0, 0)),
            pl.BlockSpec((1, HEADS, HEAD_DIM, HEAD_DIM), lambda b, t: (b, 0, 0, 0)),
        ],
        out_specs=[act_spec, act_spec, act_spec],
        out_shape=[act, act, act],
        scratch_shapes=_SCAN_SCRATCH,
        compiler_params=pltpu.CompilerParams(dimension_semantics=("arbitrary", "arbitrary"),
                                             vmem_limit_bytes=VMEM_LIMIT_BWD),
        name="bwd_scan",
    )(x, mod3, nw, wi, wi, wi, lb, sb)


def _fwd_kernel(x_ref, mod_ref, nw_ref, wz_ref, wg_ref, wc0_ref, wc1_ref, wc2_ref, wc3_ref,
                lb_ref, s0_ref, q_ref, v_ref, ob_ref, onw_ref, cw_ref, wo_ref, fnw_ref, out_ref,
                st_s, ops_s, opst_s, bc_s, z_s, g2_s, mix_s, cv_s):
    _init_state(s0_ref, st_s)
    g_s = z_s
    o_s = bc_s
    h = _modulated_norm(x_ref[0], nw_ref[0:1, :], mod_ref)

    z_s[...] = jnp.dot(h, wz_ref[...], preferred_element_type=F32)

    def conv_proj(j, wc_ref):
        cv_s[:, j * CONV_WIDTH:(j + 1) * CONV_WIDTH] = jnp.dot(h, wc_ref[...],
                                                               preferred_element_type=F32)

    conv_proj(0, wc0_ref)

    def conv_proj_rest():
        for j, wc_ref in enumerate((wc1_ref, wc2_ref, wc3_ref), start=1):
            conv_proj(j, wc_ref)

    def conv_branch():
        g_s[...] = jnp.dot(h, wg_ref[...], preferred_element_type=F32)
        n_tok = cv_s.shape[0]
        u = cv_s[:, 2 * CONV_WIDTH:3 * CONV_WIDTH] * cv_s[:, 0:CONV_WIDTH]
        col = lax.broadcasted_iota(jnp.int32, (n_tok, 1), 0) % GRID_W
        u_prev = jnp.where(col == 0, 0.0, pltpu.roll(u, 1, axis=0))
        u_next = jnp.where(col == GRID_W - 1, 0.0, pltpu.roll(u, n_tok - 1, axis=0))
        conv = cw_ref[0:1, :] * u_prev + cw_ref[1:2, :] * u + cw_ref[2:3, :] * u_next
        cg = cv_s[:, 3 * CONV_WIDTH:4 * CONV_WIDTH]
        y_cv = cv_s[:, CONV_WIDTH:2 * CONV_WIDTH] * conv * _silu(cg)
        mix_s[:, HG_WIDTH:] = y_cv.astype(BF16)

    def emit(rows, sl, o):
        o_s[rows, sl] = o + ob_ref[0, rows, sl].astype(F32)

    def gated_readout(rows):
        onw = onw_ref[0:1, :]
        for hd in range(HEADS):
            sl = slice(hd * HEAD_DIM, (hd + 1) * HEAD_DIM)
            o = o_s[rows, sl]
            g = g_s[rows, sl]
            y = (o * _rms_rows(o) * onw[:, sl]) * _silu(g)
            mix_s[rows, sl] = y.astype(BF16)

    _scan_tile(z_s, q_ref.at[0], v_ref.at[0], lb_ref[0:1, :], st_s, ops_s, opst_s, bc_s, g2_s, emit,
               reverse=False,
               after_gates=conv_proj_rest, before_matmuls=conv_branch, after_chunk=gated_readout)

    half = x_ref.shape[1] // 2
    for rows in (slice(0, half), slice(half, 2 * half)):
        y = jnp.dot(mix_s[rows, :], wo_ref[...], preferred_element_type=F32)
        r = x_ref[0, rows, :] + mod_ref[0, 2:3, :] * y
        out_ref[0, rows, :] = (r * _rms_rows(r)) * fnw_ref[0:1, :]


def _fwd_call(x, mod3, nw, wi, lb, sf, q, v, ob, onw, conv_w, w_o, fnw):
    bsz, seq, _ = x.shape
    nt = seq // TILE_T
    tile = lambda b, t: (b, t, 0)
    const2 = lambda b, t: (0, 0)
    act_spec = pl.BlockSpec((1, TILE_T, HG_WIDTH), tile)
    return pl.pallas_call(
        _fwd_kernel,
        grid=(bsz, nt),
        in_specs=[
            pl.BlockSpec((1, TILE_T, D_MODEL), tile),
            pl.BlockSpec((1, 3, D_MODEL), lambda b, t: (b, 0, 0)),
            pl.BlockSpec((ROW_PAD, D_MODEL), const2),
            _w_in_cols(COL_ZF), _w_in_cols(COL_GATE),
            _w_in_cols(COL_CONV), _w_in_cols(COL_CONV + 1), _w_in_cols(COL_CONV + 2),
            _w_in_cols(COL_CONV + 3),
            pl.BlockSpec((ROW_PAD, HG_WIDTH), const2),
            pl.BlockSpec((1, HEADS, HEAD_DIM, HEAD_DIM), lambda b, t: (b, 0, 0, 0)),
            act_spec, act_spec, act_spec,
            pl.BlockSpec((ROW_PAD, HG_WIDTH), const2),
            pl.BlockSpec(conv_w.shape, const2),
            pl.BlockSpec(w_o.shape, const2, pipeline_mode=pl.Buffered(1)),
            pl.BlockSpec((ROW_PAD, D_MODEL), const2),
        ],
        out_specs=pl.BlockSpec((1, TILE_T, D_MODEL), tile),
        out_shape=jax.ShapeDtypeStruct((bsz, seq, D_MODEL), F32),
        scratch_shapes=_SCAN_SCRATCH + [
            pltpu.VMEM((TILE_T, HG_WIDTH + CONV_WIDTH), BF16),
            pltpu.VMEM((TILE_T, 4 * CONV_WIDTH), F32),
        ],
        compiler_params=pltpu.CompilerParams(dimension_semantics=("arbitrary", "arbitrary"),
                                             vmem_limit_bytes=VMEM_LIMIT),
        name="fwd_scan_out",
    )(x, mod3, nw, wi, wi, wi, wi, wi, wi, lb, sf, q, v, ob, onw, conv_w, w_o, fnw)


def kernel(x, c, ctx, c_ctx, norm_w, w_ada, b_ada, w_in, hg_lb_logits, hg_onorm_w, conv_w,
           w_out, final_norm_w):
    assert w_in.shape[0] == 1, "single-layer block"
    bsz = x.shape[0]
    cc = jnp.concatenate([c, c_ctx[None, :], jnp.zeros((8 - bsz - 1, D_MODEL), F32)], axis=0)
    mod, lb = _mod_call(cc, w_ada[0], b_ada[0][None, :], hg_lb_logits.reshape(-1, HG_WIDTH))
    mod3 = mod.reshape(8, 3, D_MODEL)

    wi = w_in[0].astype(BF16)
    nw = _pad_rows(norm_w[0][None, :])

    sf, sb = _ctx_call(ctx, mod3, nw, wi, lb)
    q, v, ob = _bwd_call(x, mod3, nw, wi, lb, sb)
    onw = _pad_rows(jnp.tile(hg_onorm_w[0], HEADS)[None, :])
    return _fwd_call(x, mod3, nw, wi, lb, sf, q, v, ob, onw, conv_w[0],
                     w_out[0].astype(BF16), _pad_rows(final_norm_w[None, :]))
```

```python
import jax
import jax.numpy as jnp
from jax import lax
from jax.experimental import pallas as pl
from jax.experimental.pallas import tpu as pltpu

D_MODEL = 1024
HEADS = 8
HEAD_DIM = 128
HG_WIDTH = HEADS * HEAD_DIM
CONV_WIDTH = 1024
GRID_W = 64
EPS = 1e-6

CHUNK = 128
BLK = 16
NBLK = CHUNK // BLK
LEVELS = NBLK.bit_length() - 1
N_OPS = 4 + LEVELS
_KEY_OPS = (1,) + tuple(range(4, 4 + LEVELS))
TILE_T = 512
COL_TILE = 256
EXP_RANGE = 115.0
LOG2F_FLOOR = 2.0 * EXP_RANGE / (BLK - 1)
VMEM_CAPACITY = 64 * 1024 * 1024
VMEM_LIMIT = VMEM_CAPACITY - 7 * 1024 * 1024
VMEM_LIMIT_CTX = 32 * 1024 * 1024
VMEM_LIMIT_BWD = 48 * 1024 * 1024
ROW_PAD = 4

COL_Q, COL_V, COL_ZF, COL_ZB, COL_GATE, COL_CONV = 0, 1, 2, 3, 4, 5

F32 = jnp.float32
BF16 = jnp.bfloat16

_TN = (((0,), (0,)), ((), ()))


def _w_in_cols(group):
    return pl.BlockSpec((D_MODEL, HG_WIDTH), lambda *_: (0, group), pipeline_mode=pl.Buffered(1))


def _pad_rows(a):
    return jnp.pad(a, ((0, ROW_PAD - a.shape[0]), (0, 0)))


def _silu(x):
    hx = 0.5 * x
    return hx + hx * jnp.tanh(hx)


def _rms_rows(xf):
    return lax.rsqrt(jnp.mean(xf * xf, axis=-1, keepdims=True) + EPS)


def _tri(n, reverse):
    t = lax.broadcasted_iota(jnp.int32, (n, n), 0)
    r = lax.broadcasted_iota(jnp.int32, (n, n), 1)
    return (r >= t) if reverse else (r <= t)


def _split_bf16(a):
    hi = a.astype(BF16)
    return hi, (a - hi.astype(F32)).astype(BF16)


def _cumsum_time(g, tri2):
    return jnp.dot(tri2, jnp.concatenate(_split_bf16(g), axis=0), preferred_element_type=F32)


def _gates(z, lb):
    b = 0.5 * (1.0 - lb)
    bt = b * jnp.tanh(0.5 * z)
    return jnp.maximum(jnp.log2((1.0 - b) + bt), -LOG2F_FLOOR), b - bt


def _mod_kernel(cc_ref, w_ref, b_ref, lg_ref, mod_ref, lb_ref):
    cc = cc_ref[...]
    s = _silu(cc)
    s_hi, s_lo = _split_bf16(s)
    w_hi, w_lo = _split_bf16(w_ref[...])
    dot = lambda a, b: jnp.dot(a, b, preferred_element_type=F32)
    mod_ref[...] = dot(s_hi, w_hi) + (dot(s_hi, w_lo) + dot(s_lo, w_hi)) + b_ref[...]
    n_rows = lg_ref.shape[0] // 2
    for d in range(2):
        lg = lg_ref[d * n_rows:(d + 1) * n_rows, :]
        e = jnp.exp(lg - jnp.max(lg, axis=0, keepdims=True))
        lb_ref[d:d + 1, :] = e[0:1, :] / jnp.sum(e, axis=0, keepdims=True)
    lb_ref[2:, :] = jnp.zeros((ROW_PAD - 2, lb_ref.shape[1]), F32)


def _mod_call(cc, w_ada, b_ada, lb_logits):
    n_out = w_ada.shape[1]
    bn = 768
    return pl.pallas_call(
        _mod_kernel,
        grid=(n_out // bn,),
        in_specs=[
            pl.BlockSpec((8, D_MODEL), lambda j: (0, 0)),
            pl.BlockSpec((D_MODEL, bn), lambda j: (0, j)),
            pl.BlockSpec((1, bn), lambda j: (0, j)),
            pl.BlockSpec(lb_logits.shape, lambda j: (0, 0)),
        ],
        out_specs=[
            pl.BlockSpec((8, bn), lambda j: (0, j)),
            pl.BlockSpec((ROW_PAD, HG_WIDTH), lambda j: (0, 0)),
        ],
        out_shape=[
            jax.ShapeDtypeStruct((8, n_out), F32),
            jax.ShapeDtypeStruct((ROW_PAD, HG_WIDTH), F32),
        ],
        compiler_params=pltpu.CompilerParams(dimension_semantics=("arbitrary",)),
        name="adaln_mod",
    )(cc, w_ada, b_ada, lb_logits)


def _modulated_norm(x, nw, mod_ref):
    shift = mod_ref[0, 0:1, :]
    scale = mod_ref[0, 1:2, :]
    return ((x * _rms_rows(x)) * (nw * (1.0 + scale)) + shift).astype(BF16)


def _ctx_kernel(x_ref, mod_ref, nw_ref, wv_ref, wzf_ref, wzb_ref, lb_ref, sf_ref, sb_ref):
    n = x_ref.shape[1]
    h = _modulated_norm(x_ref[0], nw_ref[0:1, :], mod_ref)
    v = jnp.dot(h, wv_ref[...], preferred_element_type=F32).astype(BF16)
    for d, wz_ref, out_ref in ((0, wzf_ref, sf_ref), (1, wzb_ref, sb_ref)):
        z = jnp.dot(h, wz_ref[...], preferred_element_type=F32)
        g, kk = _gates(z, lb_ref[d:d + 1, :])
        tri = _tri(n, reverse=bool(d)).astype(BF16)
        bc = _cumsum_time(g, jnp.concatenate([tri, tri], axis=1))
        tot = bc[0:1, :] if d else bc[n - 1:n, :]
        ks = (kk * jnp.exp2(tot - bc)).astype(BF16)
        for hd in range(HEADS):
            sl = slice(hd * HEAD_DIM, (hd + 1) * HEAD_DIM)
            out_ref[0, hd] = lax.dot_general(ks[:, sl], v[:, sl], _TN, preferred_element_type=F32)


def _ctx_call(ctx, mod3, nw, wi, lb):
    bsz, n, _ = ctx.shape
    st_shape = jax.ShapeDtypeStruct((bsz, HEADS, HEAD_DIM, HEAD_DIM), F32)
    st_spec = pl.BlockSpec((1, HEADS, HEAD_DIM, HEAD_DIM), lambda b: (b, 0, 0, 0))
    return pl.pallas_call(
        _ctx_kernel,
        grid=(bsz,),
        in_specs=[
            pl.BlockSpec((1, n, D_MODEL), lambda b: (b, 0, 0)),
            pl.BlockSpec((1, 3, D_MODEL), lambda b: (bsz, 0, 0)),
            pl.BlockSpec((ROW_PAD, D_MODEL), lambda b: (0, 0)),
            _w_in_cols(COL_V), _w_in_cols(COL_ZF), _w_in_cols(COL_ZB),
            pl.BlockSpec((ROW_PAD, HG_WIDTH), lambda b: (0, 0)),
        ],
        out_specs=[st_spec, st_spec],
        out_shape=[st_shape, st_shape],
        compiler_params=pltpu.CompilerParams(dimension_semantics=("arbitrary",),
                                             vmem_limit_bytes=VMEM_LIMIT_CTX),
        name="ctx_states",
    )(ctx, mod3, nw, wi, wi, wi, lb)


def _pivot_row(i, level, reverse):
    half = NBLK >> (level + 1)
    boundary = ((i // (2 * half)) * 2 + 1) * half * BLK
    return boundary if reverse else boundary - 1


def _q_side(i, level, reverse):
    later = (i // (NBLK >> (level + 1))) % 2 == 1
    return later != reverse


def _tiles(r0, n_rows, row_tile, width, col_tile):
    return [(slice(r0 + r, r0 + r + row_tile), slice(c, c + col_tile))
            for r in range(0, n_rows, row_tile) for c in range(0, width, col_tile)]


def _scan_gates(c, z_ref, lb, bc_s, g2_s, tri2):
    for rows, cols in _tiles(c * CHUNK, CHUNK, BLK, HG_WIDTH, COL_TILE):
        g, kk = _gates(z_ref[rows, cols], lb[:, cols])
        g_hi = g.astype(BF16)
        g2_s[0, rows, cols] = g_hi
        g2_s[1, rows, cols] = (g - g_hi.astype(F32)).astype(BF16)
        z_ref[rows, cols] = kk
    rows = slice(c * CHUNK, (c + 1) * CHUNK)
    g2 = jnp.concatenate([g2_s[0, rows, :], g2_s[1, rows, :]], axis=0)
    bc_s[rows, :] = jnp.dot(tri2, g2, preferred_element_type=F32)


def _scan_prep(c, k_ref, q_ref, ops_s, opst_s, bc_s, reverse):
    r0 = c * CHUNK
    blk = lax.broadcasted_iota(jnp.int32, (NBLK, 1), 0)

    def per_block(row_of_block):
        out = None
        for i in reversed(range(NBLK)):
            if i + 1 < NBLK and row_of_block[i] == row_of_block[i + 1]:
                continue
            row = bc_s[r0 + row_of_block[i]:r0 + row_of_block[i] + 1, :]
            out = row if out is None else jnp.where(blk <= i, row, out)
        return jnp.broadcast_to(out, (NBLK, out.shape[1]))

    mid = 0.5 * (per_block([i * BLK for i in range(NBLK)])
                 + per_block([i * BLK + BLK - 1 for i in range(NBLK)]))
    tot_row = r0 if reverse else r0 + CHUNK - 1
    tot = bc_s[tot_row:tot_row + 1, :]
    scale = [jnp.exp2(mid), jnp.exp2(tot - mid)]
    for level in range(LEVELS):
        piv = per_block([_pivot_row(i, level, reverse) for i in range(NBLK)])
        scale.append(jnp.exp2(-jnp.abs(mid - piv)))
    for rows, cols in _tiles(r0, CHUNK, BLK, HG_WIDTH, COL_TILE):
        i = (rows.start - r0) // BLK
        d = bc_s[rows, cols] - mid[i:i + 1, cols]
        qd = q_ref[rows, cols].astype(F32) * jnp.exp2(d)
        kd = k_ref[rows, cols] * jnp.exp2(-d)
        ops_s[0, rows, cols] = qd.astype(BF16)
        ops_s[1, rows, cols] = kd.astype(BF16)
        ops_s[2, rows, cols] = (qd * scale[0][i:i + 1, cols]).astype(BF16)
        ops_s[3, rows, cols] = (kd * scale[1][i:i + 1, cols]).astype(BF16)
        for level in range(LEVELS):
            side = qd if _q_side(i, level, reverse) else kd
            ops_s[4 + level, rows, cols] = (side * scale[2 + level][i:i + 1, cols]).astype(BF16)
    rows = slice(r0, r0 + CHUNK)
    for hd in range(HEADS):
        sl = slice(hd * HEAD_DIM, (hd + 1) * HEAD_DIM)
        for j, n in enumerate(_KEY_OPS):
            opst_s[j, c * HEADS + hd] = ops_s[n, rows, sl].T
    return jnp.exp2(tot)


def _pair_masks(reverse):
    t = lax.broadcasted_iota(jnp.int32, (CHUNK, CHUNK), 0)
    s = lax.broadcasted_iota(jnp.int32, (CHUNK, CHUNK), 1)
    early, late = (t, s) if reverse else (s, t)
    masks = [((t // BLK) == (s // BLK)) & (early <= late)]
    for level in range(LEVELS):
        half = CHUNK >> (level + 1)
        masks.append(((late // half) == (early // half) + 1) & ((early // half) % 2 == 0))
    return masks


def _scan_mm(c, dec, v_ref, st_ref, ops_s, opst_s, masks, emit, reverse):
    r0 = c * CHUNK
    rows = slice(r0, r0 + CHUNK)
    q_blocks = [[i for i in range(NBLK) if _q_side(i, level, reverse)] for level in range(LEVELS)]
    attn = []
    for hd in range(HEADS):
        sl = slice(hd * HEAD_DIM, (hd + 1) * HEAD_DIM)
        same = jnp.dot(ops_s[0, rows, sl], opst_s[0, c * HEADS + hd], preferred_element_type=F32)
        cross = []
        for level in range(LEVELS):
            lhs = jnp.concatenate([ops_s[4 + level, r0 + i * BLK:r0 + (i + 1) * BLK, sl]
                                   for i in q_blocks[level]], axis=0)
            cross.append(jnp.dot(lhs, opst_s[1 + level, c * HEADS + hd],
                                 preferred_element_type=F32))
        blocks = []
        for i in range(NBLK):
            blk = slice(i * BLK, (i + 1) * BLK)
            a = jnp.where(masks[0][blk], same[blk], 0.0)
            for level in range(LEVELS):
                if i in q_blocks[level]:
                    j = q_blocks[level].index(i)
                    a = jnp.where(masks[1 + level][blk], cross[level][j * BLK:(j + 1) * BLK], a)
            blocks.append(a)
        attn.append(jnp.concatenate(blocks, axis=0).astype(BF16))
    for hd in range(HEADS):
        sl = slice(hd * HEAD_DIM, (hd + 1) * HEAD_DIM)
        st = st_ref[hd]
        v = v_ref[rows, sl]
        lhs = jnp.concatenate([ops_s[2, rows, sl], attn[hd]], axis=1)
        rhs = jnp.concatenate([st.astype(BF16), v], axis=0)
        emit(rows, sl, jnp.dot(lhs, rhs, preferred_element_type=F32))
        kv = lax.dot_general(ops_s[3, rows, sl], v, _TN, preferred_element_type=F32)
        dec_col = jnp.transpose(jnp.broadcast_to(dec[:, sl], (HEAD_DIM, HEAD_DIM)))
        st_ref[hd] = st * dec_col + kv


def _scan_tile(z_ref, q_ref, v_ref, lb, st_ref, ops_s, opst_s, bc_s, g2_s, emit, reverse,
               after_gates=None, before_matmuls=None, after_chunk=None):
    n_chunks = z_ref.shape[0] // CHUNK
    order = range(n_chunks - 1, -1, -1) if reverse else range(n_chunks)
    tri_b = _tri(CHUNK, reverse).astype(BF16)
    tri2 = jnp.concatenate([tri_b, tri_b], axis=1)
    masks = _pair_masks(reverse)
    for c in order:
        _scan_gates(c, z_ref, lb, bc_s, g2_s, tri2)
    if after_gates is not None:
        after_gates()
    dec = {c: _scan_prep(c, z_ref, q_ref, ops_s, opst_s, bc_s, reverse) for c in order}
    if before_matmuls is not None:
        before_matmuls()
    for c in order:
        _scan_mm(c, dec[c], v_ref, st_ref, ops_s, opst_s, masks, emit, reverse)
        if after_chunk is not None:
            after_chunk(slice(c * CHUNK, (c + 1) * CHUNK))


def _init_state(s0_ref, st_s):
    @pl.when(pl.program_id(1) == 0)
    def _():
        st_s[...] = s0_ref[0]


_SCAN_SCRATCH = [
    pltpu.VMEM((HEADS, HEAD_DIM, HEAD_DIM), F32),
    pltpu.VMEM((N_OPS, TILE_T, HG_WIDTH), BF16),
    pltpu.VMEM((len(_KEY_OPS), TILE_T // CHUNK * HEADS, HEAD_DIM, CHUNK), BF16),
    pltpu.VMEM((TILE_T, HG_WIDTH), F32),
    pltpu.VMEM((TILE_T, HG_WIDTH), F32),
    pltpu.VMEM((2, TILE_T, HG_WIDTH), BF16),
]


def _bwd_kernel(x_ref, mod_ref, nw_ref, wq_ref, wv_ref, wz_ref, lb_ref, s0_ref,
                q_ref, v_ref, ob_ref, st_s, ops_s, opst_s, bc_s, z_s, g2_s):
    _init_state(s0_ref, st_s)
    h = _modulated_norm(x_ref[0], nw_ref[0:1, :], mod_ref)
    z_s[...] = jnp.dot(h, wz_ref[...], preferred_element_type=F32)
    q_ref[0] = jnp.dot(h, wq_ref[...], preferred_element_type=F32).astype(BF16)

    def value_proj():
        v_ref[0] = jnp.dot(h, wv_ref[...], preferred_element_type=F32).astype(BF16)

    def emit(rows, sl, o):
        ob_ref[0, rows, sl] = o.astype(BF16)

    _scan_tile(z_s, q_ref.at[0], v_ref.at[0], lb_ref[1:2, :], st_s, ops_s, opst_s, bc_s, g2_s, emit,
               reverse=True,
               after_gates=value_proj)


def _bwd_call(x, mod3, nw, wi, lb, sb):
    bsz, seq, _ = x.shape
    nt = seq // TILE_T
    tile = lambda b, t: (b, nt - 1 - t, 0)
    act = jax.ShapeDtypeStruct((bsz, seq, HG_WIDTH), BF16)
    act_spec = pl.BlockSpec((1, TILE_T, HG_WIDTH), tile)
    return pl.pallas_call(
        _bwd_kernel,
        grid=(bsz, nt),
        in_specs=[
            pl.BlockSpec((1, TILE_T, D_MODEL), tile),
            pl.BlockSpec((1, 3, D_MODEL), lambda b, t: (b, 0, 0)),
            pl.BlockSpec((ROW_PAD, D_MODEL), lambda b, t: (0, 0)),
            _w_in_cols(COL_Q), _w_in_cols(COL_V), _w_in_cols(COL_ZB),
            pl.BlockSpec((ROW_PAD, HG_WIDTH), lambda b, t: (0, 0)),
            pl.BlockSpec((1, HEADS, HEAD_DIM, HEAD_DIM), lambda b, t: (b, 0, 0, 0)),
        ],
        out_specs=[act_spec, act_spec, act_spec],
        out_shape=[act, act, act],
        scratch_shapes=_SCAN_SCRATCH,
        compiler_params=pltpu.CompilerParams(dimension_semantics=("arbitrary", "arbitrary"),
                                             vmem_limit_bytes=VMEM_LIMIT_BWD),
        name="bwd_scan",
    )(x, mod3, nw, wi, wi, wi, lb, sb)


def _fwd_kernel(x_ref, mod_ref, nw_ref, wz_ref, wg_ref, wc0_ref, wc1_ref, wc2_ref, wc3_ref,
                lb_ref, s0_ref, q_ref, v_ref, ob_ref, onw_ref, cw_ref, wo_ref, fnw_ref, out_ref,
                st_s, ops_s, opst_s, bc_s, z_s, g2_s, mix_s, cv_s):
    _init_state(s0_ref, st_s)
    g_s = z_s
    o_s = bc_s
    h = _modulated_norm(x_ref[0], nw_ref[0:1, :], mod_ref)

    z_s[...] = jnp.dot(h, wz_ref[...], preferred_element_type=F32)

    def conv_proj(j, wc_ref):
        cv_s[:, j * CONV_WIDTH:(j + 1) * CONV_WIDTH] = jnp.dot(h, wc_ref[...],
                                                               preferred_element_type=F32)

    conv_proj(0, wc0_ref)

    def conv_proj_rest():
        for j, wc_ref in enumerate((wc1_ref, wc2_ref, wc3_ref), start=1):
            conv_proj(j, wc_ref)

    def conv_branch():
        g_s[...] = jnp.dot(h, wg_ref[...], preferred_element_type=F32)
        n_tok = cv_s.shape[0]
        u = cv_s[:, 2 * CONV_WIDTH:3 * CONV_WIDTH] * cv_s[:, 0:CONV_WIDTH]
        col = lax.broadcasted_iota(jnp.int32, (n_tok, 1), 0) % GRID_W
        u_prev = jnp.where(col == 0, 0.0, pltpu.roll(u, 1, axis=0))
        u_next = jnp.where(col == GRID_W - 1, 0.0, pltpu.roll(u, n_tok - 1, axis=0))
        conv = cw_ref[0:1, :] * u_prev + cw_ref[1:2, :] * u + cw_ref[2:3, :] * u_next
        cg = cv_s[:, 3 * CONV_WIDTH:4 * CONV_WIDTH]
        y_cv = cv_s[:, CONV_WIDTH:2 * CONV_WIDTH] * conv * _silu(cg)
        mix_s[:, HG_WIDTH:] = y_cv.astype(BF16)

    def emit(rows, sl, o):
        o_s[rows, sl] = o + ob_ref[0, rows, sl].astype(F32)

    def gated_readout(rows):
        onw = onw_ref[0:1, :]
        for hd in range(HEADS):
            sl = slice(hd * HEAD_DIM, (hd + 1) * HEAD_DIM)
            o = o_s[rows, sl]
            g = g_s[rows, sl]
            y = (o * _rms_rows(o) * onw[:, sl]) * _silu(g)
            mix_s[rows, sl] = y.astype(BF16)

    _scan_tile(z_s, q_ref.at[0], v_ref.at[0], lb_ref[0:1, :], st_s, ops_s, opst_s, bc_s, g2_s, emit,
               reverse=False,
               after_gates=conv_proj_rest, before_matmuls=conv_branch, after_chunk=gated_readout)

    half = x_ref.shape[1] // 2
    for rows in (slice(0, half), slice(half, 2 * half)):
        y = jnp.dot(mix_s[rows, :], wo_ref[...], preferred_element_type=F32)
        r = x_ref[0, rows, :] + mod_ref[0, 2:3, :] * y
        out_ref[0, rows, :] = (r * _rms_rows(r)) * fnw_ref[0:1, :]


def _fwd_call(x, mod3, nw, wi, lb, sf, q, v, ob, onw, conv_w, w_o, fnw):
    bsz, seq, _ = x.shape
    nt = seq // TILE_T
    tile = lambda b, t: (b, t, 0)
    const2 = lambda b, t: (0, 0)
    act_spec = pl.BlockSpec((1, TILE_T, HG_WIDTH), tile)
    return pl.pallas_call(
        _fwd_kernel,
        grid=(bsz, nt),
        in_specs=[
            pl.BlockSpec((1, TILE_T, D_MODEL), tile),
            pl.BlockSpec((1, 3, D_MODEL), lambda b, t: (b, 0, 0)),
            pl.BlockSpec((ROW_PAD, D_MODEL), const2),
            _w_in_cols(COL_ZF), _w_in_cols(COL_GATE),
            _w_in_cols(COL_CONV), _w_in_cols(COL_CONV + 1), _w_in_cols(COL_CONV + 2),
            _w_in_cols(COL_CONV + 3),
            pl.BlockSpec((ROW_PAD, HG_WIDTH), const2),
            pl.BlockSpec((1, HEADS, HEAD_DIM, HEAD_DIM), lambda b, t: (b, 0, 0, 0)),
            act_spec, act_spec, act_spec,
            pl.BlockSpec((ROW_PAD, HG_WIDTH), const2),
            pl.BlockSpec(conv_w.shape, const2),
            pl.BlockSpec(w_o.shape, const2, pipeline_mode=pl.Buffered(1)),
            pl.BlockSpec((ROW_PAD, D_MODEL), const2),
        ],
        out_specs=pl.BlockSpec((1, TILE_T, D_MODEL), tile),
        out_shape=jax.ShapeDtypeStruct((bsz, seq, D_MODEL), F32),
        scratch_shapes=_SCAN_SCRATCH + [
            pltpu.VMEM((TILE_T, HG_WIDTH + CONV_WIDTH), BF16),
            pltpu.VMEM((TILE_T, 4 * CONV_WIDTH), F32),
        ],
        compiler_params=pltpu.CompilerParams(dimension_semantics=("arbitrary", "arbitrary"),
                                             vmem_limit_bytes=VMEM_LIMIT),
        name="fwd_scan_out",
    )(x, mod3, nw, wi, wi, wi, wi, wi, wi, lb, sf, q, v, ob, onw, conv_w, w_o, fnw)


def kernel(x, c, ctx, c_ctx, norm_w, w_ada, b_ada, w_in, hg_lb_logits, hg_onorm_w, conv_w,
           w_out, final_norm_w):
    assert w_in.shape[0] == 1, "single-layer block"
    bsz = x.shape[0]
    cc = jnp.concatenate([c, c_ctx[None, :], jnp.zeros((8 - bsz - 1, D_MODEL), F32)], axis=0)
    mod, lb = _mod_call(cc, w_ada[0], b_ada[0][None, :], hg_lb_logits.reshape(-1, HG_WIDTH))
    mod3 = mod.reshape(8, 3, D_MODEL)

    wi = w_in[0].astype(BF16)
    nw = _pad_rows(norm_w[0][None, :])

    sf, sb = _ctx_call(ctx, mod3, nw, wi, lb)
    q, v, ob = _bwd_call(x, mod3, nw, wi, lb, sb)
    onw = _pad_rows(jnp.tile(hg_onorm_w[0], HEADS)[None, :])
    return _fwd_call(x, mod3, nw, wi, lb, sf, q, v, ob, onw, conv_w[0],
                     w_out[0].astype(BF16), _pad_rows(final_norm_w[None, :]))
```
